```python
import math
import jax
import jax.numpy as jnp
from jax import lax
import numpy as np

D_MODEL = 2048
BATCH = 8
SEQ = 2048
DEPTH = 2

GRID_W = 64
CTX_LEN = 256
N_MIXERS = 2
N_RWKV = (DEPTH + 1) // 2
N_HYENA = DEPTH // 2
HEAD_SIZE = 64
N_HEADS = D_MODEL // HEAD_SIZE
DECAY_LORA = max(32, int(round(D_MODEL ** 0.5 * 1.8 / 32)) * 32)
ICLR_LORA = max(32, int(round(D_MODEL ** 0.5 * 1.8 / 32)) * 32)
GATE_LORA = max(32, int(round(D_MODEL ** 0.8 * 0.6 / 32)) * 32)
LNX_EPS = 64e-5
HY_ORDER = 2
HY_EMB_DIM = 33
HY_FILTER_WIDTH = 64
HY_INNER_MLPS = 2
HY_FAST_DECAY = 0.3
HY_SLOW_DECAY = 1.5
HY_DECAY_TARGET = 1e-2
D_FF = 4 * D_MODEL
N_MOD = 6
NORM_EPS = 1e-6

kernel_name = 'hybrid_rwkv7_hyena_prefix_dit'


def rms_norm(x, g):
    xf = x.astype(jnp.float32)
    y = xf * lax.rsqrt(jnp.mean(xf * xf, axis=-1, keepdims=True) + NORM_EPS)
    return (y * g.astype(jnp.float32)).astype(x.dtype)


def modulate(h, shift, scale):
    return h * (1.0 + scale) + shift


def to_heads(t):
    return t.reshape(t.shape[:-1] + (N_HEADS, HEAD_SIZE))


def shift_grid(x):
    B, L, D = x.shape
    rows = L // GRID_W
    q = D // 4
    p = jnp.pad(x.reshape(B, rows, GRID_W, D), ((0, 0), (1, 1), (1, 1), (0, 0)))
    s = jnp.concatenate([p[:, 1:-1, :-2, :q], p[:, 1:-1, 2:, q:2 * q],
                         p[:, :-2, 1:-1, 2 * q:3 * q], p[:, 2:, 1:-1, 3 * q:]], axis=-1)
    return s.reshape(B, L, D)


def shift_seq(x):
    half = x.shape[-1] // 2
    p = jnp.pad(x, ((0, 0), (1, 1), (0, 0)))
    return jnp.concatenate([p[:, :-2, :half], p[:, 2:, half:]], axis=-1)


def rwkv_keys(h, xx, mu, w_k, w_v, dec_w0, dec_w1, dec_w2, a0, a1, a2, k_k, k_a):
    f32 = jnp.float32
    B, L, D = h.shape
    hs = (B, L, N_HEADS, HEAD_SIZE)
    xw = h + xx * mu[1]
    xk = h + xx * mu[2]
    xv = h + xx * mu[3]
    xa = h + xx * mu[4]
    k = (xk @ w_k).astype(f32)
    v = (xv @ w_v).astype(f32)
    lw = dec_w0[:, None, None, :] + jnp.einsum('eblr,erd->ebld', jnp.tanh(jnp.einsum('bld,edr->eblr', xw, dec_w1)), dec_w2)
    lw = -jax.nn.softplus(-lw.astype(f32)) - 0.5
    decay = jnp.exp(-jnp.exp(lw))
    a = jax.nn.sigmoid((a0[:, None, None, :] + jnp.einsum('eblr,erd->ebld', jnp.einsum('bld,edr->eblr', xa, a1), a2)).astype(f32))
    kk = (k * k_k).reshape(hs)
    kk = kk * lax.rsqrt(jnp.maximum(jnp.sum(kk * kk, axis=-1, keepdims=True), 1e-24))
    kd = k * (1.0 + (a - 1.0) * k_a)
    b = kk * a.reshape((2,) + hs)
    return decay.reshape((2,) + hs), kd.reshape((2,) + hs), v.reshape(hs), kk, b


def wkv_scan(state, r, decay, k, v, kk, b, reverse):
    emit = r is not None
    seqs = (decay, k, v, kk, b) + ((r,) if emit else ())
    xs = tuple(jnp.moveaxis(s.astype(jnp.float32), 1, 0) for s in seqs)

    def step(S, inp):
        w_t, k_t, v_t, kk_t, b_t = inp[:5]
        S = (S * w_t[:, :, None, :]
             - jnp.einsum('bhvk,bhk->bhv', S, kk_t)[..., None] * b_t[:, :, None, :]
             + v_t[..., None] * k_t[:, :, None, :])
        out = jnp.einsum('bhvk,bhk->bhv', S, inp[5]) if emit else None
        return S, out

    S, out = lax.scan(step, state, xs, reverse=reverse)
    return S, (jnp.moveaxis(out, 0, 1) if emit else None)


def rwkv_readout(h, xx, r, o, kd, v, mu, g1, g2, r_k, lnx_w, lnx_b, w_o):
    B, L, D = h.shape
    mean = jnp.mean(o, axis=-1, keepdims=True)
    var = jnp.mean(jnp.square(o - mean), axis=-1, keepdims=True)
    o = ((o - mean) * lax.rsqrt(var + LNX_EPS)).reshape(B, L, D) * lnx_w + lnx_b
    bonus = jnp.sum(r[None] * kd * r_k, axis=(0, -1))[..., None] * v
    g = jax.nn.sigmoid((h + xx * mu[5]) @ g1) @ g2
    return ((o + bonus.reshape(B, L, D)).astype(h.dtype) * g) @ w_o


def rwkv_mixer(h, hc, emit_ctx, mu, w_r, w_k, w_v, w_o, dec_w0, dec_w1, dec_w2, a0, a1, a2,
               g1, g2, k_k, k_a, r_k, lnx_w, lnx_b):
    keyp = (mu, w_k, w_v, dec_w0, dec_w1, dec_w2, a0, a1, a2, k_k, k_a)
    readp = (mu, g1, g2, r_k, lnx_w, lnx_b, w_o)
    xx = shift_grid(h) - h
    xxc = shift_seq(hc) - hc
    dec, kd, v, kk, b = rwkv_keys(h, xx, *keyp)
    decc, kdc, vc, kkc, bc = rwkv_keys(hc, xxc, *keyp)
    r = to_heads((h + xx * mu[0]) @ w_r)
    rc = to_heads((hc + xxc * mu[0]) @ w_r) if emit_ctx else None
    S0 = jnp.zeros((h.shape[0], N_HEADS, HEAD_SIZE, HEAD_SIZE), jnp.float32)
    S_f, oc_f = wkv_scan(S0, rc, decc[0], kdc[0], vc, kkc, bc[0], False)
    S_b, oc_b = wkv_scan(S0, rc, decc[1], kdc[1], vc, kkc, bc[1], True)
    _, o_f = wkv_scan(S_f, r, dec[0], kd[0], v, kk, b[0], False)
    _, o_b = wkv_scan(S_b, r, dec[1], kd[1], v, kk, b[1], True)
    y = rwkv_readout(h, xx, r, o_f + o_b, kd, v, *readp)
    yc = rwkv_readout(hc, xxc, rc, oc_f + oc_b, kdc, vc, *readp) if emit_ctx else None
    return y, yc


def hyena_filters(L, D, f_w1, f_w23, f_w4, f_b, f_freq):
    f32 = jnp.float32
    t = jnp.linspace(0.0, 1.0, L, dtype=f32)[:, None]
    bands = (HY_EMB_DIM - 1) // 2
    freqs = jnp.linspace(1e-4, bands - 1, bands, dtype=f32)[None, :]
    ang = (2.0 * math.pi / L) * jnp.arange(L, dtype=f32)[:, None] * freqs
    z = jnp.concatenate([t, jnp.cos(ang), -jnp.sin(ang)], axis=-1)
    f_b = f_b.astype(f32)
    f_freq = f_freq.astype(f32)
    z = jnp.sin(f_freq[0] * (z @ f_w1.astype(f32) + f_b[0]))
    for m in range(HY_INNER_MLPS):
        z = jnp.sin(f_freq[m + 1] * (z @ f_w23[m].astype(f32) + f_b[m + 1]))
    filt = (z @ f_w4.astype(f32)).reshape(L, 2, HY_ORDER, D)
    max_decay = math.log(HY_DECAY_TARGET) / HY_FAST_DECAY
    min_decay = math.log(HY_DECAY_TARGET) / HY_SLOW_DECAY
    deltas = jnp.abs(jnp.linspace(min_decay, max_decay, D, dtype=f32))
    window = jnp.exp(-t * deltas)
    return filt * window[:, None, None, :]


def fft_long_conv(u, h_fwd, h_bwd):
    L = u.shape[1]
    kern = jnp.concatenate([h_fwd, jnp.zeros_like(h_fwd[:1]), h_bwd[:0:-1]], axis=0)
    U = jnp.fft.rfft(u, n=2 * L, axis=1)
    K = jnp.fft.rfft(kern, axis=0)
    return jnp.fft.irfft(U * K, n=2 * L, axis=1)[:, :L]


def hyena_stream(h, in_w, in_b, conv_w, conv_b, f_w1, f_w23, f_w4, f_b, f_freq, skip, out_w, out_b):
    B, L, D = h.shape
    z = h @ in_w + in_b
    p = jnp.pad(z, ((0, 0), (1, 1), (0, 0)))
    z = p[:, :-2] * conv_w[0] + p[:, 1:-1] * conv_w[1] + p[:, 2:] * conv_w[2] + conv_b
    v, x1, x2 = jnp.split(z, 3, axis=-1)
    filt = hyena_filters(L, D, f_w1, f_w23, f_w4, f_b, f_freq)
    skip = skip.astype(jnp.float32)
    y = v.astype(jnp.float32)
    for o, gate in enumerate((x1, x2)):
        y = gate.astype(jnp.float32) * (fft_long_conv(y, filt[:, 0, o], filt[:, 1, o]) + skip[o] * y)
    return y.astype(h.dtype) @ out_w + out_b


def squared_relu_mlp(h, w_up, w_down):
    return jnp.square(jax.nn.relu(h @ w_up)) @ w_down


def setup_inputs(seed: int = 0) -> dict:
    key = jax.random.key(seed)
    ks = iter(jax.random.split(key, 64))

    def nrm(shape, scale):
        return jax.random.normal(next(ks), shape, jnp.float32) * scale

    D, NA, NB = D_MODEL, N_RWKV, N_HYENA
    R, RA, RG, F, E = DECAY_LORA, ICLR_LORA, GATE_LORA, HY_FILTER_WIDTH, HY_EMB_DIM
    return {
        'x': nrm((BATCH, SEQ, D), 1.0),
        'c': nrm((BATCH, D), 1.0),
        'ctx': nrm((BATCH, CTX_LEN, D), 1.0),
        'c_ctx': nrm((D,), 1.0),
        'ada_w': nrm((DEPTH, D, N_MOD * D), D ** -0.5),
        'ada_b': nrm((DEPTH, N_MOD * D), 0.02),
        'norm_g': 1.0 + nrm((DEPTH, 4, D), 0.1),
        'mlp_up': nrm((DEPTH, D, D_FF), D ** -0.5),
        'mlp_down': nrm((DEPTH, D_FF, D), D_FF ** -0.5),
        'rw_mu': jax.random.uniform(next(ks), (NA, 6, D), jnp.float32),
        'rw_w_r': nrm((NA, D, D), D ** -0.5),
        'rw_w_k': nrm((NA, D, D), D ** -0.5),
        'rw_w_v': nrm((NA, D, D), D ** -0.5),
        'rw_w_o': nrm((NA, D, D), D ** -0.5),
        'rw_dec_w0': jnp.linspace(-6.0, -1.0, D, dtype=jnp.float32) + nrm((NA, 2, D), 0.3),
        'rw_dec_w1': nrm((NA, 2, D, R), D ** -0.5),
        'rw_dec_w2': nrm((NA, 2, R, D), 0.1 * R ** -0.5),
        'rw_a0': nrm((NA, 2, D), 0.3),
        'rw_a1': nrm((NA, 2, D, RA), D ** -0.5),
        'rw_a2': nrm((NA, 2, RA, D), 0.1 * RA ** -0.5),
        'rw_g1': nrm((NA, D, RG), D ** -0.5),
        'rw_g2': nrm((NA, RG, D), RG ** -0.5),
        'rw_k_k': 0.85 + nrm((NA, D), 0.05),
        'rw_k_a': 1.0 + nrm((NA, D), 0.05),
        'rw_r_k': nrm((NA, N_HEADS, HEAD_SIZE), 0.1),
        'rw_lnx_w': 1.0 + nrm((NA, D), 0.1),
        'rw_lnx_b': nrm((NA, D), 0.02),
        'hy_in_w': nrm((NB, D, 3 * D), D ** -0.5),
        'hy_in_b': nrm((NB, 3 * D), 0.02),
        'hy_conv_w': nrm((NB, 3, 3 * D), 3 ** -0.5),
        'hy_conv_b': nrm((NB, 3 * D), 0.02),
        'hy_f_w1': nrm((NB, E, F), E ** -0.5),
        'hy_f_w23': nrm((NB, HY_INNER_MLPS, F, F), F ** -0.5),
        'hy_f_w4': nrm((NB, F, 2 * HY_ORDER * D), F ** -0.5),
        'hy_f_b': nrm((NB, HY_INNER_MLPS + 1, F), 0.1),
        'hy_f_freq': 1.0 + nrm((NB, HY_INNER_MLPS + 1, F), 0.1),
        'hy_skip': nrm((NB, HY_ORDER, D), 1.0),
        'hy_out_w': nrm((NB, D, D), D ** -0.5),
        'hy_out_b': nrm((NB, D), 0.02),
    }


def reference(x, c, ctx, c_ctx, ada_w, ada_b, norm_g, mlp_up, mlp_down,
              rw_mu, rw_w_r, rw_w_k, rw_w_v, rw_w_o, rw_dec_w0, rw_dec_w1, rw_dec_w2,
              rw_a0, rw_a1, rw_a2, rw_g1, rw_g2, rw_k_k, rw_k_a, rw_r_k, rw_lnx_w, rw_lnx_b,
              hy_in_w, hy_in_b, hy_conv_w, hy_conv_b, hy_f_w1, hy_f_w23, hy_f_w4, hy_f_b,
              hy_f_freq, hy_skip, hy_out_w, hy_out_b):
    B, _, D = x.shape
    silu_c = jax.nn.silu(c)
    silu_cc = jax.nn.silu(c_ctx)
    xc = ctx
    for i in range(DEPTH):
        kind, j = i % N_MIXERS, i // N_MIXERS
        ctx_live = any(l % N_MIXERS == 0 for l in range(i + 1, DEPTH))
        mod = (silu_c @ ada_w[i] + ada_b[i]).reshape(B, N_MOD, 1, D)
        modc = (silu_cc @ ada_w[i] + ada_b[i]).reshape(N_MOD, D)
        h = modulate(rms_norm(x, norm_g[i, 0]), mod[:, 0], mod[:, 1])
        if kind == 0:
            hc = modulate(rms_norm(xc, norm_g[i, 0]), modc[0], modc[1])
            y, yc = rwkv_mixer(h, hc, ctx_live, rw_mu[j], rw_w_r[j], rw_w_k[j], rw_w_v[j], rw_w_o[j],
                               rw_dec_w0[j], rw_dec_w1[j], rw_dec_w2[j], rw_a0[j], rw_a1[j], rw_a2[j],
                               rw_g1[j], rw_g2[j], rw_k_k[j], rw_k_a[j], rw_r_k[j],
                               rw_lnx_w[j], rw_lnx_b[j])
        else:
            hyp = (hy_in_w[j], hy_in_b[j], hy_conv_w[j], hy_conv_b[j], hy_f_w1[j], hy_f_w23[j],
                   hy_f_w4[j], hy_f_b[j], hy_f_freq[j], hy_skip[j], hy_out_w[j], hy_out_b[j])
            y = hyena_stream(h, *hyp)
            if ctx_live:
                hc = modulate(rms_norm(xc, norm_g[i, 0]), modc[0], modc[1])
                yc = hyena_stream(hc, *hyp)
        x = x + mod[:, 2] * rms_norm(y, norm_g[i, 1])
        h = modulate(rms_norm(x, norm_g[i, 2]), mod[:, 3], mod[:, 4])
        x = x + mod[:, 5] * rms_norm(squared_relu_mlp(h, mlp_up[i], mlp_down[i]), norm_g[i, 3])
        if ctx_live:
            xc = xc + modc[2] * rms_norm(yc, norm_g[i, 1])
            hc = modulate(rms_norm(xc, norm_g[i, 2]), modc[3], modc[4])
            xc = xc + modc[5] * rms_norm(squared_relu_mlp(hc, mlp_up[i], mlp_down[i]), norm_g[i, 3])
    return x
```

```python
import functools
import math

import jax
import jax.numpy as jnp
from jax import lax
from jax.experimental import pallas as pl
from jax.experimental.pallas import tpu as pltpu

F32 = jnp.float32
BF16 = jnp.bfloat16

HEAD_SIZE = 64
GRID_W = 64
N_MOD = 6
NORM_EPS = 1e-6
LNX_EPS = 64e-5
HY_FAST_DECAY = 0.3
HY_SLOW_DECAY = 1.5
HY_DECAY_TARGET = 1e-2
HY_EMB_DIM = 33
MOD_ROWS = 16
CHUNK = 64
LANES = 128
TOKEN_TILE = 256
VMEM_LIMIT_CAP = 60000 * 1024


def _cparams(sem, est_bytes):
    limit = int(min(max(2 * est_bytes, 32 * 1024 * 1024), VMEM_LIMIT_CAP))
    return pltpu.CompilerParams(dimension_semantics=sem, vmem_limit_bytes=limit)


def _largest_tile(n, cap, align):
    t = min(cap, n) // align * align
    while t > align and n % t:
        t -= align
    assert t > 0 and n % t == 0, (n, cap, align)
    return t


def _split3(x):
    hi = x.astype(BF16)
    r1 = x - hi.astype(F32)
    mid = r1.astype(BF16)
    lo = (r1 - mid.astype(F32)).astype(BF16)
    return hi, mid, lo


def _rms(x):
    return x * lax.rsqrt(jnp.mean(x * x, axis=-1, keepdims=True) + NORM_EPS)


def _norm_mod(x, g, shift, scale):
    return (_rms(x) * g) * (1.0 + scale) + shift


def _softplus(y):
    return jnp.maximum(y, 0.0) + jnp.log1p(jnp.exp(-jnp.abs(y)))


def _ada_kernel(c_ref, w_ref, b_ref, o_ref):
    c = c_ref[...]
    s = c * jax.nn.sigmoid(c)
    s_hi = s.astype(BF16)
    s_lo = (s - s_hi.astype(F32)).astype(BF16)
    w = w_ref[0]
    w_hi = w.astype(BF16)
    w_lo = (w - w_hi.astype(F32)).astype(BF16)
    p = jnp.dot(jnp.concatenate([s_hi, s_lo], axis=0), w_hi, preferred_element_type=F32)
    q = jnp.dot(s_hi, w_lo, preferred_element_type=F32)
    o_ref[0] = p[:MOD_ROWS] + p[MOD_ROWS:] + q + b_ref[0]


def _ada_mod(c_rows, ada_w, ada_b):
    depth, d, n = ada_w.shape
    tn = 1024
    out = pl.pallas_call(
        _ada_kernel,
        out_shape=jax.ShapeDtypeStruct((depth, MOD_ROWS, n), F32),
        grid=(depth, n // tn),
        in_specs=[pl.BlockSpec((MOD_ROWS, d), lambda l, j: (0, 0)),
                  pl.BlockSpec((1, d, tn), lambda l, j: (l, 0, j)),
                  pl.BlockSpec((1, 1, tn), lambda l, j: (l, 0, j))],
        out_specs=pl.BlockSpec((1, MOD_ROWS, tn), lambda l, j: (l, 0, j)),
        compiler_params=_cparams(("parallel", "parallel"), 2 * d * tn * 4 + 4 * d * tn),
        name="ada_mod",
    )(c_rows, ada_w, ada_b.reshape(depth, 1, n))
    return out.reshape(depth * MOD_ROWS * N_MOD, 1, d)


def _mod_row(layer, row, j):
    return (layer * MOD_ROWS + row) * N_MOD + j


def _pre_rwkv_kernel(hc_ref, sc_ref_ctx, xm_ref, xu_ref, xd_ref, g_ref, sh_ref, sc_ref, mu_ref,
                     *o_refs, n_tiles):
    t = pl.program_id(1)
    d = xm_ref.shape[-1]
    g = g_ref[...]
    mu = mu_ref[...]

    def emit(h, s, lo, hi):
        xx = s - h
        for j in range(6):
            o_refs[j][0, :, lo:hi] = (h + xx * mu[j:j + 1, lo:hi]).astype(BF16)

    @pl.when(t == 0)
    def _():
        emit(hc_ref[0], sc_ref_ctx[0], 0, d)

    @pl.when(t > 0)
    def _():
        sh = sh_ref[0]
        sc = sc_ref[0]
        xm = xm_ref[0].reshape(TOKEN_TILE, d)
        hm = _norm_mod(xm, g, sh, sc)
        hu = _norm_mod(xu_ref[0, 0], g, sh, sc) * jnp.where(t > 1, 1.0, 0.0)
        hd = _norm_mod(xd_ref[0, 0], g, sh, sc) * jnp.where(t < n_tiles, 1.0, 0.0)
        col = lax.broadcasted_iota(jnp.int32, (TOKEN_TILE, 1), 0) & (GRID_W - 1)
        q = d // 4
        left = jnp.where(col != 0, pltpu.roll(hm[:, :q], 1, 0), 0.0)
        right = jnp.where(col != GRID_W - 1, pltpu.roll(hm[:, q:2 * q], TOKEN_TILE - 1, 0), 0.0)
        up = jnp.concatenate([hu[:, 2 * q:3 * q], hm[:TOKEN_TILE - GRID_W, 2 * q:3 * q]], axis=0)
        down = jnp.concatenate([hm[GRID_W:, 3 * q:], hd[:, 3 * q:]], axis=0)
        emit(hm[:, :q], left, 0, q)
        emit(hm[:, q:2 * q], right, q, 2 * q)
        emit(hm[:, 2 * q:3 * q], up, 2 * q, 3 * q)
        emit(hm[:, 3 * q:], down, 3 * q, d)


def _shift_seq(x):
    half = x.shape[-1] // 2
    p = jnp.pad(x, ((0, 0), (1, 1), (0, 0)))
    return jnp.concatenate([p[:, :-2, :half], p[:, 2:, half:]], axis=-1)


def _pre_rwkv(x, ctx, mod3, layer, g0, mu):
    b, l, d = x.shape
    n_ctx = ctx.shape[1]
    assert n_ctx == TOKEN_TILE and l % TOKEN_TILE == 0
    n_tiles = l // TOKEN_TILE
    rows_per_tile = TOKEN_TILE // GRID_W
    n_rows = l // GRID_W
    x4 = x.reshape(b, n_rows, GRID_W, d)
    s = n_ctx + l
    hc = _pre_norm(ctx.reshape(b * n_ctx, d), mod3, layer, g0, n_ctx, tm=n_ctx, fixed_row=b,
                   out_dtype=F32).reshape(b, n_ctx, d)
    sc = _shift_seq(hc)

    def mrow(j):
        return pl.BlockSpec((1, 1, d), lambda bi, t: (_mod_row(layer, bi, j), 0, 0))

    main = lambda bi, t: (bi, jnp.maximum(t - 1, 0), 0, 0)
    up = lambda bi, t: (bi, jnp.maximum((t - 1) * rows_per_tile - 1, 0), 0, 0)
    down = lambda bi, t: (bi, jnp.minimum(jnp.maximum(t, 1) * rows_per_tile, n_rows - 1), 0, 0)
    out_sds = jax.ShapeDtypeStruct((b, s, d), BF16)
    outs = pl.pallas_call(
        functools.partial(_pre_rwkv_kernel, n_tiles=n_tiles),
        out_shape=[out_sds] * 6,
        grid=(b, n_tiles + 1),
        in_specs=[pl.BlockSpec((1, n_ctx, d), lambda bi, t: (bi, 0, 0)),
                  pl.BlockSpec((1, n_ctx, d), lambda bi, t: (bi, 0, 0)),
                  pl.BlockSpec((1, rows_per_tile, GRID_W, d), main),
                  pl.BlockSpec((1, 1, GRID_W, d), up),
                  pl.BlockSpec((1, 1, GRID_W, d), down),
                  pl.BlockSpec((1, d), lambda bi, t: (0, 0)),
                  mrow(0), mrow(1),
                  pl.BlockSpec((6, d), lambda bi, t: (0, 0))],
        out_specs=[pl.BlockSpec((1, TOKEN_TILE, d), lambda bi, t: (bi, t, 0))] * 6,
        compiler_params=_cparams(("parallel", "arbitrary"),
                                 2 * (3 * TOKEN_TILE * d * 4 + 2 * GRID_W * d * 4 + 6 * TOKEN_TILE * d * 2)),
        name="rwkv_pre",
    )(hc, sc, x4, x4, x4, g0.reshape(1, d), mod3, mod3, mu)
    return outs


def _mm_kernel(a_ref, w_ref, *rest, act, has_bias):
    o_ref = rest[-1]
    acc = jnp.dot(a_ref[...], w_ref[...], preferred_element_type=F32)
    if has_bias:
        acc = acc + rest[0][...]
    if act == "tanh":
        acc = jnp.tanh(acc)
    elif act == "sigmoid":
        acc = jax.nn.sigmoid(acc)
    o_ref[...] = acc.astype(o_ref.dtype)


def _matmul(a, w, bias=None, act=None, out_dtype=F32, tm=1024, tn=1024, name="matmul"):
    m, k = a.shape
    n = w.shape[1]
    tm = _largest_tile(m, tm, 8)
    tn = _largest_tile(n, tn, LANES)
    in_specs = [pl.BlockSpec((tm, k), lambda i, j: (i, 0)),
                pl.BlockSpec((k, tn), lambda i, j: (0, j))]
    args = [a, w]
    if bias is not None:
        in_specs.append(pl.BlockSpec((1, tn), lambda i, j: (0, j)))
        args.append(bias.reshape(1, n).astype(F32))
    est = 2 * (tm * k * 2 + k * tn * 2 + tm * tn * jnp.dtype(out_dtype).itemsize) + tm * tn * 4
    return pl.pallas_call(
        functools.partial(_mm_kernel, act=act, has_bias=bias is not None),
        out_shape=jax.ShapeDtypeStruct((m, n), out_dtype),
        grid=(m // tm, n // tn),
        in_specs=in_specs,
        out_specs=pl.BlockSpec((tm, tn), lambda i, j: (i, j)),
        compiler_params=_cparams(("parallel", "parallel"), est),
        name=name,
    )(*args)


def _seg_sum(x, ones_bd):
    hi, mid, lo = _split3(x)
    r = x.shape[0]
    p = jnp.dot(jnp.concatenate([hi, mid, lo], axis=0), ones_bd, preferred_element_type=F32)
    return p[:r] + p[r:2 * r] + p[2 * r:]


def _bd(x, head0):
    return jnp.concatenate([jnp.where(head0, x, 0.0), jnp.where(head0, 0.0, x)], axis=0)


def _wkv_constants(rev):
    c = CHUNK
    shift = int(math.log2(c))
    head0 = lax.broadcasted_iota(jnp.int32, (1, LANES), 1) < HEAD_SIZE
    row2 = lax.broadcasted_iota(jnp.int32, (2 * c, 2 * c), 0)
    col2 = lax.broadcasted_iota(jnp.int32, (2 * c, 2 * c), 1)
    same = (row2 >> shift) == (col2 >> shift)
    tt = row2 & (c - 1)
    ss = col2 & (c - 1)
    if rev:
        tt, ss = ss, tt
    ones_bd = jnp.where(same, 1.0, 0.0).astype(BF16)
    eye = jnp.where(row2 == col2, 1.0, 0.0).astype(F32)
    rowc = lax.broadcasted_iota(jnp.int32, (c, c), 0)
    colc = lax.broadcasted_iota(jnp.int32, (c, c), 1)
    tri = jnp.where(colc >= rowc if rev else colc <= rowc, 1.0, 0.0).astype(BF16)
    merge = tuple(((tt >> lv) == (ss >> lv) + 1) & ((tt >> (lv + 1)) == (ss >> (lv + 1))) for lv in range(shift))
    return head0, ones_bd, tri, ss < tt, ss <= tt, eye, merge


def _wkv_chunk(rev, r, k, v, lwx, ax, w0, a0, k_k, k_a, h_state, cst):
    head0, ones_bd, tri_cum, strict, incl, eye, merge = cst
    c = r.shape[0]
    logw = -jnp.exp(-_softplus(-(w0 + lwx)) - 0.5)
    a = jax.nn.sigmoid(a0 + ax)
    kk0 = k * k_k
    kk = kk0 * lax.rsqrt(jnp.maximum(_seg_sum(kk0 * kk0, ones_bd), 1e-24))
    kd = k * (1.0 + (a - 1.0) * k_a)
    b = kk * a

    hi, mid, lo = _split3(logw)
    cum3 = jnp.dot(tri_cum, jnp.concatenate([hi, mid, lo], axis=1), preferred_element_type=F32)
    cum = cum3[:, :LANES] + cum3[:, LANES:2 * LANES] + cum3[:, 2 * LANES:]
    total = cum[0:1] if rev else cum[c - 1:c]
    gam = jnp.exp(cum)
    igam = jnp.exp(-cum)
    kap_t = kk * jnp.exp(cum - logw)
    r_t = r * gam
    k_h = kd * igam
    b_h = b * igam
    tail = jnp.exp(total - cum)
    k_c = kd * tail
    b_c = b * tail
    g_c = jnp.exp(total)

    bd_kap = _bd(kap_t, head0)
    bd_r = _bd(r_t, head0)
    lhs = jnp.concatenate([bd_kap, bd_r], axis=0).astype(BF16)
    rhs = jnp.concatenate([_bd(k_h, head0), _bd(b_h, head0)], axis=0).astype(BF16)
    gmat = lax.dot_general(lhs, rhs, (((1,), (1,)), ((), ())), preferred_element_type=F32)
    c2 = 2 * c
    a_kk = jnp.where(strict, gmat[:c2, :c2], 0.0)
    a_kb = jnp.where(strict, gmat[:c2, c2:], 0.0)
    a_rk = jnp.where(incl, gmat[c2:, :c2], 0.0)
    a_rb = jnp.where(incl, gmat[c2:, c2:], 0.0)

    tinv = eye - jnp.where(merge[0], a_kb, 0.0)
    for msk in merge[1:]:
        xb = tinv.astype(BF16)
        y = jnp.dot(jnp.where(msk, a_kb, 0.0).astype(BF16), xb, preferred_element_type=F32)
        tinv = tinv - jnp.dot(xb, y.astype(BF16), preferred_element_type=F32)

    bd_v = _bd(v, head0).astype(BF16)
    av = jnp.dot(jnp.concatenate([a_kk, a_rk], axis=0).astype(BF16), bd_v, preferred_element_type=F32)
    x0 = jnp.concatenate([bd_kap, av[:c2]], axis=1).astype(BF16)
    wu = jnp.dot(tinv.astype(BF16), x0, preferred_element_type=F32)
    wub = wu.astype(BF16)
    rb = jnp.dot(a_rb.astype(BF16), wub, preferred_element_type=F32)
    rk = bd_r - rb[:, :LANES]
    ov = av[c2:] - rb[:, LANES:]
    bct = _bd(b_c, head0).T.astype(BF16)
    kct = _bd(k_c, head0).T.astype(BF16)
    mn = jnp.dot(bct, wub, preferred_element_type=F32)
    m_mat = jnp.where(eye > 0.0, g_c, 0.0) - mn[:, :LANES]
    n_mat = jnp.dot(kct, bd_v, preferred_element_type=F32) - mn[:, LANES:]

    rm = jnp.dot(jnp.concatenate([rk, m_mat], axis=0).astype(BF16), h_state.astype(BF16),
                 preferred_element_type=F32)
    o_bd = rm[:c2] + ov
    h_new = rm[c2:] + n_mat
    return o_bd[:c] + o_bd[c:], h_new


def _wkv_kernel(r_ref, k_ref, v_ref, lwf_ref, lwb_ref, af_ref, ab_ref, g_ref,
                w0_ref, a0_ref, kk_ref, ka_ref, rk_ref, lnw_ref, lnb_ref,
                z_ref, of_scr, ob_scr, *, n_ctx_chunks, n_chunks):
    c = CHUNK
    cst_f = _wkv_constants(False)
    cst_b = _wkv_constants(True)
    ones_bd = cst_f[1]

    w0f, w0b = w0_ref[0:1, :], w0_ref[1:2, :]
    a0f, a0b = a0_ref[0:1, :], a0_ref[1:2, :]
    k_k = kk_ref[...]
    k_a = ka_ref[...]

    def load(ref, rows):
        return ref[0, rows, :].astype(F32)

    def body(i, carry):
        h_f, h_b = carry
        cf = i
        cb = jnp.where(i < n_ctx_chunks, n_ctx_chunks - 1 - i, n_chunks + n_ctx_chunks - 1 - i)
        rows_f = pl.ds(pl.multiple_of(cf * c, c), c)
        rows_b = pl.ds(pl.multiple_of(cb * c, c), c)
        o_f, h_f = _wkv_chunk(False, load(r_ref, rows_f), load(k_ref, rows_f), load(v_ref, rows_f),
                              load(lwf_ref, rows_f), load(af_ref, rows_f), w0f, a0f, k_k, k_a, h_f, cst_f)
        o_b, h_b = _wkv_chunk(True, load(r_ref, rows_b), load(k_ref, rows_b), load(v_ref, rows_b),
                              load(lwb_ref, rows_b), load(ab_ref, rows_b), w0b, a0b, k_k, k_a, h_b, cst_b)
        of_scr[rows_f, :] = o_f
        ob_scr[rows_b, :] = o_b
        return h_f, h_b

    zero = jnp.zeros((2 * c, LANES), F32)
    lax.fori_loop(0, n_chunks, body, (zero, zero))

    r_k = rk_ref[...]
    lnw = lnw_ref[...]
    lnb = lnb_ref[...]
    blk = TOKEN_TILE
    n_ctx = n_ctx_chunks * c
    inv_n = 1.0 / HEAD_SIZE

    def read_body(j, _):
        rows = pl.ds(pl.multiple_of(n_ctx + j * blk, blk), blk)
        o = of_scr[rows, :] + ob_scr[rows, :]
        mean = _seg_sum(o, ones_bd) * inv_n
        dev = o - mean
        var = _seg_sum(dev * dev, ones_bd) * inv_n
        on = dev * lax.rsqrt(var + LNX_EPS) * lnw + lnb
        r = load(r_ref, rows)
        k = load(k_ref, rows)
        v = load(v_ref, rows)
        a_sum = jax.nn.sigmoid(a0f + load(af_ref, rows)) + jax.nn.sigmoid(a0b + load(ab_ref, rows))
        kd_sum = k * (2.0 + (a_sum - 2.0) * k_a)
        bonus = _seg_sum(r * kd_sum * r_k, ones_bd) * v
        z = (on + bonus) * g_ref[0, rows, :]
        z_ref[0, pl.ds(pl.multiple_of(j * blk, blk), blk), :] = z.astype(z_ref.dtype)
        return 0

    lax.fori_loop(0, (n_chunks - n_ctx_chunks) * c // blk, read_body, 0)


def _wkv(r, k, v, lw, a, g, dec_w0, a0, k_k, k_a, r_k, lnx_w, lnx_b, n_ctx):
    b, s, d = r.shape
    l = s - n_ctx
    npair = d // LANES
    seq = lambda bi, p: (bi, 0, p)
    seq_b = lambda bi, p: (bi, 0, p + npair)
    vec = lambda bi, p: (0, p)
    sblk = pl.BlockSpec((1, s, LANES), seq)
    est = 2 * (3 * s * LANES * 2 + 5 * s * LANES * 4 + l * LANES * 2) + 2 * s * LANES * 4
    return pl.pallas_call(
        functools.partial(_wkv_kernel, n_ctx_chunks=n_ctx // CHUNK, n_chunks=s // CHUNK),
        out_shape=jax.ShapeDtypeStruct((b, l, d), BF16),
        grid=(b, npair),
        in_specs=[sblk, sblk, sblk,
                  pl.BlockSpec((1, s, LANES), seq), pl.BlockSpec((1, s, LANES), seq_b),
                  pl.BlockSpec((1, s, LANES), seq), pl.BlockSpec((1, s, LANES), seq_b),
                  pl.BlockSpec((1, s, LANES), seq),
                  pl.BlockSpec((2, LANES), vec), pl.BlockSpec((2, LANES), vec),
                  pl.BlockSpec((1, LANES), vec), pl.BlockSpec((1, LANES), vec), pl.BlockSpec((1, LANES), vec),
                  pl.BlockSpec((1, LANES), vec), pl.BlockSpec((1, LANES), vec)],
        out_specs=pl.BlockSpec((1, l, LANES), seq),
        scratch_shapes=[pltpu.VMEM((s, LANES), F32), pltpu.VMEM((s, LANES), F32)],
        compiler_params=_cparams(("parallel", "parallel"), est),
        name="wkv_scan",
    )(r, k, v, lw, lw, a, a, g, dec_w0, a0, k_k.reshape(1, d), k_a.reshape(1, d), r_k.reshape(1, d),
      lnx_w.reshape(1, d), lnx_b.reshape(1, d))


def _post_kernel(x_ref, y_ref, g1_ref, g2_ref, gate_ref, sh_ref, sc_ref, xo_ref, h_ref):
    x = x_ref[...] + gate_ref[0] * (_rms(y_ref[...]) * g1_ref[...])
    xo_ref[...] = x
    h_ref[...] = _norm_mod(x, g2_ref[...], sh_ref[0], sc_ref[0]).astype(h_ref.dtype)


def _post_mixer(x2, y2, mod3, layer, g1, g2, l, tm=512):
    m, d = x2.shape
    per_b = l // tm
    row = lambda j: pl.BlockSpec((1, 1, d), lambda i: (_mod_row(layer, i // per_b, j), 0, 0))
    tile = pl.BlockSpec((tm, d), lambda i: (i, 0))
    vec = pl.BlockSpec((1, d), lambda i: (0, 0))
    return pl.pallas_call(
        _post_kernel,
        out_shape=[jax.ShapeDtypeStruct((m, d), F32), jax.ShapeDtypeStruct((m, d), BF16)],
        grid=(m // tm,),
        in_specs=[tile, tile, vec, vec, row(2), row(3), row(4)],
        out_specs=[tile, tile],
        compiler_params=_cparams(("parallel",), 2 * tm * d * (4 + 4 + 4 + 2)),
        name="post_mixer",
    )(x2, y2, g1.reshape(1, d), g2.reshape(1, d), mod3, mod3, mod3)


def _mlp_kernel(h_ref, wu_ref, wd_ref, x_ref, g_ref, gate_ref, o_ref, acc_ref):
    kf = pl.program_id(1)

    @pl.when(kf == 0)
    def _():
        acc_ref[...] = jnp.zeros_like(acc_ref)

    u = jnp.dot(h_ref[...], wu_ref[...], preferred_element_type=F32)
    u = jnp.square(jnp.maximum(u, 0.0)).astype(BF16)
    acc_ref[...] += jnp.dot(u, wd_ref[...], preferred_element_type=F32)

    @pl.when(kf == pl.num_programs(1) - 1)
    def _():
        o_ref[...] = x_ref[...] + gate_ref[0] * (_rms(acc_ref[...]) * g_ref[...])


def _mlp(h2, w_up, w_down, x2, mod3, layer, g3, l, tm=512, tf=1024):
    m, d = h2.shape
    dff = w_up.shape[1]
    per_b = l // tm
    est = 2 * (tm * d * 2 + 2 * d * tf * 2 + 2 * tm * d * 4) + tm * d * 4 + tm * tf * 6
    return pl.pallas_call(
        _mlp_kernel,
        out_shape=jax.ShapeDtypeStruct((m, d), F32),
        grid=(m // tm, dff // tf),
        in_specs=[pl.BlockSpec((tm, d), lambda i, f: (i, 0)),
                  pl.BlockSpec((d, tf), lambda i, f: (0, f)),
                  pl.BlockSpec((tf, d), lambda i, f: (f, 0)),
                  pl.BlockSpec((tm, d), lambda i, f: (i, 0)),
                  pl.BlockSpec((1, d), lambda i, f: (0, 0)),
                  pl.BlockSpec((1, 1, d), lambda i, f: (_mod_row(layer, i // per_b, 5), 0, 0))],
        out_specs=pl.BlockSpec((tm, d), lambda i, f: (i, 0)),
        scratch_shapes=[pltpu.VMEM((tm, d), F32)],
        compiler_params=_cparams(("parallel", "arbitrary"), est),
        name="mlp",
    )(h2, w_up, w_down, x2, g3.reshape(1, d), mod3)


def _pre_norm_kernel(x_ref, g_ref, sh_ref, sc_ref, h_ref):
    h_ref[...] = _norm_mod(x_ref[...], g_ref[...], sh_ref[0], sc_ref[0]).astype(h_ref.dtype)


def _pre_norm(x2, mod3, layer, g0, l, tm=512, fixed_row=None, out_dtype=BF16):
    m, d = x2.shape
    per_b = l // tm
    if fixed_row is None:
        row = lambda j: pl.BlockSpec((1, 1, d), lambda i: (_mod_row(layer, i // per_b, j), 0, 0))
    else:
        row = lambda j: pl.BlockSpec((1, 1, d), lambda i: (_mod_row(layer, fixed_row, j), 0, 0))
    tile = pl.BlockSpec((tm, d), lambda i: (i, 0))
    return pl.pallas_call(
        _pre_norm_kernel,
        out_shape=jax.ShapeDtypeStruct((m, d), out_dtype),
        grid=(m // tm,),
        in_specs=[tile, pl.BlockSpec((1, d), lambda i: (0, 0)), row(0), row(1)],
        out_specs=tile,
        compiler_params=_cparams(("parallel",), 2 * tm * d * 6),
        name="pre_norm",
    )(x2, g0.reshape(1, d), mod3, mod3)


def _filter_kernel(z_ref, w1_ref, w2_ref, w3_ref, b_ref, fr_ref, w4_ref, t_ref, dl_ref, o_ref):
    hp = lax.Precision.HIGHEST
    b = b_ref[...]
    fr = fr_ref[...]
    z = jnp.sin(fr[0:1] * (jnp.dot(z_ref[...], w1_ref[...], precision=hp, preferred_element_type=F32) + b[0:1]))
    z = jnp.sin(fr[1:2] * (jnp.dot(z, w2_ref[...], precision=hp, preferred_element_type=F32) + b[1:2]))
    z = jnp.sin(fr[2:3] * (jnp.dot(z, w3_ref[...], precision=hp, preferred_element_type=F32) + b[2:3]))
    filt = jnp.dot(z, w4_ref[...], precision=hp, preferred_element_type=F32)
    o_ref[...] = filt * jnp.exp(-t_ref[...] * dl_ref[...])


def _hyena_filters(l, d, f_w1, f_w23, f_w4, f_b, f_freq):
    t = jnp.linspace(0.0, 1.0, l, dtype=F32)[:, None]
    bands = (HY_EMB_DIM - 1) // 2
    freqs = jnp.linspace(1e-4, bands - 1, bands, dtype=F32)[None, :]
    ang = (2.0 * math.pi / l) * jnp.arange(l, dtype=F32)[:, None] * freqs
    z = jnp.concatenate([t, jnp.cos(ang), -jnp.sin(ang)], axis=-1)
    e, f = f_w1.shape
    pad = lambda a_, r, c: jnp.pad(a_.astype(F32), ((0, r - a_.shape[0]), (0, c - a_.shape[1])))
    zp = pad(z, l, LANES)
    w1 = pad(f_w1, LANES, LANES)
    w2 = pad(f_w23[0], LANES, LANES)
    w3 = pad(f_w23[1], LANES, LANES)
    bb = pad(f_b, 8, LANES)
    fr = pad(f_freq, 8, LANES)
    n = f_w4.shape[1]
    w4 = pad(f_w4, LANES, n)
    max_decay = math.log(HY_DECAY_TARGET) / HY_FAST_DECAY
    min_decay = math.log(HY_DECAY_TARGET) / HY_SLOW_DECAY
    deltas = jnp.abs(jnp.linspace(min_decay, max_decay, d, dtype=F32))[None, :]
    tn = _largest_tile(d, 1024, LANES)
    per_d = d // tn
    sq = pl.BlockSpec((LANES, LANES), lambda j: (0, 0))
    small = pl.BlockSpec((8, LANES), lambda j: (0, 0))
    return pl.pallas_call(
        _filter_kernel,
        out_shape=jax.ShapeDtypeStruct((l, n), F32),
        grid=(n // tn,),
        in_specs=[pl.BlockSpec((l, LANES), lambda j: (0, 0)), sq, sq, sq, small, small,
                  pl.BlockSpec((LANES, tn), lambda j: (0, j)),
                  pl.BlockSpec((l, 1), lambda j: (0, 0)),
                  pl.BlockSpec((1, tn), lambda j: (0, j % per_d))],
        out_specs=pl.BlockSpec((l, tn), lambda j: (0, j)),
        compiler_params=_cparams(("parallel",), 4 * l * tn * 4),
        name="hyena_filters",
    )(zp, w1, w2, w3, bb, fr, w4, t, deltas)


def _dft_matrices(l):
    n = 2 * l
    k = jnp.arange(l, dtype=jnp.int32)[:, None]
    t = jnp.arange(l, dtype=jnp.int32)[None, :]
    ang = ((k * t) % n).astype(F32) * (2.0 * math.pi / n)
    cos = jnp.cos(ang)
    msin = jnp.where(k == 0, jnp.where(t % 2 == 0, 1.0, -1.0), -jnp.sin(ang))
    fwd = jnp.concatenate([cos, msin], axis=0)
    return fwd.astype(BF16), fwd.T.astype(BF16)


def _short_conv(z, cw, cb):
    n = z.shape[0]
    row = lax.broadcasted_iota(jnp.int32, (n, 1), 0)
    prev = jnp.where(row != 0, pltpu.roll(z, 1, 0), 0.0)
    nxt = jnp.where(row != n - 1, pltpu.roll(z, n - 1, 0), 0.0)
    return prev * cw[0:1] + z * cw[1:2] + nxt * cw[2:3] + cb


def _spec_kernel(f_ref, y_ref, o_ref):
    o_ref[0] = jnp.dot(f_ref[...], y_ref[0].astype(BF16), preferred_element_type=F32)


def _spectrum(fwd, y, tn=256):
    nb, l, c = y.shape
    n2 = fwd.shape[0]
    return pl.pallas_call(
        _spec_kernel,
        out_shape=jax.ShapeDtypeStruct((nb, n2, c), F32),
        grid=(c // tn, nb),
        in_specs=[pl.BlockSpec((n2, l), lambda j, bi: (0, 0), pipeline_mode=pl.Buffered(1)),
                  pl.BlockSpec((1, l, tn), lambda j, bi: (bi, 0, j))],
        out_specs=pl.BlockSpec((1, n2, tn), lambda j, bi: (bi, 0, j)),
        compiler_params=_cparams(("parallel", "parallel"), n2 * l * 2 + 2 * (l * tn * 4 + n2 * tn * 4)),
        name="filter_spectrum",
    )(fwd, y)


def _conv_fwd_kernel(f_ref, y_ref, k_ref, cw_ref, cb_ref, p_ref, *, short_conv, n_split):
    l = y_ref.shape[1]
    y = y_ref[0]
    if short_conv:
        y = _short_conv(y, cw_ref[...], cb_ref[...])
    yb = y.astype(BF16)
    rows = l // n_split
    for s in range(n_split):
        lo, hi = s * rows, (s + 1) * rows
        ure = jnp.dot(f_ref[lo:hi, :], yb, preferred_element_type=F32)
        uim = jnp.dot(f_ref[l + lo:l + hi, :], yb, preferred_element_type=F32)
        kre = k_ref[lo:hi, :]
        kim = k_ref[l + lo:l + hi, :]
        pre = ure * kre - uim * kim
        pim = ure * kim + uim * kre
        if s == 0:
            first = lax.broadcasted_iota(jnp.int32, (rows, 1), 0) == 0
            pre = jnp.where(first, ure * kre, pre)
            pim = jnp.where(first, uim * kim, pim)
        p_ref[0, lo:hi, :] = pre.astype(p_ref.dtype)
        p_ref[0, l + lo:l + hi, :] = pim.astype(p_ref.dtype)


def _conv_fwd(fwd, y, y_col0, kspec, conv_w, conv_b, short_conv, d, tn=256):
    b, l, _ = y.shape
    n2 = fwd.shape[0]
    off = y_col0 // tn
    est = n2 * l * 2 + 2 * (l * tn * 4 + n2 * tn * 4 + n2 * tn * 2) + 6 * l * tn * 4
    return pl.pallas_call(
        functools.partial(_conv_fwd_kernel, short_conv=short_conv, n_split=2),
        out_shape=jax.ShapeDtypeStruct((b, n2, d), BF16),
        grid=(d // tn, b),
        in_specs=[pl.BlockSpec((n2, l), lambda j, bi: (0, 0), pipeline_mode=pl.Buffered(1)),
                  pl.BlockSpec((1, l, tn), lambda j, bi: (bi, 0, j + off)),
                  pl.BlockSpec((n2, tn), lambda j, bi: (0, j)),
                  pl.BlockSpec((3, tn), lambda j, bi: (0, j + off)),
                  pl.BlockSpec((1, tn), lambda j, bi: (0, j + off))],
        out_specs=pl.BlockSpec((1, n2, tn), lambda j, bi: (bi, 0, j)),
        compiler_params=_cparams(("parallel", "parallel"), est),
        name="hyena_conv_fwd",
    )(fwd, y, kspec, conv_w, conv_b)


def _conv_inv_kernel(ft_ref, p_ref, yp_ref, gt_ref, cwy_ref, cby_ref, cwg_ref, cbg_ref, sk_ref, o_ref,
                     *, short_conv_prev):
    conv = jnp.dot(ft_ref[...], p_ref[0], preferred_element_type=F32)
    yp = yp_ref[0]
    if short_conv_prev:
        yp = _short_conv(yp, cwy_ref[...], cby_ref[...])
    gate = _short_conv(gt_ref[0], cwg_ref[...], cbg_ref[...])
    o_ref[0] = (gate * (conv + sk_ref[...] * yp)).astype(o_ref.dtype)


def _conv_inv(finv, p, yprev, yprev_col0, short_conv_prev, z, gate_col0, conv_w, conv_b, skip, out_dtype, tn=256):
    b, n2, d = p.shape
    l = n2 // 2
    offy = yprev_col0 // tn
    offg = gate_col0 // tn
    est = l * n2 * 2 + 2 * (n2 * tn * 2 + 3 * l * tn * 4) + 6 * l * tn * 4
    return pl.pallas_call(
        functools.partial(_conv_inv_kernel, short_conv_prev=short_conv_prev),
        out_shape=jax.ShapeDtypeStruct((b, l, d), out_dtype),
        grid=(d // tn, b),
        in_specs=[pl.BlockSpec((l, n2), lambda j, bi: (0, 0), pipeline_mode=pl.Buffered(1)),
                  pl.BlockSpec((1, n2, tn), lambda j, bi: (bi, 0, j)),
                  pl.BlockSpec((1, l, tn), lambda j, bi: (bi, 0, j + offy)),
                  pl.BlockSpec((1, l, tn), lambda j, bi: (bi, 0, j + offg)),
                  pl.BlockSpec((3, tn), lambda j, bi: (0, j + offy)),
                  pl.BlockSpec((1, tn), lambda j, bi: (0, j + offy)),
                  pl.BlockSpec((3, tn), lambda j, bi: (0, j + offg)),
                  pl.BlockSpec((1, tn), lambda j, bi: (0, j + offg)),
                  pl.BlockSpec((1, tn), lambda j, bi: (0, j))],
        out_specs=pl.BlockSpec((1, l, tn), lambda j, bi: (bi, 0, j)),
        compiler_params=_cparams(("parallel", "parallel"), est),
        name="hyena_conv_inv",
    )(finv, p, yprev, z, conv_w, conv_b, conv_w, conv_b, skip)


def _hyena_mixer(h2, b, l, in_w, in_b, conv_w, conv_b, f_w1, f_w23, f_w4, f_b, f_freq, skip, out_w, out_b):
    d = h2.shape[1]
    z = _matmul(h2, in_w.astype(BF16), bias=in_b, name="hyena_in").reshape(b, l, 3 * d)
    filt = _hyena_filters(l, d, f_w1, f_w23, f_w4, f_b, f_freq)
    fwd, finv = _dft_matrices(l)
    halves = []
    for o in range(2):
        h_fwd = filt[:, o * d:(o + 1) * d]
        h_bwd = filt[:, (2 + o) * d:(3 + o) * d]
        halves.append(h_fwd)
        halves.append(jnp.concatenate([jnp.zeros_like(h_bwd[:1]), h_bwd[:0:-1]], axis=0))
    ks = _spectrum(fwd, jnp.stack(halves))
    row = jnp.arange(2 * l, dtype=jnp.int32)[:, None]
    sign = jnp.where((row % 2 == 1) & (row != l), -1.0, 1.0).astype(F32)
    scale = jnp.where((row == 0) | (row == l), 1.0 / (2 * l), 2.0 / (2 * l)).astype(F32)
    kspec = [(ks[2 * o] + sign * ks[2 * o + 1]) * scale for o in range(2)]
    skip = skip.astype(F32)
    p0 = _conv_fwd(fwd, z, 0, kspec[0], conv_w, conv_b.reshape(1, -1), True, d)
    y1 = _conv_inv(finv, p0, z, 0, True, z, d, conv_w, conv_b.reshape(1, -1), skip[0:1], F32)
    p1 = _conv_fwd(fwd, y1, 0, kspec[1], conv_w, conv_b.reshape(1, -1), False, d)
    y2 = _conv_inv(finv, p1, y1, 0, False, z, 2 * d, conv_w, conv_b.reshape(1, -1), skip[1:2], BF16)
    return _matmul(y2.reshape(b * l, d), out_w.astype(BF16), bias=out_b, name="hyena_out")


def _lora_in(w):
    pad = lambda m: jnp.pad(m, ((0, 0), (0, LANES - m.shape[1])))
    return jnp.concatenate([pad(w[0]), pad(w[1])], axis=1)


def _lora_out(w):
    pad = lambda m: jnp.pad(m, ((0, LANES - m.shape[0]), (0, 0)))
    z = jnp.zeros_like(pad(w[0]))
    return jnp.concatenate([jnp.concatenate([pad(w[0]), z], axis=1),
                            jnp.concatenate([z, pad(w[1])], axis=1)], axis=0)


def _rwkv_mixer(x, ctx, mod3, layer, g0, mu, w_r, w_k, w_v, w_o, dec_w0, dec_w1, dec_w2, a0, a1, a2,
                g1, g2, k_k, k_a, r_k, lnx_w, lnx_b):
    b, l, d = x.shape
    n_ctx = ctx.shape[1]
    s = n_ctx + l
    xr, xw, xk, xv, xa, xg = [t.reshape(b * s, d) for t in _pre_rwkv(x, ctx, mod3, layer, g0, mu)]
    bf = lambda w: w.astype(BF16)
    r = _matmul(xr, bf(w_r), out_dtype=BF16, name="rwkv_r").reshape(b, s, d)
    k = _matmul(xk, bf(w_k), out_dtype=BF16, name="rwkv_k").reshape(b, s, d)
    v = _matmul(xv, bf(w_v), out_dtype=BF16, name="rwkv_v").reshape(b, s, d)
    assert dec_w1.shape[2] <= LANES and a1.shape[2] <= LANES
    lw1 = _matmul(xw, bf(_lora_in(dec_w1)), act="tanh", out_dtype=BF16, name="rwkv_dec1")
    lw = _matmul(lw1, bf(_lora_out(dec_w2)), name="rwkv_dec2").reshape(b, s, 2 * d)
    a1o = _matmul(xa, bf(_lora_in(a1)), out_dtype=BF16, name="rwkv_a1")
    a = _matmul(a1o, bf(_lora_out(a2)), name="rwkv_a2").reshape(b, s, 2 * d)
    g1o = _matmul(xg, bf(g1), act="sigmoid", out_dtype=BF16, name="rwkv_g1")
    g = _matmul(g1o, bf(g2), name="rwkv_g2").reshape(b, s, d)
    zz = _wkv(r, k, v, lw, a, g, dec_w0, a0, k_k, k_a, r_k, lnx_w, lnx_b, n_ctx)
    return _matmul(zz.reshape(b * l, d), bf(w_o), name="rwkv_o")


def kernel(x, c, ctx, c_ctx, ada_w, ada_b, norm_g, mlp_up, mlp_down, rw_mu, rw_w_r, rw_w_k, rw_w_v, rw_w_o, rw_dec_w0, rw_dec_w1, rw_dec_w2, rw_a0, rw_a1, rw_a2, rw_g1, rw_g2, rw_k_k, rw_k_a, rw_r_k, rw_lnx_w, rw_lnx_b, hy_in_w, hy_in_b, hy_conv_w, hy_conv_b, hy_f_w1, hy_f_w23, hy_f_w4, hy_f_b, hy_f_freq, hy_skip, hy_out_w, hy_out_b):
    b, l, d = x.shape
    depth = ada_w.shape[0]
    assert b < MOD_ROWS
    c_rows = jnp.zeros((MOD_ROWS, d), F32).at[:b].set(c).at[b].set(c_ctx)
    mod3 = _ada_mod(c_rows, ada_w, ada_b)
    x2 = x.reshape(b * l, d)
    xc = ctx
    for i in range(depth):
        kind, j = i % 2, i // 2
        ctx_live = any(q % 2 == 0 for q in range(i + 1, depth))
        assert not ctx_live, "context-stream update is not implemented for this depth"
        if kind == 0:
            y2 = _rwkv_mixer(x2.reshape(b, l, d), xc, mod3, i, norm_g[i, 0], rw_mu[j], rw_w_r[j], rw_w_k[j],
                             rw_w_v[j], rw_w_o[j], rw_dec_w0[j], rw_dec_w1[j], rw_dec_w2[j], rw_a0[j],
                             rw_a1[j], rw_a2[j], rw_g1[j], rw_g2[j], rw_k_k[j], rw_k_a[j], rw_r_k[j],
                             rw_lnx_w[j], rw_lnx_b[j])
        else:
            h = _pre_norm(x2, mod3, i, norm_g[i, 0], l)
            y2 = _hyena_mixer(h, b, l, hy_in_w[j], hy_in_b[j], hy_conv_w[j], hy_conv_b[j], hy_f_w1[j],
                              hy_f_w23[j], hy_f_w4[j], hy_f_b[j], hy_f_freq[j], hy_skip[j], hy_out_w[j],
                              hy_out_b[j])
        x2, h2 = _post_mixer(x2, y2, mod3, i, norm_g[i, 1], norm_g[i, 2], l)
        x2 = _mlp(h2, mlp_up[i].astype(BF16), mlp_down[i].astype(BF16), x2, mod3, i, norm_g[i, 3], l)
    return x2.reshape(b, l, d)
```

```python
import functools
import math

import jax
import jax.numpy as jnp
from jax import lax
from jax.experimental import pallas as pl
from jax.experimental.pallas import tpu as pltpu

F32 = jnp.float32
BF16 = jnp.bfloat16

HEAD_SIZE = 64
GRID_W = 64
N_MOD = 6
NORM_EPS = 1e-6
LNX_EPS = 64e-5
HY_FAST_DECAY = 0.3
HY_SLOW_DECAY = 1.5
HY_DECAY_TARGET = 1e-2
HY_EMB_DIM = 33
MOD_ROWS = 16
CHUNK = 64
LANES = 128
TOKEN_TILE = 256
VMEM_LIMIT_CAP = 60000 * 1024


def _cparams(sem, est_bytes):
    limit = int(min(max(2 * est_bytes, 32 * 1024 * 1024), VMEM_LIMIT_CAP))
    return pltpu.CompilerParams(dimension_semantics=sem, vmem_limit_bytes=limit)


def _largest_tile(n, cap, align):
    t = min(cap, n) // align * align
    while t > align and n % t:
        t -= align
    assert t > 0 and n % t == 0, (n, cap, align)
    return t


def _split3(x):
    hi = x.astype(BF16)
    r1 = x - hi.astype(F32)
    mid = r1.astype(BF16)
    lo = (r1 - mid.astype(F32)).astype(BF16)
    return hi, mid, lo


def _rms(x):
    return x * lax.rsqrt(jnp.mean(x * x, axis=-1, keepdims=True) + NORM_EPS)


def _norm_mod(x, g, shift, scale):
    return (_rms(x) * g) * (1.0 + scale) + shift


def _softplus(y):
    return jnp.maximum(y, 0.0) + jnp.log1p(jnp.exp(-jnp.abs(y)))


def _ada_kernel(c_ref, w_ref, b_ref, o_ref):
    c = c_ref[...]
    s = c * jax.nn.sigmoid(c)
    s_hi = s.astype(BF16)
    s_lo = (s - s_hi.astype(F32)).astype(BF16)
    w = w_ref[0]
    w_hi = w.astype(BF16)
    w_lo = (w - w_hi.astype(F32)).astype(BF16)
    p = jnp.dot(jnp.concatenate([s_hi, s_lo], axis=0), w_hi, preferred_element_type=F32)
    q = jnp.dot(s_hi, w_lo, preferred_element_type=F32)
    o_ref[0] = p[:MOD_ROWS] + p[MOD_ROWS:] + q + b_ref[0]


def _ada_mod(c_rows, ada_w, ada_b):
    depth, d, n = ada_w.shape
    tn = 1024
    out = pl.pallas_call(
        _ada_kernel,
        out_shape=jax.ShapeDtypeStruct((depth, MOD_ROWS, n), F32),
        grid=(depth, n // tn),
        in_specs=[pl.BlockSpec((MOD_ROWS, d), lambda l, j: (0, 0)),
                  pl.BlockSpec((1, d, tn), lambda l, j: (l, 0, j)),
                  pl.BlockSpec((1, 1, tn), lambda l, j: (l, 0, j))],
        out_specs=pl.BlockSpec((1, MOD_ROWS, tn), lambda l, j: (l, 0, j)),
        compiler_params=_cparams(("parallel", "parallel"), 2 * d * tn * 4 + 4 * d * tn),
        name="ada_mod",
    )(c_rows, ada_w, ada_b.reshape(depth, 1, n))
    return out.reshape(depth * MOD_ROWS * N_MOD, 1, d)


def _mod_row(layer, row, j):
    return (layer * MOD_ROWS + row) * N_MOD + j


def _pre_rwkv_kernel(hc_ref, sc_ref_ctx, xm_ref, xu_ref, xd_ref, g_ref, sh_ref, sc_ref, mu_ref,
                     *o_refs, n_tiles):
    t = pl.program_id(1)
    d = xm_ref.shape[-1]
    g = g_ref[...]
    mu = mu_ref[...]

    def emit(h, s, lo, hi):
        xx = s - h
        for j in range(6):
            o_refs[j][0, :, lo:hi] = (h + xx * mu[j:j + 1, lo:hi]).astype(BF16)

    @pl.when(t == 0)
    def _():
        emit(hc_ref[0], sc_ref_ctx[0], 0, d)

    @pl.when(t > 0)
    def _():
        sh = sh_ref[0]
        sc = sc_ref[0]
        xm = xm_ref[0].reshape(TOKEN_TILE, d)
        hm = _norm_mod(xm, g, sh, sc)
        hu = _norm_mod(xu_ref[0, 0], g, sh, sc) * jnp.where(t > 1, 1.0, 0.0)
        hd = _norm_mod(xd_ref[0, 0], g, sh, sc) * jnp.where(t < n_tiles, 1.0, 0.0)
        col = lax.broadcasted_iota(jnp.int32, (TOKEN_TILE, 1), 0) & (GRID_W - 1)
        q = d // 4
        left = jnp.where(col != 0, pltpu.roll(hm[:, :q], 1, 0), 0.0)
        right = jnp.where(col != GRID_W - 1, pltpu.roll(hm[:, q:2 * q], TOKEN_TILE - 1, 0), 0.0)
        up = jnp.concatenate([hu[:, 2 * q:3 * q], hm[:TOKEN_TILE - GRID_W, 2 * q:3 * q]], axis=0)
        down = jnp.concatenate([hm[GRID_W:, 3 * q:], hd[:, 3 * q:]], axis=0)
        emit(hm[:, :q], left, 0, q)
        emit(hm[:, q:2 * q], right, q, 2 * q)
        emit(hm[:, 2 * q:3 * q], up, 2 * q, 3 * q)
        emit(hm[:, 3 * q:], down, 3 * q, d)


def _shift_seq(x):
    half = x.shape[-1] // 2
    p = jnp.pad(x, ((0, 0), (1, 1), (0, 0)))
    return jnp.concatenate([p[:, :-2, :half], p[:, 2:, half:]], axis=-1)


def _pre_rwkv(x, ctx, mod3, layer, g0, mu):
    b, l, d = x.shape
    n_ctx = ctx.shape[1]
    assert n_ctx == TOKEN_TILE and l % TOKEN_TILE == 0
    n_tiles = l // TOKEN_TILE
    rows_per_tile = TOKEN_TILE // GRID_W
    n_rows = l // GRID_W
    x4 = x.reshape(b, n_rows, GRID_W, d)
    s = n_ctx + l
    hc = _pre_norm(ctx.reshape(b * n_ctx, d), mod3, layer, g0, n_ctx, tm=n_ctx, fixed_row=b,
                   out_dtype=F32).reshape(b, n_ctx, d)
    sc = _shift_seq(hc)

    def mrow(j):
        return pl.BlockSpec((1, 1, d), lambda bi, t: (_mod_row(layer, bi, j), 0, 0))

    main = lambda bi, t: (bi, jnp.maximum(t - 1, 0), 0, 0)
    up = lambda bi, t: (bi, jnp.maximum((t - 1) * rows_per_tile - 1, 0), 0, 0)
    down = lambda bi, t: (bi, jnp.minimum(jnp.maximum(t, 1) * rows_per_tile, n_rows - 1), 0, 0)
    out_sds = jax.ShapeDtypeStruct((b, s, d), BF16)
    outs = pl.pallas_call(
        functools.partial(_pre_rwkv_kernel, n_tiles=n_tiles),
        out_shape=[out_sds] * 6,
        grid=(b, n_tiles + 1),
        in_specs=[pl.BlockSpec((1, n_ctx, d), lambda bi, t: (bi, 0, 0)),
                  pl.BlockSpec((1, n_ctx, d), lambda bi, t: (bi, 0, 0)),
                  pl.BlockSpec((1, rows_per_tile, GRID_W, d), main),
                  pl.BlockSpec((1, 1, GRID_W, d), up),
                  pl.BlockSpec((1, 1, GRID_W, d), down),
                  pl.BlockSpec((1, d), lambda bi, t: (0, 0)),
                  mrow(0), mrow(1),
                  pl.BlockSpec((6, d), lambda bi, t: (0, 0))],
        out_specs=[pl.BlockSpec((1, TOKEN_TILE, d), lambda bi, t: (bi, t, 0))] * 6,
        compiler_params=_cparams(("parallel", "arbitrary"),
                                 2 * (3 * TOKEN_TILE * d * 4 + 2 * GRID_W * d * 4 + 6 * TOKEN_TILE * d * 2)),
        name="rwkv_pre",
    )(hc, sc, x4, x4, x4, g0.reshape(1, d), mod3, mod3, mu)
    return outs


def _mm_kernel(a_ref, w_ref, *rest, act, has_bias):
    o_ref = rest[-1]
    acc = jnp.dot(a_ref[...], w_ref[...], preferred_element_type=F32)
    if has_bias:
        acc = acc + rest[0][...]
    if act == "tanh":
        acc = jnp.tanh(acc)
    elif act == "sigmoid":
        acc = jax.nn.sigmoid(acc)
    o_ref[...] = acc.astype(o_ref.dtype)


def _matmul(a, w, bias=None, act=None, out_dtype=F32, tm=1024, tn=1024, name="matmul"):
    m, k = a.shape
    n = w.shape[1]
    tm = _largest_tile(m, tm, 8)
    tn = _largest_tile(n, tn, LANES)
    in_specs = [pl.BlockSpec((tm, k), lambda i, j: (i, 0)),
                pl.BlockSpec((k, tn), lambda i, j: (0, j))]
    args = [a, w]
    if bias is not None:
        in_specs.append(pl.BlockSpec((1, tn), lambda i, j: (0, j)))
        args.append(bias.reshape(1, n).astype(F32))
    est = 2 * (tm * k * 2 + k * tn * 2 + tm * tn * jnp.dtype(out_dtype).itemsize) + tm * tn * 4
    return pl.pallas_call(
        functools.partial(_mm_kernel, act=act, has_bias=bias is not None),
        out_shape=jax.ShapeDtypeStruct((m, n), out_dtype),
        grid=(m // tm, n // tn),
        in_specs=in_specs,
        out_specs=pl.BlockSpec((tm, tn), lambda i, j: (i, j)),
        compiler_params=_cparams(("parallel", "parallel"), est),
        name=name,
    )(*args)


def _seg_sum(x, ones_bd):
    hi, mid, lo = _split3(x)
    r = x.shape[0]
    p = jnp.dot(jnp.concatenate([hi, mid, lo], axis=0), ones_bd, preferred_element_type=F32)
    return p[:r] + p[r:2 * r] + p[2 * r:]


def _bd(x, head0):
    return jnp.concatenate([jnp.where(head0, x, 0.0), jnp.where(head0, 0.0, x)], axis=0)


def _wkv_constants(rev):
    c = CHUNK
    shift = int(math.log2(c))
    head0 = lax.broadcasted_iota(jnp.int32, (1, LANES), 1) < HEAD_SIZE
    row2 = lax.broadcasted_iota(jnp.int32, (2 * c, 2 * c), 0)
    col2 = lax.broadcasted_iota(jnp.int32, (2 * c, 2 * c), 1)
    same = (row2 >> shift) == (col2 >> shift)
    tt = row2 & (c - 1)
    ss = col2 & (c - 1)
    if rev:
        tt, ss = ss, tt
    ones_bd = jnp.where(same, 1.0, 0.0).astype(BF16)
    eye = jnp.where(row2 == col2, 1.0, 0.0).astype(F32)
    rowc = lax.broadcasted_iota(jnp.int32, (c, c), 0)
    colc = lax.broadcasted_iota(jnp.int32, (c, c), 1)
    tri = jnp.where(colc >= rowc if rev else colc <= rowc, 1.0, 0.0).astype(BF16)
    merge = tuple(((tt >> lv) == (ss >> lv) + 1) & ((tt >> (lv + 1)) == (ss >> (lv + 1))) for lv in range(shift))
    return head0, ones_bd, tri, ss < tt, ss <= tt, eye, merge


def _each(fn, *lists):
    return [fn(*args) for args in zip(*lists)]


def _mm(a, b):
    return jnp.dot(a.astype(BF16), b.astype(BF16), preferred_element_type=F32)


def _wkv_prepare(probs, k_a):
    revs, rs, ks, kks, bd_vs, lwxs, axs, w0s, a0s, csts = [list(t) for t in zip(*probs)]
    head0 = csts[0][0]
    eye = csts[0][5]
    c = rs[0].shape[0]
    c2 = 2 * c
    logw = _each(lambda w0, lwx: -jnp.exp(-_softplus(-(w0 + lwx)) - 0.5), w0s, lwxs)
    a = _each(lambda a0, ax: jax.nn.sigmoid(a0 + ax), a0s, axs)
    kd = _each(lambda k, a_: k * (1.0 + (a_ - 1.0) * k_a), ks, a)
    b = _each(lambda kk, a_: kk * a_, kks, a)

    def cumsum(lw, cst):
        hi, mid, lo = _split3(lw)
        c3 = jnp.dot(cst[2], jnp.concatenate([hi, mid, lo], axis=1), preferred_element_type=F32)
        return c3[:, :LANES] + c3[:, LANES:2 * LANES] + c3[:, 2 * LANES:]

    cum = _each(cumsum, logw, csts)
    total = _each(lambda cm, rev: cm[0:1] if rev else cm[c - 1:c], cum, revs)
    kap_t = _each(lambda kk, cm, lw: kk * jnp.exp(cm - lw), kks, cum, logw)
    r_t = _each(lambda r, cm: r * jnp.exp(cm), rs, cum)
    igam = _each(lambda cm: jnp.exp(-cm), cum)
    tail = _each(lambda t, cm: jnp.exp(t - cm), total, cum)

    bd_kap = _each(lambda x: _bd(x, head0), kap_t)
    bd_r = _each(lambda x: _bd(x, head0), r_t)
    lhs = _each(lambda p, q: jnp.concatenate([p, q], axis=0).astype(BF16), bd_kap, bd_r)
    rhs = _each(lambda kd_, b_, ig: jnp.concatenate([_bd(kd_ * ig, head0), _bd(b_ * ig, head0)],
                                                    axis=0).astype(BF16), kd, b, igam)
    gmat = _each(lambda l_, r_: lax.dot_general(l_, r_, (((1,), (1,)), ((), ())),
                                                preferred_element_type=F32), lhs, rhs)
    a_kb = _each(lambda g, cst: jnp.where(cst[3], g[:c2, c2:], 0.0), gmat, csts)

    tinv = _each(lambda akb, cst: eye - jnp.where(cst[6][0], akb, 0.0), a_kb, csts)
    for lv in range(1, len(csts[0][6])):
        y = _each(lambda akb, x, cst: _mm(jnp.where(cst[6][lv], akb, 0.0), x), a_kb, tinv, csts)
        tinv = _each(lambda x, y_: x - _mm(x, y_), tinv, y)

    a_kr = _each(lambda g, cst: jnp.concatenate([jnp.where(cst[3], g[:c2, :c2], 0.0),
                                                 jnp.where(cst[4], g[c2:, :c2], 0.0)], axis=0), gmat, csts)
    av = _each(_mm, a_kr, bd_vs)
    wu = _each(lambda x, kap, av_: _mm(x, jnp.concatenate([kap, av_[:c2]], axis=1)), tinv, bd_kap, av)
    rb = _each(lambda g, cst, wu_: _mm(jnp.where(cst[4], g[c2:, c2:], 0.0), wu_), gmat, csts, wu)
    bct = _each(lambda b_, tl: _bd(b_ * tl, head0).T, b, tail)
    kct = _each(lambda kd_, tl: _bd(kd_ * tl, head0).T, kd, tail)
    mn = _each(_mm, bct, wu)
    kv = _each(_mm, kct, bd_vs)
    out = []
    for i in range(len(probs)):
        rk = bd_r[i] - rb[i][:, :LANES]
        ov = av[i][c2:] - rb[i][:, LANES:]
        m_mat = jnp.where(eye > 0.0, jnp.exp(total[i]), 0.0) - mn[i][:, :LANES]
        n_mat = kv[i] - mn[i][:, LANES:]
        out.append((jnp.concatenate([rk, m_mat], axis=0).astype(BF16), ov[:c] + ov[c:], n_mat))
    return out


def _wkv_apply(rm_lhs, ov, n_mat, h_state):
    c = ov.shape[0]
    rm = jnp.dot(rm_lhs, h_state.astype(BF16), preferred_element_type=F32)
    return rm[:c] + rm[c:2 * c] + ov, rm[2 * c:] + n_mat


def _wkv_kernel(r_ref, k_ref, v_ref, lwf_ref, lwb_ref, af_ref, ab_ref, g_ref,
                w0_ref, a0_ref, kk_ref, ka_ref, rk_ref, lnw_ref, lnb_ref,
                z_ref, of_scr, ob_scr, rm_scr, ov_scr, n_scr, *, n_ctx_chunks, n_chunks, unroll):
    c = CHUNK
    cst_f = _wkv_constants(False)
    cst_b = _wkv_constants(True)
    head0, ones_bd = cst_f[0], cst_f[1]

    w0f, w0b = w0_ref[0:1, :], w0_ref[1:2, :]
    a0f, a0b = a0_ref[0:1, :], a0_ref[1:2, :]
    k_k = kk_ref[...]
    k_a = ka_ref[...]

    def load(ref, rows):
        return ref[0, rows, :].astype(F32)

    def prep_body(grp, _):
        probs, slots = [], []
        for u in range(unroll):
            ci = grp * unroll + u
            rows = pl.ds(pl.multiple_of(ci * c, c), c)
            r = load(r_ref, rows)
            k = load(k_ref, rows)
            kk0 = k * k_k
            kk = kk0 * lax.rsqrt(jnp.maximum(_seg_sum(kk0 * kk0, ones_bd), 1e-24))
            bd_v = _bd(load(v_ref, rows), head0).astype(BF16)
            probs.append((False, r, k, kk, bd_v, load(lwf_ref, rows), load(af_ref, rows), w0f, a0f, cst_f))
            probs.append((True, r, k, kk, bd_v, load(lwb_ref, rows), load(ab_ref, rows), w0b, a0b, cst_b))
            slots += [ci, ci + n_chunks]
        for slot, (rm_lhs, ov, n_mat) in zip(slots, _wkv_prepare(probs, k_a)):
            rm_scr[slot] = rm_lhs
            ov_scr[slot] = ov
            n_scr[slot] = n_mat
        return 0

    lax.fori_loop(0, n_chunks // unroll, prep_body, 0)

    def scan_body(i, carry):
        h_f, h_b = carry
        cf = i
        cb = jnp.where(i < n_ctx_chunks, n_ctx_chunks - 1 - i, n_chunks + n_ctx_chunks - 1 - i)
        o_f, h_f = _wkv_apply(rm_scr[cf], ov_scr[cf], n_scr[cf], h_f)
        o_b, h_b = _wkv_apply(rm_scr[cb + n_chunks], ov_scr[cb + n_chunks], n_scr[cb + n_chunks], h_b)
        of_scr[pl.ds(pl.multiple_of(cf * c, c), c), :] = o_f
        ob_scr[pl.ds(pl.multiple_of(cb * c, c), c), :] = o_b
        return h_f, h_b

    zero = jnp.zeros((2 * c, LANES), F32)
    lax.fori_loop(0, n_chunks, scan_body, (zero, zero))

    r_k = rk_ref[...]
    lnw = lnw_ref[...]
    lnb = lnb_ref[...]
    blk = TOKEN_TILE
    n_ctx = n_ctx_chunks * c
    inv_n = 1.0 / HEAD_SIZE

    def read_body(j, _):
        rows = pl.ds(pl.multiple_of(n_ctx + j * blk, blk), blk)
        o = of_scr[rows, :] + ob_scr[rows, :]
        mean = _seg_sum(o, ones_bd) * inv_n
        dev = o - mean
        var = _seg_sum(dev * dev, ones_bd) * inv_n
        on = dev * lax.rsqrt(var + LNX_EPS) * lnw + lnb
        r = load(r_ref, rows)
        k = load(k_ref, rows)
        v = load(v_ref, rows)
        a_sum = jax.nn.sigmoid(a0f + load(af_ref, rows)) + jax.nn.sigmoid(a0b + load(ab_ref, rows))
        kd_sum = k * (2.0 + (a_sum - 2.0) * k_a)
        bonus = _seg_sum(r * kd_sum * r_k, ones_bd) * v
        z = (on + bonus) * g_ref[0, rows, :]
        z_ref[0, pl.ds(pl.multiple_of(j * blk, blk), blk), :] = z.astype(z_ref.dtype)
        return 0

    lax.fori_loop(0, (n_chunks - n_ctx_chunks) * c // blk, read_body, 0)


def _wkv(r, k, v, lw, a, g, dec_w0, a0, k_k, k_a, r_k, lnx_w, lnx_b, n_ctx):
    b, s, d = r.shape
    l = s - n_ctx
    npair = d // LANES
    seq = lambda bi, p: (bi, 0, p)
    seq_b = lambda bi, p: (bi, 0, p + npair)
    vec = lambda bi, p: (0, p)
    sblk = pl.BlockSpec((1, s, LANES), seq)
    n_chunks = s // CHUNK
    unroll = next(u for u in (4, 2, 1) if n_chunks % u == 0)
    scratch = [pltpu.VMEM((s, LANES), F32), pltpu.VMEM((s, LANES), F32),
               pltpu.VMEM((2 * n_chunks, 4 * CHUNK, LANES), BF16),
               pltpu.VMEM((2 * n_chunks, CHUNK, LANES), F32),
               pltpu.VMEM((2 * n_chunks, 2 * CHUNK, LANES), F32)]
    est = (2 * (3 * s * LANES * 2 + 5 * s * LANES * 4 + l * LANES * 2) + 2 * s * LANES * 4
           + 2 * n_chunks * CHUNK * LANES * (4 * 2 + 4 + 2 * 4))
    return pl.pallas_call(
        functools.partial(_wkv_kernel, n_ctx_chunks=n_ctx // CHUNK, n_chunks=n_chunks, unroll=unroll),
        out_shape=jax.ShapeDtypeStruct((b, l, d), BF16),
        grid=(b, npair),
        in_specs=[sblk, sblk, sblk,
                  pl.BlockSpec((1, s, LANES), seq), pl.BlockSpec((1, s, LANES), seq_b),
                  pl.BlockSpec((1, s, LANES), seq), pl.BlockSpec((1, s, LANES), seq_b),
                  pl.BlockSpec((1, s, LANES), seq),
                  pl.BlockSpec((2, LANES), vec), pl.BlockSpec((2, LANES), vec),
                  pl.BlockSpec((1, LANES), vec), pl.BlockSpec((1, LANES), vec), pl.BlockSpec((1, LANES), vec),
                  pl.BlockSpec((1, LANES), vec), pl.BlockSpec((1, LANES), vec)],
        out_specs=pl.BlockSpec((1, l, LANES), seq),
        scratch_shapes=scratch,
        compiler_params=_cparams(("parallel", "parallel"), est),
        name="wkv_scan",
    )(r, k, v, lw, lw, a, a, g, dec_w0, a0, k_k.reshape(1, d), k_a.reshape(1, d), r_k.reshape(1, d),
      lnx_w.reshape(1, d), lnx_b.reshape(1, d))


def _post_kernel(x_ref, y_ref, g1_ref, g2_ref, gate_ref, sh_ref, sc_ref, xo_ref, h_ref):
    x = x_ref[...] + gate_ref[0] * (_rms(y_ref[...]) * g1_ref[...])
    xo_ref[...] = x
    h_ref[...] = _norm_mod(x, g2_ref[...], sh_ref[0], sc_ref[0]).astype(h_ref.dtype)


def _post_mixer(x2, y2, mod3, layer, g1, g2, l, tm=512):
    m, d = x2.shape
    per_b = l // tm
    row = lambda j: pl.BlockSpec((1, 1, d), lambda i: (_mod_row(layer, i // per_b, j), 0, 0))
    tile = pl.BlockSpec((tm, d), lambda i: (i, 0))
    vec = pl.BlockSpec((1, d), lambda i: (0, 0))
    return pl.pallas_call(
        _post_kernel,
        out_shape=[jax.ShapeDtypeStruct((m, d), F32), jax.ShapeDtypeStruct((m, d), BF16)],
        grid=(m // tm,),
        in_specs=[tile, tile, vec, vec, row(2), row(3), row(4)],
        out_specs=[tile, tile],
        compiler_params=_cparams(("parallel",), 2 * tm * d * (4 + 4 + 4 + 2)),
        name="post_mixer",
    )(x2, y2, g1.reshape(1, d), g2.reshape(1, d), mod3, mod3, mod3)


def _mlp_kernel(h_ref, wu_ref, wd_ref, x_ref, g_ref, gate_ref, o_ref, acc_ref):
    kf = pl.program_id(1)

    @pl.when(kf == 0)
    def _():
        acc_ref[...] = jnp.zeros_like(acc_ref)

    u = jnp.dot(h_ref[...], wu_ref[...], preferred_element_type=F32)
    u = jnp.square(jnp.maximum(u, 0.0)).astype(BF16)
    acc_ref[...] += jnp.dot(u, wd_ref[...], preferred_element_type=F32)

    @pl.when(kf == pl.num_programs(1) - 1)
    def _():
        o_ref[...] = x_ref[...] + gate_ref[0] * (_rms(acc_ref[...]) * g_ref[...])


def _mlp(h2, w_up, w_down, x2, mod3, layer, g3, l, tm=512, tf=1024):
    m, d = h2.shape
    dff = w_up.shape[1]
    per_b = l // tm
    est = 2 * (tm * d * 2 + 2 * d * tf * 2 + 2 * tm * d * 4) + tm * d * 4 + tm * tf * 6
    return pl.pallas_call(
        _mlp_kernel,
        out_shape=jax.ShapeDtypeStruct((m, d), F32),
        grid=(m // tm, dff // tf),
        in_specs=[pl.BlockSpec((tm, d), lambda i, f: (i, 0)),
                  pl.BlockSpec((d, tf), lambda i, f: (0, f)),
                  pl.BlockSpec((tf, d), lambda i, f: (f, 0)),
                  pl.BlockSpec((tm, d), lambda i, f: (i, 0)),
                  pl.BlockSpec((1, d), lambda i, f: (0, 0)),
                  pl.BlockSpec((1, 1, d), lambda i, f: (_mod_row(layer, i // per_b, 5), 0, 0))],
        out_specs=pl.BlockSpec((tm, d), lambda i, f: (i, 0)),
        scratch_shapes=[pltpu.VMEM((tm, d), F32)],
        compiler_params=_cparams(("parallel", "arbitrary"), est),
        name="mlp",
    )(h2, w_up, w_down, x2, g3.reshape(1, d), mod3)


def _pre_norm_kernel(x_ref, g_ref, sh_ref, sc_ref, h_ref):
    h_ref[...] = _norm_mod(x_ref[...], g_ref[...], sh_ref[0], sc_ref[0]).astype(h_ref.dtype)


def _pre_norm(x2, mod3, layer, g0, l, tm=512, fixed_row=None, out_dtype=BF16):
    m, d = x2.shape
    per_b = l // tm
    if fixed_row is None:
        row = lambda j: pl.BlockSpec((1, 1, d), lambda i: (_mod_row(layer, i // per_b, j), 0, 0))
    else:
        row = lambda j: pl.BlockSpec((1, 1, d), lambda i: (_mod_row(layer, fixed_row, j), 0, 0))
    tile = pl.BlockSpec((tm, d), lambda i: (i, 0))
    return pl.pallas_call(
        _pre_norm_kernel,
        out_shape=jax.ShapeDtypeStruct((m, d), out_dtype),
        grid=(m // tm,),
        in_specs=[tile, pl.BlockSpec((1, d), lambda i: (0, 0)), row(0), row(1)],
        out_specs=tile,
        compiler_params=_cparams(("parallel",), 2 * tm * d * 6),
        name="pre_norm",
    )(x2, g0.reshape(1, d), mod3, mod3)


def _filter_kernel(z_ref, w1_ref, w2_ref, w3_ref, b_ref, fr_ref, w4_ref, t_ref, dl_ref, o_ref):
    hp = lax.Precision.HIGHEST
    b = b_ref[...]
    fr = fr_ref[...]
    z = jnp.sin(fr[0:1] * (jnp.dot(z_ref[...], w1_ref[...], precision=hp, preferred_element_type=F32) + b[0:1]))
    z = jnp.sin(fr[1:2] * (jnp.dot(z, w2_ref[...], precision=hp, preferred_element_type=F32) + b[1:2]))
    z = jnp.sin(fr[2:3] * (jnp.dot(z, w3_ref[...], precision=hp, preferred_element_type=F32) + b[2:3]))
    filt = jnp.dot(z, w4_ref[...], precision=hp, preferred_element_type=F32)
    o_ref[...] = filt * jnp.exp(-t_ref[...] * dl_ref[...])


def _hyena_filters(l, d, f_w1, f_w23, f_w4, f_b, f_freq):
    t = jnp.linspace(0.0, 1.0, l, dtype=F32)[:, None]
    bands = (HY_EMB_DIM - 1) // 2
    freqs = jnp.linspace(1e-4, bands - 1, bands, dtype=F32)[None, :]
    ang = (2.0 * math.pi / l) * jnp.arange(l, dtype=F32)[:, None] * freqs
    z = jnp.concatenate([t, jnp.cos(ang), -jnp.sin(ang)], axis=-1)
    e, f = f_w1.shape
    pad = lambda a_, r, c: jnp.pad(a_.astype(F32), ((0, r - a_.shape[0]), (0, c - a_.shape[1])))
    zp = pad(z, l, LANES)
    w1 = pad(f_w1, LANES, LANES)
    w2 = pad(f_w23[0], LANES, LANES)
    w3 = pad(f_w23[1], LANES, LANES)
    bb = pad(f_b, 8, LANES)
    fr = pad(f_freq, 8, LANES)
    n = f_w4.shape[1]
    w4 = pad(f_w4, LANES, n)
    max_decay = math.log(HY_DECAY_TARGET) / HY_FAST_DECAY
    min_decay = math.log(HY_DECAY_TARGET) / HY_SLOW_DECAY
    deltas = jnp.abs(jnp.linspace(min_decay, max_decay, d, dtype=F32))[None, :]
    tn = _largest_tile(d, 1024, LANES)
    per_d = d // tn
    sq = pl.BlockSpec((LANES, LANES), lambda j: (0, 0))
    small = pl.BlockSpec((8, LANES), lambda j: (0, 0))
    return pl.pallas_call(
        _filter_kernel,
        out_shape=jax.ShapeDtypeStruct((l, n), F32),
        grid=(n // tn,),
        in_specs=[pl.BlockSpec((l, LANES), lambda j: (0, 0)), sq, sq, sq, small, small,
                  pl.BlockSpec((LANES, tn), lambda j: (0, j)),
                  pl.BlockSpec((l, 1), lambda j: (0, 0)),
                  pl.BlockSpec((1, tn), lambda j: (0, j % per_d))],
        out_specs=pl.BlockSpec((l, tn), lambda j: (0, j)),
        compiler_params=_cparams(("parallel",), 4 * l * tn * 4),
        name="hyena_filters",
    )(zp, w1, w2, w3, bb, fr, w4, t, deltas)


def _dft_matrices(l):
    n = 2 * l
    k = jnp.arange(l, dtype=jnp.int32)[:, None]
    t = jnp.arange(l, dtype=jnp.int32)[None, :]
    ang = ((k * t) % n).astype(F32) * (2.0 * math.pi / n)
    cos = jnp.cos(ang)
    msin = jnp.where(k == 0, jnp.where(t % 2 == 0, 1.0, -1.0), -jnp.sin(ang))
    fwd = jnp.concatenate([cos, msin], axis=0)
    return fwd.astype(BF16), fwd.T.astype(BF16)


def _short_conv(z, cw, cb):
    n = z.shape[0]
    row = lax.broadcasted_iota(jnp.int32, (n, 1), 0)
    prev = jnp.where(row != 0, pltpu.roll(z, 1, 0), 0.0)
    nxt = jnp.where(row != n - 1, pltpu.roll(z, n - 1, 0), 0.0)
    return prev * cw[0:1] + z * cw[1:2] + nxt * cw[2:3] + cb


def _spec_kernel(f_ref, y_ref, o_ref):
    o_ref[0] = jnp.dot(f_ref[...], y_ref[0].astype(BF16), preferred_element_type=F32)


def _spectrum(fwd, y, tn=256):
    nb, l, c = y.shape
    n2 = fwd.shape[0]
    return pl.pallas_call(
        _spec_kernel,
        out_shape=jax.ShapeDtypeStruct((nb, n2, c), F32),
        grid=(c // tn, nb),
        in_specs=[pl.BlockSpec((n2, l), lambda j, bi: (0, 0), pipeline_mode=pl.Buffered(1)),
                  pl.BlockSpec((1, l, tn), lambda j, bi: (bi, 0, j))],
        out_specs=pl.BlockSpec((1, n2, tn), lambda j, bi: (bi, 0, j)),
        compiler_params=_cparams(("parallel", "parallel"), n2 * l * 2 + 2 * (l * tn * 4 + n2 * tn * 4)),
        name="filter_spectrum",
    )(fwd, y)


def _conv_fwd_kernel(f_ref, y_ref, k_ref, cw_ref, cb_ref, p_ref, *, short_conv, n_split):
    l = y_ref.shape[1]
    y = y_ref[0]
    if short_conv:
        y = _short_conv(y, cw_ref[...], cb_ref[...])
    yb = y.astype(BF16)
    rows = l // n_split
    for s in range(n_split):
        lo, hi = s * rows, (s + 1) * rows
        ure = jnp.dot(f_ref[lo:hi, :], yb, preferred_element_type=F32)
        uim = jnp.dot(f_ref[l + lo:l + hi, :], yb, preferred_element_type=F32)
        kre = k_ref[lo:hi, :]
        kim = k_ref[l + lo:l + hi, :]
        pre = ure * kre - uim * kim
        pim = ure * kim + uim * kre
        if s == 0:
            first = lax.broadcasted_iota(jnp.int32, (rows, 1), 0) == 0
            pre = jnp.where(first, ure * kre, pre)
            pim = jnp.where(first, uim * kim, pim)
        p_ref[0, lo:hi, :] = pre.astype(p_ref.dtype)
        p_ref[0, l + lo:l + hi, :] = pim.astype(p_ref.dtype)


def _conv_fwd(fwd, y, y_col0, kspec, conv_w, conv_b, short_conv, d, tn=256):
    b, l, _ = y.shape
    n2 = fwd.shape[0]
    off = y_col0 // tn
    est = n2 * l * 2 + 2 * (l * tn * 4 + n2 * tn * 4 + n2 * tn * 2) + 6 * l * tn * 4
    return pl.pallas_call(
        functools.partial(_conv_fwd_kernel, short_conv=short_conv, n_split=2),
        out_shape=jax.ShapeDtypeStruct((b, n2, d), BF16),
        grid=(d // tn, b),
        in_specs=[pl.BlockSpec((n2, l), lambda j, bi: (0, 0), pipeline_mode=pl.Buffered(1)),
                  pl.BlockSpec((1, l, tn), lambda j, bi: (bi, 0, j + off)),
                  pl.BlockSpec((n2, tn), lambda j, bi: (0, j)),
                  pl.BlockSpec((3, tn), lambda j, bi: (0, j + off)),
                  pl.BlockSpec((1, tn), lambda j, bi: (0, j + off))],
        out_specs=pl.BlockSpec((1, n2, tn), lambda j, bi: (bi, 0, j)),
        compiler_params=_cparams(("parallel", "parallel"), est),
        name="hyena_conv_fwd",
    )(fwd, y, kspec, conv_w, conv_b)


def _conv_inv_kernel(ft_ref, p_ref, yp_ref, gt_ref, cwy_ref, cby_ref, cwg_ref, cbg_ref, sk_ref, o_ref,
                     *, short_conv_prev):
    conv = jnp.dot(ft_ref[...], p_ref[0], preferred_element_type=F32)
    yp = yp_ref[0]
    if short_conv_prev:
        yp = _short_conv(yp, cwy_ref[...], cby_ref[...])
    gate = _short_conv(gt_ref[0], cwg_ref[...], cbg_ref[...])
    o_ref[0] = (gate * (conv + sk_ref[...] * yp)).astype(o_ref.dtype)


def _conv_inv(finv, p, yprev, yprev_col0, short_conv_prev, z, gate_col0, conv_w, conv_b, skip, out_dtype, tn=256):
    b, n2, d = p.shape
    l = n2 // 2
    offy = yprev_col0 // tn
    offg = gate_col0 // tn
    est = l * n2 * 2 + 2 * (n2 * tn * 2 + 3 * l * tn * 4) + 6 * l * tn * 4
    return pl.pallas_call(
        functools.partial(_conv_inv_kernel, short_conv_prev=short_conv_prev),
        out_shape=jax.ShapeDtypeStruct((b, l, d), out_dtype),
        grid=(d // tn, b),
        in_specs=[pl.BlockSpec((l, n2), lambda j, bi: (0, 0), pipeline_mode=pl.Buffered(1)),
                  pl.BlockSpec((1, n2, tn), lambda j, bi: (bi, 0, j)),
                  pl.BlockSpec((1, l, tn), lambda j, bi: (bi, 0, j + offy)),
                  pl.BlockSpec((1, l, tn), lambda j, bi: (bi, 0, j + offg)),
                  pl.BlockSpec((3, tn), lambda j, bi: (0, j + offy)),
                  pl.BlockSpec((1, tn), lambda j, bi: (0, j + offy)),
                  pl.BlockSpec((3, tn), lambda j, bi: (0, j + offg)),
                  pl.BlockSpec((1, tn), lambda j, bi: (0, j + offg)),
                  pl.BlockSpec((1, tn), lambda j, bi: (0, j))],
        out_specs=pl.BlockSpec((1, l, tn), lambda j, bi: (bi, 0, j)),
        compiler_params=_cparams(("parallel", "parallel"), est),
        name="hyena_conv_inv",
    )(finv, p, yprev, z, conv_w, conv_b, conv_w, conv_b, skip)


def _hyena_mixer(h2, b, l, in_w, in_b, conv_w, conv_b, f_w1, f_w23, f_w4, f_b, f_freq, skip, out_w, out_b):
    d = h2.shape[1]
    z = _matmul(h2, in_w.astype(BF16), bias=in_b, name="hyena_in").reshape(b, l, 3 * d)
    filt = _hyena_filters(l, d, f_w1, f_w23, f_w4, f_b, f_freq)
    fwd, finv = _dft_matrices(l)
    halves = []
    for o in range(2):
        h_fwd = filt[:, o * d:(o + 1) * d]
        h_bwd = filt[:, (2 + o) * d:(3 + o) * d]
        halves.append(h_fwd)
        halves.append(jnp.concatenate([jnp.zeros_like(h_bwd[:1]), h_bwd[:0:-1]], axis=0))
    ks = _spectrum(fwd, jnp.stack(halves))
    row = jnp.arange(2 * l, dtype=jnp.int32)[:, None]
    sign = jnp.where((row % 2 == 1) & (row != l), -1.0, 1.0).astype(F32)
    scale = jnp.where((row == 0) | (row == l), 1.0 / (2 * l), 2.0 / (2 * l)).astype(F32)
    kspec = [(ks[2 * o] + sign * ks[2 * o + 1]) * scale for o in range(2)]
    skip = skip.astype(F32)
    p0 = _conv_fwd(fwd, z, 0, kspec[0], conv_w, conv_b.reshape(1, -1), True, d)
    y1 = _conv_inv(finv, p0, z, 0, True, z, d, conv_w, conv_b.reshape(1, -1), skip[0:1], F32)
    p1 = _conv_fwd(fwd, y1, 0, kspec[1], conv_w, conv_b.reshape(1, -1), False, d)
    y2 = _conv_inv(finv, p1, y1, 0, False, z, 2 * d, conv_w, conv_b.reshape(1, -1), skip[1:2], BF16)
    return _matmul(y2.reshape(b * l, d), out_w.astype(BF16), bias=out_b, name="hyena_out")


def _lora_in(w):
    pad = lambda m: jnp.pad(m, ((0, 0), (0, LANES - m.shape[1])))
    return jnp.concatenate([pad(w[0]), pad(w[1])], axis=1)


def _lora_out(w):
    pad = lambda m: jnp.pad(m, ((0, LANES - m.shape[0]), (0, 0)))
    z = jnp.zeros_like(pad(w[0]))
    return jnp.concatenate([jnp.concatenate([pad(w[0]), z], axis=1),
                            jnp.concatenate([z, pad(w[1])], axis=1)], axis=0)


def _rwkv_mixer(x, ctx, mod3, layer, g0, mu, w_r, w_k, w_v, w_o, dec_w0, dec_w1, dec_w2, a0, a1, a2,
                g1, g2, k_k, k_a, r_k, lnx_w, lnx_b):
    b, l, d = x.shape
    n_ctx = ctx.shape[1]
    s = n_ctx + l
    xr, xw, xk, xv, xa, xg = [t.reshape(b * s, d) for t in _pre_rwkv(x, ctx, mod3, layer, g0, mu)]
    bf = lambda w: w.astype(BF16)
    r = _matmul(xr, bf(w_r), out_dtype=BF16, name="rwkv_r").reshape(b, s, d)
    k = _matmul(xk, bf(w_k), out_dtype=BF16, name="rwkv_k").reshape(b, s, d)
    v = _matmul(xv, bf(w_v), out_dtype=BF16, name="rwkv_v").reshape(b, s, d)
    assert dec_w1.shape[2] <= LANES and a1.shape[2] <= LANES
    lw1 = _matmul(xw, bf(_lora_in(dec_w1)), act="tanh", out_dtype=BF16, name="rwkv_dec1")
    lw = _matmul(lw1, bf(_lora_out(dec_w2)), name="rwkv_dec2").reshape(b, s, 2 * d)
    a1o = _matmul(xa, bf(_lora_in(a1)), out_dtype=BF16, name="rwkv_a1")
    a = _matmul(a1o, bf(_lora_out(a2)), name="rwkv_a2").reshape(b, s, 2 * d)
    g1o = _matmul(xg, bf(g1), act="sigmoid", out_dtype=BF16, name="rwkv_g1")
    g = _matmul(g1o, bf(g2), name="rwkv_g2").reshape(b, s, d)
    zz = _wkv(r, k, v, lw, a, g, dec_w0, a0, k_k, k_a, r_k, lnx_w, lnx_b, n_ctx)
    return _matmul(zz.reshape(b * l, d), bf(w_o), name="rwkv_o")


def kernel(x, c, ctx, c_ctx, ada_w, ada_b, norm_g, mlp_up, mlp_down, rw_mu, rw_w_r, rw_w_k, rw_w_v, rw_w_o, rw_dec_w0, rw_dec_w1, rw_dec_w2, rw_a0, rw_a1, rw_a2, rw_g1, rw_g2, rw_k_k, rw_k_a, rw_r_k, rw_lnx_w, rw_lnx_b, hy_in_w, hy_in_b, hy_conv_w, hy_conv_b, hy_f_w1, hy_f_w23, hy_f_w4, hy_f_b, hy_f_freq, hy_skip, hy_out_w, hy_out_b):
    b, l, d = x.shape
    depth = ada_w.shape[0]
    assert b < MOD_ROWS
    c_rows = jnp.zeros((MOD_ROWS, d), F32).at[:b].set(c).at[b].set(c_ctx)
    mod3 = _ada_mod(c_rows, ada_w, ada_b)
    x2 = x.reshape(b * l, d)
    xc = ctx
    for i in range(depth):
        kind, j = i % 2, i // 2
        ctx_live = any(q % 2 == 0 for q in range(i + 1, depth))
        assert not ctx_live, "context-stream update is not implemented for this depth"
        if kind == 0:
            y2 = _rwkv_mixer(x2.reshape(b, l, d), xc, mod3, i, norm_g[i, 0], rw_mu[j], rw_w_r[j], rw_w_k[j],
                             rw_w_v[j], rw_w_o[j], rw_dec_w0[j], rw_dec_w1[j], rw_dec_w2[j], rw_a0[j],
                             rw_a1[j], rw_a2[j], rw_g1[j], rw_g2[j], rw_k_k[j], rw_k_a[j], rw_r_k[j],
                             rw_lnx_w[j], rw_lnx_b[j])
        else:
            h = _pre_norm(x2, mod3, i, norm_g[i, 0], l)
            y2 = _hyena_mixer(h, b, l, hy_in_w[j], hy_in_b[j], hy_conv_w[j], hy_conv_b[j], hy_f_w1[j],
                              hy_f_w23[j], hy_f_w4[j], hy_f_b[j], hy_f_freq[j], hy_skip[j], hy_out_w[j],
                              hy_out_b[j])
        x2, h2 = _post_mixer(x2, y2, mod3, i, norm_g[i, 1], norm_g[i, 2], l)
        x2 = _mlp(h2, mlp_up[i].astype(BF16), mlp_down[i].astype(BF16), x2, mod3, i, norm_g[i, 3], l)
    return x2.reshape(b, l, d)
```

```python
import functools
import math

import jax
import jax.numpy as jnp
from jax import lax
from jax.experimental import pallas as pl
from jax.experimental.pallas import tpu as pltpu

F32 = jnp.float32
BF16 = jnp.bfloat16

HEAD_SIZE = 64
GRID_W = 64
N_MOD = 6
NORM_EPS = 1e-6
LNX_EPS = 64e-5
HY_FAST_DECAY = 0.3
HY_SLOW_DECAY = 1.5
HY_DECAY_TARGET = 1e-2
HY_EMB_DIM = 33
MOD_ROWS = 16
CHUNK = 64
LANES = 128
TOKEN_TILE = 256
VMEM_LIMIT_CAP = 60000 * 1024


def _cparams(sem, est_bytes):
    limit = int(min(max(2 * est_bytes, 32 * 1024 * 1024), VMEM_LIMIT_CAP))
    return pltpu.CompilerParams(dimension_semantics=sem, vmem_limit_bytes=limit)


def _largest_tile(n, cap, align):
    t = min(cap, n) // align * align
    while t > align and n % t:
        t -= align
    assert t > 0 and n % t == 0, (n, cap, align)
    return t


def _split3(x):
    hi = x.astype(BF16)
    r1 = x - hi.astype(F32)
    mid = r1.astype(BF16)
    lo = (r1 - mid.astype(F32)).astype(BF16)
    return hi, mid, lo


def _rms(x):
    return x * lax.rsqrt(jnp.mean(x * x, axis=-1, keepdims=True) + NORM_EPS)


def _norm_mod(x, g, shift, scale):
    return (_rms(x) * g) * (1.0 + scale) + shift


def _softplus(y):
    return jnp.maximum(y, 0.0) + jnp.log1p(jnp.exp(-jnp.abs(y)))


def _ada_kernel(c_ref, w_ref, b_ref, o_ref):
    c = c_ref[...]
    s = c * jax.nn.sigmoid(c)
    s_hi = s.astype(BF16)
    s_lo = (s - s_hi.astype(F32)).astype(BF16)
    w = w_ref[0]
    w_hi = w.astype(BF16)
    w_lo = (w - w_hi.astype(F32)).astype(BF16)
    p = jnp.dot(jnp.concatenate([s_hi, s_lo], axis=0), w_hi, preferred_element_type=F32)
    q = jnp.dot(s_hi, w_lo, preferred_element_type=F32)
    o_ref[0] = p[:MOD_ROWS] + p[MOD_ROWS:] + q + b_ref[0]


def _ada_mod(c_rows, ada_w, ada_b):
    depth, d, n = ada_w.shape
    tn = 1024
    out = pl.pallas_call(
        _ada_kernel,
        out_shape=jax.ShapeDtypeStruct((depth, MOD_ROWS, n), F32),
        grid=(depth, n // tn),
        in_specs=[pl.BlockSpec((MOD_ROWS, d), lambda l, j: (0, 0)),
                  pl.BlockSpec((1, d, tn), lambda l, j: (l, 0, j)),
                  pl.BlockSpec((1, 1, tn), lambda l, j: (l, 0, j))],
        out_specs=pl.BlockSpec((1, MOD_ROWS, tn), lambda l, j: (l, 0, j)),
        compiler_params=_cparams(("parallel", "parallel"), 2 * d * tn * 4 + 4 * d * tn),
        name="ada_mod",
    )(c_rows, ada_w, ada_b.reshape(depth, 1, n))
    return out.reshape(depth * MOD_ROWS * N_MOD, 1, d)


def _mod_row(layer, row, j):
    return (layer * MOD_ROWS + row) * N_MOD + j


def _pre_rwkv_kernel(hc_ref, sc_ref_ctx, xm_ref, xu_ref, xd_ref, g_ref, sh_ref, sc_ref, mu_ref,
                     *o_refs, n_tiles):
    t = pl.program_id(1)
    d = xm_ref.shape[-1]
    g = g_ref[...]
    mu = mu_ref[...]

    def emit(h, s, lo, hi):
        xx = s - h
        for j in range(6):
            o_refs[j][0, :, lo:hi] = (h + xx * mu[j:j + 1, lo:hi]).astype(BF16)

    @pl.when(t == 0)
    def _():
        emit(hc_ref[0], sc_ref_ctx[0], 0, d)

    @pl.when(t > 0)
    def _():
        sh = sh_ref[0]
        sc = sc_ref[0]
        xm = xm_ref[0].reshape(TOKEN_TILE, d)
        hm = _norm_mod(xm, g, sh, sc)
        hu = _norm_mod(xu_ref[0, 0], g, sh, sc) * jnp.where(t > 1, 1.0, 0.0)
        hd = _norm_mod(xd_ref[0, 0], g, sh, sc) * jnp.where(t < n_tiles, 1.0, 0.0)
        col = lax.broadcasted_iota(jnp.int32, (TOKEN_TILE, 1), 0) & (GRID_W - 1)
        q = d // 4
        left = jnp.where(col != 0, pltpu.roll(hm[:, :q], 1, 0), 0.0)
        right = jnp.where(col != GRID_W - 1, pltpu.roll(hm[:, q:2 * q], TOKEN_TILE - 1, 0), 0.0)
        up = jnp.concatenate([hu[:, 2 * q:3 * q], hm[:TOKEN_TILE - GRID_W, 2 * q:3 * q]], axis=0)
        down = jnp.concatenate([hm[GRID_W:, 3 * q:], hd[:, 3 * q:]], axis=0)
        emit(hm[:, :q], left, 0, q)
        emit(hm[:, q:2 * q], right, q, 2 * q)
        emit(hm[:, 2 * q:3 * q], up, 2 * q, 3 * q)
        emit(hm[:, 3 * q:], down, 3 * q, d)


def _shift_seq(x):
    half = x.shape[-1] // 2
    p = jnp.pad(x, ((0, 0), (1, 1), (0, 0)))
    return jnp.concatenate([p[:, :-2, :half], p[:, 2:, half:]], axis=-1)


def _pre_rwkv(x, ctx, mod3, layer, g0, mu):
    b, l, d = x.shape
    n_ctx = ctx.shape[1]
    assert n_ctx == TOKEN_TILE and l % TOKEN_TILE == 0
    n_tiles = l // TOKEN_TILE
    rows_per_tile = TOKEN_TILE // GRID_W
    n_rows = l // GRID_W
    x4 = x.reshape(b, n_rows, GRID_W, d)
    s = n_ctx + l
    hc = _pre_norm(ctx.reshape(b * n_ctx, d), mod3, layer, g0, n_ctx, tm=n_ctx, fixed_row=b,
                   out_dtype=F32).reshape(b, n_ctx, d)
    sc = _shift_seq(hc)

    def mrow(j):
        return pl.BlockSpec((1, 1, d), lambda bi, t: (_mod_row(layer, bi, j), 0, 0))

    main = lambda bi, t: (bi, jnp.maximum(t - 1, 0), 0, 0)
    up = lambda bi, t: (bi, jnp.maximum((t - 1) * rows_per_tile - 1, 0), 0, 0)
    down = lambda bi, t: (bi, jnp.minimum(jnp.maximum(t, 1) * rows_per_tile, n_rows - 1), 0, 0)
    out_sds = jax.ShapeDtypeStruct((b, s, d), BF16)
    outs = pl.pallas_call(
        functools.partial(_pre_rwkv_kernel, n_tiles=n_tiles),
        out_shape=[out_sds] * 6,
        grid=(b, n_tiles + 1),
        in_specs=[pl.BlockSpec((1, n_ctx, d), lambda bi, t: (bi, 0, 0)),
                  pl.BlockSpec((1, n_ctx, d), lambda bi, t: (bi, 0, 0)),
                  pl.BlockSpec((1, rows_per_tile, GRID_W, d), main),
                  pl.BlockSpec((1, 1, GRID_W, d), up),
                  pl.BlockSpec((1, 1, GRID_W, d), down),
                  pl.BlockSpec((1, d), lambda bi, t: (0, 0)),
                  mrow(0), mrow(1),
                  pl.BlockSpec((6, d), lambda bi, t: (0, 0))],
        out_specs=[pl.BlockSpec((1, TOKEN_TILE, d), lambda bi, t: (bi, t, 0))] * 6,
        compiler_params=_cparams(("parallel", "arbitrary"),
                                 2 * (3 * TOKEN_TILE * d * 4 + 2 * GRID_W * d * 4 + 6 * TOKEN_TILE * d * 2)),
        name="rwkv_pre",
    )(hc, sc, x4, x4, x4, g0.reshape(1, d), mod3, mod3, mu)
    return outs


def _mm_kernel(a_ref, w_ref, *rest, act, has_bias):
    o_ref = rest[-1]
    acc = jnp.dot(a_ref[...], w_ref[...], preferred_element_type=F32)
    if has_bias:
        acc = acc + rest[0][...]
    if act == "tanh":
        acc = jnp.tanh(acc)
    elif act == "sigmoid":
        acc = jax.nn.sigmoid(acc)
    o_ref[...] = acc.astype(o_ref.dtype)


def _matmul(a, w, bias=None, act=None, out_dtype=F32, tm=1024, tn=1024, name="matmul"):
    m, k = a.shape
    n = w.shape[1]
    tm = _largest_tile(m, tm, 8)
    tn = _largest_tile(n, tn, LANES)
    in_specs = [pl.BlockSpec((tm, k), lambda i, j: (i, 0)),
                pl.BlockSpec((k, tn), lambda i, j: (0, j))]
    args = [a, w]
    if bias is not None:
        in_specs.append(pl.BlockSpec((1, tn), lambda i, j: (0, j)))
        args.append(bias.reshape(1, n).astype(F32))
    est = 2 * (tm * k * 2 + k * tn * 2 + tm * tn * jnp.dtype(out_dtype).itemsize) + tm * tn * 4
    return pl.pallas_call(
        functools.partial(_mm_kernel, act=act, has_bias=bias is not None),
        out_shape=jax.ShapeDtypeStruct((m, n), out_dtype),
        grid=(m // tm, n // tn),
        in_specs=in_specs,
        out_specs=pl.BlockSpec((tm, tn), lambda i, j: (i, j)),
        compiler_params=_cparams(("parallel", "parallel"), est),
        name=name,
    )(*args)


def _seg_sum(x, ones_bd):
    hi, mid, lo = _split3(x)
    r = x.shape[0]
    p = jnp.dot(jnp.concatenate([hi, mid, lo], axis=0), ones_bd, preferred_element_type=F32)
    return p[:r] + p[r:2 * r] + p[2 * r:]


def _bd(x, head0):
    return jnp.concatenate([jnp.where(head0, x, 0.0), jnp.where(head0, 0.0, x)], axis=0)


def _wkv_constants(rev):
    c = CHUNK
    shift = int(math.log2(c))
    head0 = lax.broadcasted_iota(jnp.int32, (1, LANES), 1) < HEAD_SIZE
    row2 = lax.broadcasted_iota(jnp.int32, (2 * c, 2 * c), 0)
    col2 = lax.broadcasted_iota(jnp.int32, (2 * c, 2 * c), 1)
    same = (row2 >> shift) == (col2 >> shift)
    tt = row2 & (c - 1)
    ss = col2 & (c - 1)
    if rev:
        tt, ss = ss, tt
    ones_bd = jnp.where(same, 1.0, 0.0).astype(BF16)
    eye = jnp.where(row2 == col2, 1.0, 0.0).astype(F32)
    rowc = lax.broadcasted_iota(jnp.int32, (c, c), 0)
    colc = lax.broadcasted_iota(jnp.int32, (c, c), 1)
    tri = jnp.where(colc >= rowc if rev else colc <= rowc, 1.0, 0.0).astype(BF16)
    merge = tuple(((tt >> lv) == (ss >> lv) + 1) & ((tt >> (lv + 1)) == (ss >> (lv + 1))) for lv in range(shift))
    return head0, ones_bd, tri, ss < tt, ss <= tt, eye, merge


def _each(fn, *lists):
    return [fn(*args) for args in zip(*lists)]


def _mm(a, b):
    return jnp.dot(a.astype(BF16), b.astype(BF16), preferred_element_type=F32)


def _wkv_prepare(probs, k_a, hooks=()):
    hooks = list(hooks)

    def run_hook():
        if hooks:
            hooks.pop(0)()

    revs, rs, ks, kks, bd_vs, lwxs, axs, w0s, a0s, csts = [list(t) for t in zip(*probs)]
    head0 = csts[0][0]
    eye = csts[0][5]
    c = rs[0].shape[0]
    c2 = 2 * c
    logw = _each(lambda w0, lwx: -jnp.exp(-_softplus(-(w0 + lwx)) - 0.5), w0s, lwxs)
    a = _each(lambda a0, ax: jax.nn.sigmoid(a0 + ax), a0s, axs)
    kd = _each(lambda k, a_: k * (1.0 + (a_ - 1.0) * k_a), ks, a)
    b = _each(lambda kk, a_: kk * a_, kks, a)

    def cumsum(lw, cst):
        hi, mid, lo = _split3(lw)
        c3 = jnp.dot(cst[2], jnp.concatenate([hi, mid, lo], axis=1), preferred_element_type=F32)
        return c3[:, :LANES] + c3[:, LANES:2 * LANES] + c3[:, 2 * LANES:]

    cum = _each(cumsum, logw, csts)
    total = _each(lambda cm, rev: cm[0:1] if rev else cm[c - 1:c], cum, revs)
    kap_t = _each(lambda kk, cm, lw: kk * jnp.exp(cm - lw), kks, cum, logw)
    r_t = _each(lambda r, cm: r * jnp.exp(cm), rs, cum)
    igam = _each(lambda cm: jnp.exp(-cm), cum)
    tail = _each(lambda t, cm: jnp.exp(t - cm), total, cum)

    bd_kap = _each(lambda x: _bd(x, head0), kap_t)
    bd_r = _each(lambda x: _bd(x, head0), r_t)
    lhs = _each(lambda p, q: jnp.concatenate([p, q], axis=0).astype(BF16), bd_kap, bd_r)
    rhs = _each(lambda kd_, b_, ig: jnp.concatenate([_bd(kd_ * ig, head0), _bd(b_ * ig, head0)],
                                                    axis=0).astype(BF16), kd, b, igam)
    gmat = _each(lambda l_, r_: lax.dot_general(l_, r_, (((1,), (1,)), ((), ())),
                                                preferred_element_type=F32), lhs, rhs)
    a_kb = _each(lambda g, cst: jnp.where(cst[3], g[:c2, c2:], 0.0), gmat, csts)
    run_hook()

    tinv = _each(lambda akb, cst: eye - jnp.where(cst[6][0], akb, 0.0), a_kb, csts)
    for lv in range(1, len(csts[0][6])):
        y = _each(lambda akb, x, cst: _mm(jnp.where(cst[6][lv], akb, 0.0), x), a_kb, tinv, csts)
        tinv = _each(lambda x, y_: x - _mm(x, y_), tinv, y)
        if lv % 2 == 0:
            run_hook()

    a_kr = _each(lambda g, cst: jnp.concatenate([jnp.where(cst[3], g[:c2, :c2], 0.0),
                                                 jnp.where(cst[4], g[c2:, :c2], 0.0)], axis=0), gmat, csts)
    av = _each(_mm, a_kr, bd_vs)
    while hooks:
        run_hook()
    wu = _each(lambda x, kap, av_: _mm(x, jnp.concatenate([kap, av_[:c2]], axis=1)), tinv, bd_kap, av)
    rb = _each(lambda g, cst, wu_: _mm(jnp.where(cst[4], g[c2:, c2:], 0.0), wu_), gmat, csts, wu)
    bct = _each(lambda b_, tl: _bd(b_ * tl, head0).T, b, tail)
    kct = _each(lambda kd_, tl: _bd(kd_ * tl, head0).T, kd, tail)
    mn = _each(_mm, bct, wu)
    kv = _each(_mm, kct, bd_vs)
    out = []
    for i in range(len(probs)):
        rk = bd_r[i] - rb[i][:, :LANES]
        ov = av[i][c2:] - rb[i][:, LANES:]
        m_mat = jnp.where(eye > 0.0, jnp.exp(total[i]), 0.0) - mn[i][:, :LANES]
        n_mat = kv[i] - mn[i][:, LANES:]
        out.append((jnp.concatenate([rk, m_mat], axis=0).astype(BF16), ov[:c] + ov[c:], n_mat))
    return out


def _wkv_apply(rm_lhs, ov, n_mat, h_state):
    c = ov.shape[0]
    rm = jnp.dot(rm_lhs, h_state.astype(BF16), preferred_element_type=F32)
    return rm[:c] + rm[c:2 * c] + ov, rm[2 * c:] + n_mat


def _wkv_kernel(r_ref, k_ref, v_ref, lwf_ref, lwb_ref, af_ref, ab_ref, g_ref,
                w0_ref, a0_ref, kk_ref, ka_ref, rk_ref, lnw_ref, lnb_ref,
                z_ref, of_scr, ob_scr, rm_scr, ov_scr, n_scr, *, n_ctx_chunks, n_chunks, unroll):
    c = CHUNK
    cst_f = _wkv_constants(False)
    cst_b = _wkv_constants(True)
    head0, ones_bd = cst_f[0], cst_f[1]

    w0f, w0b = w0_ref[0:1, :], w0_ref[1:2, :]
    a0f, a0b = a0_ref[0:1, :], a0_ref[1:2, :]
    k_k = kk_ref[...]
    k_a = ka_ref[...]

    def load(ref, rows):
        return ref[0, rows, :].astype(F32)

    def chunk_of(i, rev):
        if not rev:
            return i
        return jnp.where(i < n_ctx_chunks, n_ctx_chunks - 1 - i, n_chunks + n_ctx_chunks - 1 - i)

    def prepare_group(grp, hooks=()):
        probs, slots = [], []
        for u in range(unroll):
            for rev, lw_ref, a_ref, w0, a0, cst in ((False, lwf_ref, af_ref, w0f, a0f, cst_f),
                                                     (True, lwb_ref, ab_ref, w0b, a0b, cst_b)):
                ci = chunk_of(grp * unroll + u, rev)
                rows = pl.ds(pl.multiple_of(ci * c, c), c)
                k = load(k_ref, rows)
                kk0 = k * k_k
                kk = kk0 * lax.rsqrt(jnp.maximum(_seg_sum(kk0 * kk0, ones_bd), 1e-24))
                bd_v = _bd(load(v_ref, rows), head0).astype(BF16)
                probs.append((rev, load(r_ref, rows), k, kk, bd_v, load(lw_ref, rows), load(a_ref, rows),
                              w0, a0, cst))
                slots.append(ci + n_chunks * int(rev))
        for slot, (rm_lhs, ov, n_mat) in zip(slots, _wkv_prepare(probs, k_a, hooks)):
            rm_scr[slot] = rm_lhs
            ov_scr[slot] = ov
            n_scr[slot] = n_mat

    def state_steps(grp, state):
        def step(i):
            for rev, o_scr in ((False, of_scr), (True, ob_scr)):
                ci = chunk_of(i, rev)
                slot = ci + n_chunks * int(rev)
                o, state[int(rev)] = _wkv_apply(rm_scr[slot], ov_scr[slot], n_scr[slot], state[int(rev)])
                o_scr[pl.ds(pl.multiple_of(ci * c, c), c), :] = o
        return [functools.partial(step, grp * unroll + u) for u in range(unroll)]

    n_groups = n_chunks // unroll
    prepare_group(0)

    def body(grp, carry):
        state = list(carry)
        prepare_group(grp, state_steps(grp - 1, state))
        return tuple(state)

    zero = jnp.zeros((2 * c, LANES), F32)
    state = list(lax.fori_loop(1, n_groups, body, (zero, zero)))
    for step in state_steps(n_groups - 1, state):
        step()

    r_k = rk_ref[...]
    lnw = lnw_ref[...]
    lnb = lnb_ref[...]
    blk = TOKEN_TILE
    n_ctx = n_ctx_chunks * c
    inv_n = 1.0 / HEAD_SIZE

    def read_body(j, _):
        rows = pl.ds(pl.multiple_of(n_ctx + j * blk, blk), blk)
        o = of_scr[rows, :] + ob_scr[rows, :]
        mean = _seg_sum(o, ones_bd) * inv_n
        dev = o - mean
        var = _seg_sum(dev * dev, ones_bd) * inv_n
        on = dev * lax.rsqrt(var + LNX_EPS) * lnw + lnb
        r = load(r_ref, rows)
        k = load(k_ref, rows)
        v = load(v_ref, rows)
        a_sum = jax.nn.sigmoid(a0f + load(af_ref, rows)) + jax.nn.sigmoid(a0b + load(ab_ref, rows))
        kd_sum = k * (2.0 + (a_sum - 2.0) * k_a)
        bonus = _seg_sum(r * kd_sum * r_k, ones_bd) * v
        z = (on + bonus) * g_ref[0, rows, :]
        z_ref[0, pl.ds(pl.multiple_of(j * blk, blk), blk), :] = z.astype(z_ref.dtype)
        return 0

    lax.fori_loop(0, (n_chunks - n_ctx_chunks) * c // blk, read_body, 0)


def _wkv(r, k, v, lw, a, g, dec_w0, a0, k_k, k_a, r_k, lnx_w, lnx_b, n_ctx):
    b, s, d = r.shape
    l = s - n_ctx
    npair = d // LANES
    seq = lambda bi, p: (bi, 0, p)
    seq_b = lambda bi, p: (bi, 0, p + npair)
    vec = lambda bi, p: (0, p)
    sblk = pl.BlockSpec((1, s, LANES), seq)
    n_chunks = s // CHUNK
    unroll = next(u for u in (4, 2, 1) if n_chunks % u == 0)
    scratch = [pltpu.VMEM((s, LANES), F32), pltpu.VMEM((s, LANES), F32),
               pltpu.VMEM((2 * n_chunks, 4 * CHUNK, LANES), BF16),
               pltpu.VMEM((2 * n_chunks, CHUNK, LANES), F32),
               pltpu.VMEM((2 * n_chunks, 2 * CHUNK, LANES), F32)]
    est = (2 * (3 * s * LANES * 2 + 5 * s * LANES * 4 + l * LANES * 2) + 2 * s * LANES * 4
           + 2 * n_chunks * CHUNK * LANES * (4 * 2 + 4 + 2 * 4))
    return pl.pallas_call(
        functools.partial(_wkv_kernel, n_ctx_chunks=n_ctx // CHUNK, n_chunks=n_chunks, unroll=unroll),
        out_shape=jax.ShapeDtypeStruct((b, l, d), BF16),
        grid=(b, npair),
        in_specs=[sblk, sblk, sblk,
                  pl.BlockSpec((1, s, LANES), seq), pl.BlockSpec((1, s, LANES), seq_b),
                  pl.BlockSpec((1, s, LANES), seq), pl.BlockSpec((1, s, LANES), seq_b),
                  pl.BlockSpec((1, s, LANES), seq),
                  pl.BlockSpec((2, LANES), vec), pl.BlockSpec((2, LANES), vec),
                  pl.BlockSpec((1, LANES), vec), pl.BlockSpec((1, LANES), vec), pl.BlockSpec((1, LANES), vec),
                  pl.BlockSpec((1, LANES), vec), pl.BlockSpec((1, LANES), vec)],
        out_specs=pl.BlockSpec((1, l, LANES), seq),
        scratch_shapes=scratch,
        compiler_params=_cparams(("parallel", "parallel"), est),
        name="wkv_scan",
    )(r, k, v, lw, lw, a, a, g, dec_w0, a0, k_k.reshape(1, d), k_a.reshape(1, d), r_k.reshape(1, d),
      lnx_w.reshape(1, d), lnx_b.reshape(1, d))


def _post_kernel(x_ref, y_ref, g1_ref, g2_ref, gate_ref, sh_ref, sc_ref, xo_ref, h_ref):
    x = x_ref[...] + gate_ref[0] * (_rms(y_ref[...]) * g1_ref[...])
    xo_ref[...] = x
    h_ref[...] = _norm_mod(x, g2_ref[...], sh_ref[0], sc_ref[0]).astype(h_ref.dtype)


def _post_mixer(x2, y2, mod3, layer, g1, g2, l, tm=512):
    m, d = x2.shape
    per_b = l // tm
    row = lambda j: pl.BlockSpec((1, 1, d), lambda i: (_mod_row(layer, i // per_b, j), 0, 0))
    tile = pl.BlockSpec((tm, d), lambda i: (i, 0))
    vec = pl.BlockSpec((1, d), lambda i: (0, 0))
    return pl.pallas_call(
        _post_kernel,
        out_shape=[jax.ShapeDtypeStruct((m, d), F32), jax.ShapeDtypeStruct((m, d), BF16)],
        grid=(m // tm,),
        in_specs=[tile, tile, vec, vec, row(2), row(3), row(4)],
        out_specs=[tile, tile],
        compiler_params=_cparams(("parallel",), 2 * tm * d * (4 + 4 + 4 + 2)),
        name="post_mixer",
    )(x2, y2, g1.reshape(1, d), g2.reshape(1, d), mod3, mod3, mod3)


def _mlp_kernel(h_ref, wu_ref, wd_ref, x_ref, g_ref, gate_ref, o_ref, acc_ref):
    kf = pl.program_id(1)

    @pl.when(kf == 0)
    def _():
        acc_ref[...] = jnp.zeros_like(acc_ref)

    u = jnp.dot(h_ref[...], wu_ref[...], preferred_element_type=F32)
    u = jnp.square(jnp.maximum(u, 0.0)).astype(BF16)
    acc_ref[...] += jnp.dot(u, wd_ref[...], preferred_element_type=F32)

    @pl.when(kf == pl.num_programs(1) - 1)
    def _():
        o_ref[...] = x_ref[...] + gate_ref[0] * (_rms(acc_ref[...]) * g_ref[...])


def _mlp(h2, w_up, w_down, x2, mod3, layer, g3, l, tm=512, tf=1024):
    m, d = h2.shape
    dff = w_up.shape[1]
    per_b = l // tm
    est = 2 * (tm * d * 2 + 2 * d * tf * 2 + 2 * tm * d * 4) + tm * d * 4 + tm * tf * 6
    return pl.pallas_call(
        _mlp_kernel,
        out_shape=jax.ShapeDtypeStruct((m, d), F32),
        grid=(m // tm, dff // tf),
        in_specs=[pl.BlockSpec((tm, d), lambda i, f: (i, 0)),
                  pl.BlockSpec((d, tf), lambda i, f: (0, f)),
                  pl.BlockSpec((tf, d), lambda i, f: (f, 0)),
                  pl.BlockSpec((tm, d), lambda i, f: (i, 0)),
                  pl.BlockSpec((1, d), lambda i, f: (0, 0)),
                  pl.BlockSpec((1, 1, d), lambda i, f: (_mod_row(layer, i // per_b, 5), 0, 0))],
        out_specs=pl.BlockSpec((tm, d), lambda i, f: (i, 0)),
        scratch_shapes=[pltpu.VMEM((tm, d), F32)],
        compiler_params=_cparams(("parallel", "arbitrary"), est),
        name="mlp",
    )(h2, w_up, w_down, x2, g3.reshape(1, d), mod3)


def _pre_norm_kernel(x_ref, g_ref, sh_ref, sc_ref, h_ref):
    h_ref[...] = _norm_mod(x_ref[...], g_ref[...], sh_ref[0], sc_ref[0]).astype(h_ref.dtype)


def _pre_norm(x2, mod3, layer, g0, l, tm=512, fixed_row=None, out_dtype=BF16):
    m, d = x2.shape
    per_b = l // tm
    if fixed_row is None:
        row = lambda j: pl.BlockSpec((1, 1, d), lambda i: (_mod_row(layer, i // per_b, j), 0, 0))
    else:
        row = lambda j: pl.BlockSpec((1, 1, d), lambda i: (_mod_row(layer, fixed_row, j), 0, 0))
    tile = pl.BlockSpec((tm, d), lambda i: (i, 0))
    return pl.pallas_call(
        _pre_norm_kernel,
        out_shape=jax.ShapeDtypeStruct((m, d), out_dtype),
        grid=(m // tm,),
        in_specs=[tile, pl.BlockSpec((1, d), lambda i: (0, 0)), row(0), row(1)],
        out_specs=tile,
        compiler_params=_cparams(("parallel",), 2 * tm * d * 6),
        name="pre_norm",
    )(x2, g0.reshape(1, d), mod3, mod3)


def _filter_kernel(z_ref, w1_ref, w2_ref, w3_ref, b_ref, fr_ref, w4_ref, t_ref, dl_ref, o_ref, hid_ref):
    hp = lax.Precision.HIGHEST

    @pl.when(pl.program_id(0) == 0)
    def _():
        b = b_ref[...]
        fr = fr_ref[...]
        z = jnp.sin(fr[0:1] * (jnp.dot(z_ref[...], w1_ref[...], precision=hp, preferred_element_type=F32) + b[0:1]))
        z = jnp.sin(fr[1:2] * (jnp.dot(z, w2_ref[...], precision=hp, preferred_element_type=F32) + b[1:2]))
        hid_ref[...] = jnp.sin(fr[2:3] * (jnp.dot(z, w3_ref[...], precision=hp, preferred_element_type=F32)
                                          + b[2:3]))

    filt = jnp.dot(hid_ref[...], w4_ref[...], precision=hp, preferred_element_type=F32)
    o_ref[...] = filt * jnp.exp(-t_ref[...] * dl_ref[...])


def _hyena_filters(l, d, f_w1, f_w23, f_w4, f_b, f_freq):
    t = jnp.linspace(0.0, 1.0, l, dtype=F32)[:, None]
    bands = (HY_EMB_DIM - 1) // 2
    freqs = jnp.linspace(1e-4, bands - 1, bands, dtype=F32)[None, :]
    ang = (2.0 * math.pi / l) * jnp.arange(l, dtype=F32)[:, None] * freqs
    z = jnp.concatenate([t, jnp.cos(ang), -jnp.sin(ang)], axis=-1)
    e, f = f_w1.shape
    pad = lambda a_, r, c: jnp.pad(a_.astype(F32), ((0, r - a_.shape[0]), (0, c - a_.shape[1])))
    zp = pad(z, l, LANES)
    w1 = pad(f_w1, LANES, LANES)
    w2 = pad(f_w23[0], LANES, LANES)
    w3 = pad(f_w23[1], LANES, LANES)
    bb = pad(f_b, 8, LANES)
    fr = pad(f_freq, 8, LANES)
    n = f_w4.shape[1]
    w4 = pad(f_w4, LANES, n)
    max_decay = math.log(HY_DECAY_TARGET) / HY_FAST_DECAY
    min_decay = math.log(HY_DECAY_TARGET) / HY_SLOW_DECAY
    deltas = jnp.abs(jnp.linspace(min_decay, max_decay, d, dtype=F32))[None, :]
    tn = _largest_tile(d, 1024, LANES)
    per_d = d // tn
    sq = pl.BlockSpec((LANES, LANES), lambda j: (0, 0))
    small = pl.BlockSpec((8, LANES), lambda j: (0, 0))
    return pl.pallas_call(
        _filter_kernel,
        out_shape=jax.ShapeDtypeStruct((l, n), F32),
        grid=(n // tn,),
        in_specs=[pl.BlockSpec((l, LANES), lambda j: (0, 0)), sq, sq, sq, small, small,
                  pl.BlockSpec((LANES, tn), lambda j: (0, j)),
                  pl.BlockSpec((l, 1), lambda j: (0, 0)),
                  pl.BlockSpec((1, tn), lambda j: (0, j % per_d))],
        out_specs=pl.BlockSpec((l, tn), lambda j: (0, j)),
        scratch_shapes=[pltpu.VMEM((l, LANES), F32)],
        compiler_params=_cparams(("arbitrary",), 4 * l * tn * 4),
        name="hyena_filters",
    )(zp, w1, w2, w3, bb, fr, w4, t, deltas)


def _dft_matrices(l):
    n = 2 * l
    k = jnp.arange(l, dtype=jnp.int32)[:, None]
    t = jnp.arange(l, dtype=jnp.int32)[None, :]
    ang = ((k * t) % n).astype(F32) * (2.0 * math.pi / n)
    cos = jnp.cos(ang)
    msin = jnp.where(k == 0, jnp.where(t % 2 == 0, 1.0, -1.0), -jnp.sin(ang))
    fwd = jnp.concatenate([cos, msin], axis=0)
    return fwd.astype(BF16), fwd.T.astype(BF16)


def _short_conv(z, cw, cb):
    n = z.shape[0]
    row = lax.broadcasted_iota(jnp.int32, (n, 1), 0)
    prev = jnp.where(row != 0, pltpu.roll(z, 1, 0), 0.0)
    nxt = jnp.where(row != n - 1, pltpu.roll(z, n - 1, 0), 0.0)
    return prev * cw[0:1] + z * cw[1:2] + nxt * cw[2:3] + cb


def _spec_kernel(f_ref, hf_ref, hb_ref, o_ref):
    l, tn = hf_ref.shape
    row = lax.broadcasted_iota(jnp.int32, (l, 1), 0)
    hb = jnp.where(row == 0, 0.0, hb_ref[...])
    h2 = jnp.concatenate([hf_ref[...], hb], axis=1).astype(BF16)
    u = jnp.dot(f_ref[...], h2, preferred_element_type=F32)
    row2 = lax.broadcasted_iota(jnp.int32, (2 * l, 1), 0)
    sign = jnp.where(row2 > l, -1.0, 1.0)
    scale = jnp.where((row2 == 0) | (row2 == l), 1.0 / (2 * l), 2.0 / (2 * l))
    o_ref[0] = (u[:, :tn] + sign * u[:, tn:]) * scale


def _spectrum(fwd, filt, d, tn=256):
    l = filt.shape[0]
    n2 = fwd.shape[0]
    orders = filt.shape[1] // (2 * d)
    per = d // tn
    return pl.pallas_call(
        _spec_kernel,
        out_shape=jax.ShapeDtypeStruct((orders, n2, d), F32),
        grid=(orders, per),
        in_specs=[pl.BlockSpec((n2, l), lambda o, j: (0, 0), pipeline_mode=pl.Buffered(1)),
                  pl.BlockSpec((l, tn), lambda o, j: (0, o * per + j)),
                  pl.BlockSpec((l, tn), lambda o, j: (0, (orders + o) * per + j))],
        out_specs=pl.BlockSpec((1, n2, tn), lambda o, j: (o, 0, j)),
        compiler_params=_cparams(("parallel", "parallel"), n2 * l * 2 + 2 * (2 * l * tn * 4 + n2 * tn * 4)
                                 + 3 * n2 * tn * 4),
        name="filter_spectrum",
    )(fwd, filt, filt)


def _conv_fwd_kernel(f_ref, y_ref, k_ref, cw_ref, cb_ref, p_ref, *, short_conv, n_split):
    l = y_ref.shape[1]
    y = y_ref[0]
    if short_conv:
        y = _short_conv(y, cw_ref[...], cb_ref[...])
    yb = y.astype(BF16)
    rows = l // n_split
    for s in range(n_split):
        lo, hi = s * rows, (s + 1) * rows
        ure = jnp.dot(f_ref[lo:hi, :], yb, preferred_element_type=F32)
        uim = jnp.dot(f_ref[l + lo:l + hi, :], yb, preferred_element_type=F32)
        kre = k_ref[0, lo:hi, :]
        kim = k_ref[0, l + lo:l + hi, :]
        pre = ure * kre - uim * kim
        pim = ure * kim + uim * kre
        if s == 0:
            first = lax.broadcasted_iota(jnp.int32, (rows, 1), 0) == 0
            pre = jnp.where(first, ure * kre, pre)
            pim = jnp.where(first, uim * kim, pim)
        p_ref[0, lo:hi, :] = pre.astype(p_ref.dtype)
        p_ref[0, l + lo:l + hi, :] = pim.astype(p_ref.dtype)


def _conv_fwd(fwd, y, y_col0, kspec, order, conv_w, conv_b, short_conv, d, tn=256):
    b, l, _ = y.shape
    n2 = fwd.shape[0]
    off = y_col0 // tn
    est = n2 * l * 2 + 2 * (l * tn * 4 + n2 * tn * 4 + n2 * tn * 2) + 6 * l * tn * 4
    return pl.pallas_call(
        functools.partial(_conv_fwd_kernel, short_conv=short_conv, n_split=2),
        out_shape=jax.ShapeDtypeStruct((b, n2, d), BF16),
        grid=(d // tn, b),
        in_specs=[pl.BlockSpec((n2, l), lambda j, bi: (0, 0), pipeline_mode=pl.Buffered(1)),
                  pl.BlockSpec((1, l, tn), lambda j, bi: (bi, 0, j + off)),
                  pl.BlockSpec((1, n2, tn), lambda j, bi: (order, 0, j)),
                  pl.BlockSpec((3, tn), lambda j, bi: (0, j + off)),
                  pl.BlockSpec((1, tn), lambda j, bi: (0, j + off))],
        out_specs=pl.BlockSpec((1, n2, tn), lambda j, bi: (bi, 0, j)),
        compiler_params=_cparams(("parallel", "parallel"), est),
        name="hyena_conv_fwd",
    )(fwd, y, kspec, conv_w, conv_b)


def _conv_inv_kernel(ft_ref, p_ref, yp_ref, gt_ref, cwy_ref, cby_ref, cwg_ref, cbg_ref, sk_ref, o_ref,
                     *, short_conv_prev):
    conv = jnp.dot(ft_ref[...], p_ref[0], preferred_element_type=F32)
    yp = yp_ref[0]
    if short_conv_prev:
        yp = _short_conv(yp, cwy_ref[...], cby_ref[...])
    gate = _short_conv(gt_ref[0], cwg_ref[...], cbg_ref[...])
    o_ref[0] = (gate * (conv + sk_ref[...] * yp)).astype(o_ref.dtype)


def _conv_inv(finv, p, yprev, yprev_col0, short_conv_prev, z, gate_col0, conv_w, conv_b, skip, out_dtype, tn=256):
    b, n2, d = p.shape
    l = n2 // 2
    offy = yprev_col0 // tn
    offg = gate_col0 // tn
    est = l * n2 * 2 + 2 * (n2 * tn * 2 + 3 * l * tn * 4) + 6 * l * tn * 4
    return pl.pallas_call(
        functools.partial(_conv_inv_kernel, short_conv_prev=short_conv_prev),
        out_shape=jax.ShapeDtypeStruct((b, l, d), out_dtype),
        grid=(d // tn, b),
        in_specs=[pl.BlockSpec((l, n2), lambda j, bi: (0, 0), pipeline_mode=pl.Buffered(1)),
                  pl.BlockSpec((1, n2, tn), lambda j, bi: (bi, 0, j)),
                  pl.BlockSpec((1, l, tn), lambda j, bi: (bi, 0, j + offy)),
                  pl.BlockSpec((1, l, tn), lambda j, bi: (bi, 0, j + offg)),
                  pl.BlockSpec((3, tn), lambda j, bi: (0, j + offy)),
                  pl.BlockSpec((1, tn), lambda j, bi: (0, j + offy)),
                  pl.BlockSpec((3, tn), lambda j, bi: (0, j + offg)),
                  pl.BlockSpec((1, tn), lambda j, bi: (0, j + offg)),
                  pl.BlockSpec((1, tn), lambda j, bi: (0, j))],
        out_specs=pl.BlockSpec((1, l, tn), lambda j, bi: (bi, 0, j)),
        compiler_params=_cparams(("parallel", "parallel"), est),
        name="hyena_conv_inv",
    )(finv, p, yprev, z, conv_w, conv_b, conv_w, conv_b, skip)


def _hyena_mixer(h2, b, l, in_w, in_b, conv_w, conv_b, f_w1, f_w23, f_w4, f_b, f_freq, skip, out_w, out_b):
    d = h2.shape[1]
    z = _matmul(h2, in_w.astype(BF16), bias=in_b, name="hyena_in").reshape(b, l, 3 * d)
    filt = _hyena_filters(l, d, f_w1, f_w23, f_w4, f_b, f_freq)
    fwd, finv = _dft_matrices(l)
    kspec = _spectrum(fwd, filt, d)
    skip = skip.astype(F32)
    p0 = _conv_fwd(fwd, z, 0, kspec, 0, conv_w, conv_b.reshape(1, -1), True, d)
    y1 = _conv_inv(finv, p0, z, 0, True, z, d, conv_w, conv_b.reshape(1, -1), skip[0:1], F32)
    p1 = _conv_fwd(fwd, y1, 0, kspec, 1, conv_w, conv_b.reshape(1, -1), False, d)
    y2 = _conv_inv(finv, p1, y1, 0, False, z, 2 * d, conv_w, conv_b.reshape(1, -1), skip[1:2], BF16)
    return _matmul(y2.reshape(b * l, d), out_w.astype(BF16), bias=out_b, name="hyena_out")


def _lora_in(w):
    pad = lambda m: jnp.pad(m, ((0, 0), (0, LANES - m.shape[1])))
    return jnp.concatenate([pad(w[0]), pad(w[1])], axis=1)


def _lora_out(w):
    pad = lambda m: jnp.pad(m, ((0, LANES - m.shape[0]), (0, 0)))
    z = jnp.zeros_like(pad(w[0]))
    return jnp.concatenate([jnp.concatenate([pad(w[0]), z], axis=1),
                            jnp.concatenate([z, pad(w[1])], axis=1)], axis=0)


def _rwkv_mixer(x, ctx, mod3, layer, g0, mu, w_r, w_k, w_v, w_o, dec_w0, dec_w1, dec_w2, a0, a1, a2,
                g1, g2, k_k, k_a, r_k, lnx_w, lnx_b):
    b, l, d = x.shape
    n_ctx = ctx.shape[1]
    s = n_ctx + l
    xr, xw, xk, xv, xa, xg = [t.reshape(b * s, d) for t in _pre_rwkv(x, ctx, mod3, layer, g0, mu)]
    bf = lambda w: w.astype(BF16)
    r = _matmul(xr, bf(w_r), out_dtype=BF16, name="rwkv_r").reshape(b, s, d)
    k = _matmul(xk, bf(w_k), out_dtype=BF16, name="rwkv_k").reshape(b, s, d)
    v = _matmul(xv, bf(w_v), out_dtype=BF16, name="rwkv_v").reshape(b, s, d)
    assert dec_w1.shape[2] <= LANES and a1.shape[2] <= LANES
    lw1 = _matmul(xw, bf(_lora_in(dec_w1)), act="tanh", out_dtype=BF16, name="rwkv_dec1")
    lw = _matmul(lw1, bf(_lora_out(dec_w2)), name="rwkv_dec2").reshape(b, s, 2 * d)
    a1o = _matmul(xa, bf(_lora_in(a1)), out_dtype=BF16, name="rwkv_a1")
    a = _matmul(a1o, bf(_lora_out(a2)), name="rwkv_a2").reshape(b, s, 2 * d)
    g1o = _matmul(xg, bf(g1), act="sigmoid", out_dtype=BF16, name="rwkv_g1")
    g = _matmul(g1o, bf(g2), name="rwkv_g2").reshape(b, s, d)
    zz = _wkv(r, k, v, lw, a, g, dec_w0, a0, k_k, k_a, r_k, lnx_w, lnx_b, n_ctx)
    return _matmul(zz.reshape(b * l, d), bf(w_o), name="rwkv_o")


def kernel(x, c, ctx, c_ctx, ada_w, ada_b, norm_g, mlp_up, mlp_down, rw_mu, rw_w_r, rw_w_k, rw_w_v, rw_w_o, rw_dec_w0, rw_dec_w1, rw_dec_w2, rw_a0, rw_a1, rw_a2, rw_g1, rw_g2, rw_k_k, rw_k_a, rw_r_k, rw_lnx_w, rw_lnx_b, hy_in_w, hy_in_b, hy_conv_w, hy_conv_b, hy_f_w1, hy_f_w23, hy_f_w4, hy_f_b, hy_f_freq, hy_skip, hy_out_w, hy_out_b):
    b, l, d = x.shape
    depth = ada_w.shape[0]
    assert b < MOD_ROWS
    c_rows = jnp.zeros((MOD_ROWS, d), F32).at[:b].set(c).at[b].set(c_ctx)
    mod3 = _ada_mod(c_rows, ada_w, ada_b)
    x2 = x.reshape(b * l, d)
    xc = ctx
    for i in range(depth):
        kind, j = i % 2, i // 2
        ctx_live = any(q % 2 == 0 for q in range(i + 1, depth))
        assert not ctx_live, "context-stream update is not implemented for this depth"
        if kind == 0:
            y2 = _rwkv_mixer(x2.reshape(b, l, d), xc, mod3, i, norm_g[i, 0], rw_mu[j], rw_w_r[j], rw_w_k[j],
                             rw_w_v[j], rw_w_o[j], rw_dec_w0[j], rw_dec_w1[j], rw_dec_w2[j], rw_a0[j],
                             rw_a1[j], rw_a2[j], rw_g1[j], rw_g2[j], rw_k_k[j], rw_k_a[j], rw_r_k[j],
                             rw_lnx_w[j], rw_lnx_b[j])
        else:
            h = _pre_norm(x2, mod3, i, norm_g[i, 0], l)
            y2 = _hyena_mixer(h, b, l, hy_in_w[j], hy_in_b[j], hy_conv_w[j], hy_conv_b[j], hy_f_w1[j],
                              hy_f_w23[j], hy_f_w4[j], hy_f_b[j], hy_f_freq[j], hy_skip[j], hy_out_w[j],
                              hy_out_b[j])
        x2, h2 = _post_mixer(x2, y2, mod3, i, norm_g[i, 1], norm_g[i, 2], l)
        x2 = _mlp(h2, mlp_up[i].astype(BF16), mlp_down[i].astype(BF16), x2, mod3, i, norm_g[i, 3], l)
    return x2.reshape(b, l, d)
```

```python
import functools
import math

import jax
import jax.numpy as jnp
from jax import lax
from jax.experimental import pallas as pl
from jax.experimental.pallas import tpu as pltpu

F32 = jnp.float32
BF16 = jnp.bfloat16

HEAD_SIZE = 64
GRID_W = 64
N_MOD = 6
NORM_EPS = 1e-6
LNX_EPS = 64e-5
HY_FAST_DECAY = 0.3
HY_SLOW_DECAY = 1.5
HY_DECAY_TARGET = 1e-2
HY_EMB_DIM = 33
MOD_ROWS = 16
CHUNK = 64
LANES = 128
TOKEN_TILE = 256
VMEM_LIMIT_CAP = 60000 * 1024


def _cparams(sem, est_bytes):
    limit = int(min(max(2 * est_bytes, 32 * 1024 * 1024), VMEM_LIMIT_CAP))
    return pltpu.CompilerParams(dimension_semantics=sem, vmem_limit_bytes=limit)


def _largest_tile(n, cap, align):
    t = min(cap, n) // align * align
    while t > align and n % t:
        t -= align
    assert t > 0 and n % t == 0, (n, cap, align)
    return t


def _split3(x):
    hi = x.astype(BF16)
    r1 = x - hi.astype(F32)
    mid = r1.astype(BF16)
    lo = (r1 - mid.astype(F32)).astype(BF16)
    return hi, mid, lo


def _rms(x):
    return x * lax.rsqrt(jnp.mean(x * x, axis=-1, keepdims=True) + NORM_EPS)


def _norm_mod(x, g, shift, scale):
    return (_rms(x) * g) * (1.0 + scale) + shift


def _softplus(y):
    return jnp.maximum(y, 0.0) + jnp.log1p(jnp.exp(-jnp.abs(y)))


def _ada_kernel(c_ref, w_ref, b_ref, o_ref):
    c = c_ref[...]
    s = c * jax.nn.sigmoid(c)
    s_hi = s.astype(BF16)
    s_lo = (s - s_hi.astype(F32)).astype(BF16)
    w = w_ref[0]
    w_hi = w.astype(BF16)
    w_lo = (w - w_hi.astype(F32)).astype(BF16)
    p = jnp.dot(jnp.concatenate([s_hi, s_lo], axis=0), w_hi, preferred_element_type=F32)
    q = jnp.dot(s_hi, w_lo, preferred_element_type=F32)
    o_ref[0] = p[:MOD_ROWS] + p[MOD_ROWS:] + q + b_ref[0]


def _ada_mod(c_rows, ada_w, ada_b):
    depth, d, n = ada_w.shape
    tn = 1024
    out = pl.pallas_call(
        _ada_kernel,
        out_shape=jax.ShapeDtypeStruct((depth, MOD_ROWS, n), F32),
        grid=(depth, n // tn),
        in_specs=[pl.BlockSpec((MOD_ROWS, d), lambda l, j: (0, 0)),
                  pl.BlockSpec((1, d, tn), lambda l, j: (l, 0, j)),
                  pl.BlockSpec((1, 1, tn), lambda l, j: (l, 0, j))],
        out_specs=pl.BlockSpec((1, MOD_ROWS, tn), lambda l, j: (l, 0, j)),
        compiler_params=_cparams(("parallel", "parallel"), 2 * d * tn * 4 + 4 * d * tn),
        name="ada_mod",
    )(c_rows, ada_w, ada_b.reshape(depth, 1, n))
    return out.reshape(depth * MOD_ROWS * N_MOD, 1, d)


def _mod_row(layer, row, j):
    return (layer * MOD_ROWS + row) * N_MOD + j


def _pre_rwkv_kernel(hc_ref, sc_ref_ctx, xm_ref, xu_ref, xd_ref, g_ref, sh_ref, sc_ref, mu_ref,
                     *o_refs, n_tiles):
    t = pl.program_id(1)
    d = xm_ref.shape[-1]
    g = g_ref[...]
    mu = mu_ref[...]

    def emit(h, s, lo, hi):
        xx = s - h
        for j in range(6):
            o_refs[j][0, :, lo:hi] = (h + xx * mu[j:j + 1, lo:hi]).astype(BF16)

    @pl.when(t == 0)
    def _():
        emit(hc_ref[0], sc_ref_ctx[0], 0, d)

    @pl.when(t > 0)
    def _():
        sh = sh_ref[0]
        sc = sc_ref[0]
        xm = xm_ref[0].reshape(TOKEN_TILE, d)
        hm = _norm_mod(xm, g, sh, sc)
        hu = _norm_mod(xu_ref[0, 0], g, sh, sc) * jnp.where(t > 1, 1.0, 0.0)
        hd = _norm_mod(xd_ref[0, 0], g, sh, sc) * jnp.where(t < n_tiles, 1.0, 0.0)
        col = lax.broadcasted_iota(jnp.int32, (TOKEN_TILE, 1), 0) & (GRID_W - 1)
        q = d // 4
        left = jnp.where(col != 0, pltpu.roll(hm[:, :q], 1, 0), 0.0)
        right = jnp.where(col != GRID_W - 1, pltpu.roll(hm[:, q:2 * q], TOKEN_TILE - 1, 0), 0.0)
        up = jnp.concatenate([hu[:, 2 * q:3 * q], hm[:TOKEN_TILE - GRID_W, 2 * q:3 * q]], axis=0)
        down = jnp.concatenate([hm[GRID_W:, 3 * q:], hd[:, 3 * q:]], axis=0)
        emit(hm[:, :q], left, 0, q)
        emit(hm[:, q:2 * q], right, q, 2 * q)
        emit(hm[:, 2 * q:3 * q], up, 2 * q, 3 * q)
        emit(hm[:, 3 * q:], down, 3 * q, d)


def _shift_seq(x):
    half = x.shape[-1] // 2
    p = jnp.pad(x, ((0, 0), (1, 1), (0, 0)))
    return jnp.concatenate([p[:, :-2, :half], p[:, 2:, half:]], axis=-1)


def _pre_rwkv(x, ctx, mod3, layer, g0, mu):
    b, l, d = x.shape
    n_ctx = ctx.shape[1]
    assert n_ctx == TOKEN_TILE and l % TOKEN_TILE == 0
    n_tiles = l // TOKEN_TILE
    rows_per_tile = TOKEN_TILE // GRID_W
    n_rows = l // GRID_W
    x4 = x.reshape(b, n_rows, GRID_W, d)
    s = n_ctx + l
    hc = _pre_norm(ctx.reshape(b * n_ctx, d), mod3, layer, g0, n_ctx, tm=n_ctx, fixed_row=b,
                   out_dtype=F32).reshape(b, n_ctx, d)
    sc = _shift_seq(hc)

    def mrow(j):
        return pl.BlockSpec((1, 1, d), lambda bi, t: (_mod_row(layer, bi, j), 0, 0))

    main = lambda bi, t: (bi, jnp.maximum(t - 1, 0), 0, 0)
    up = lambda bi, t: (bi, jnp.maximum((t - 1) * rows_per_tile - 1, 0), 0, 0)
    down = lambda bi, t: (bi, jnp.minimum(jnp.maximum(t, 1) * rows_per_tile, n_rows - 1), 0, 0)
    out_sds = jax.ShapeDtypeStruct((b, s, d), BF16)
    outs = pl.pallas_call(
        functools.partial(_pre_rwkv_kernel, n_tiles=n_tiles),
        out_shape=[out_sds] * 6,
        grid=(b, n_tiles + 1),
        in_specs=[pl.BlockSpec((1, n_ctx, d), lambda bi, t: (bi, 0, 0)),
                  pl.BlockSpec((1, n_ctx, d), lambda bi, t: (bi, 0, 0)),
                  pl.BlockSpec((1, rows_per_tile, GRID_W, d), main),
                  pl.BlockSpec((1, 1, GRID_W, d), up),
                  pl.BlockSpec((1, 1, GRID_W, d), down),
                  pl.BlockSpec((1, d), lambda bi, t: (0, 0)),
                  mrow(0), mrow(1),
                  pl.BlockSpec((6, d), lambda bi, t: (0, 0))],
        out_specs=[pl.BlockSpec((1, TOKEN_TILE, d), lambda bi, t: (bi, t, 0))] * 6,
        compiler_params=_cparams(("parallel", "arbitrary"),
                                 2 * (3 * TOKEN_TILE * d * 4 + 2 * GRID_W * d * 4 + 6 * TOKEN_TILE * d * 2)),
        name="rwkv_pre",
    )(hc, sc, x4, x4, x4, g0.reshape(1, d), mod3, mod3, mu)
    return outs


def _mm_kernel(a_ref, w_ref, *rest, act, has_bias):
    o_ref = rest[-1]
    acc = jnp.dot(a_ref[...], w_ref[...], preferred_element_type=F32)
    if has_bias:
        acc = acc + rest[0][...]
    if act == "tanh":
        acc = jnp.tanh(acc)
    elif act == "sigmoid":
        acc = jax.nn.sigmoid(acc)
    o_ref[...] = acc.astype(o_ref.dtype)


def _matmul(a, w, bias=None, act=None, out_dtype=F32, tm=1024, tn=1024, name="matmul"):
    m, k = a.shape
    n = w.shape[1]
    tm = _largest_tile(m, tm, 8)
    tn = _largest_tile(n, tn, LANES)
    in_specs = [pl.BlockSpec((tm, k), lambda i, j: (i, 0)),
                pl.BlockSpec((k, tn), lambda i, j: (0, j))]
    args = [a, w]
    if bias is not None:
        in_specs.append(pl.BlockSpec((1, tn), lambda i, j: (0, j)))
        args.append(bias.reshape(1, n).astype(F32))
    est = 2 * (tm * k * 2 + k * tn * 2 + tm * tn * jnp.dtype(out_dtype).itemsize) + tm * tn * 4
    return pl.pallas_call(
        functools.partial(_mm_kernel, act=act, has_bias=bias is not None),
        out_shape=jax.ShapeDtypeStruct((m, n), out_dtype),
        grid=(m // tm, n // tn),
        in_specs=in_specs,
        out_specs=pl.BlockSpec((tm, tn), lambda i, j: (i, j)),
        compiler_params=_cparams(("parallel", "parallel"), est),
        name=name,
    )(*args)


def _seg_sum(x, ones_bd):
    hi, mid, lo = _split3(x)
    r = x.shape[0]
    p = jnp.dot(jnp.concatenate([hi, mid, lo], axis=0), ones_bd, preferred_element_type=F32)
    return p[:r] + p[r:2 * r] + p[2 * r:]


def _bd(x, head0):
    return jnp.concatenate([jnp.where(head0, x, 0.0), jnp.where(head0, 0.0, x)], axis=0)


def _wkv_constants(rev):
    c = CHUNK
    shift = int(math.log2(c))
    head0 = lax.broadcasted_iota(jnp.int32, (1, LANES), 1) < HEAD_SIZE
    row2 = lax.broadcasted_iota(jnp.int32, (2 * c, 2 * c), 0)
    col2 = lax.broadcasted_iota(jnp.int32, (2 * c, 2 * c), 1)
    same = (row2 >> shift) == (col2 >> shift)
    tt = row2 & (c - 1)
    ss = col2 & (c - 1)
    if rev:
        tt, ss = ss, tt
    ones_bd = jnp.where(same, 1.0, 0.0).astype(BF16)
    eye = jnp.where(row2 == col2, 1.0, 0.0).astype(F32)
    rowc = lax.broadcasted_iota(jnp.int32, (c, c), 0)
    colc = lax.broadcasted_iota(jnp.int32, (c, c), 1)
    tri = jnp.where(colc >= rowc if rev else colc <= rowc, 1.0, 0.0).astype(BF16)
    merge = tuple(((tt >> lv) == (ss >> lv) + 1) & ((tt >> (lv + 1)) == (ss >> (lv + 1))) for lv in range(shift))
    return head0, ones_bd, tri, ss < tt, ss <= tt, eye, merge


def _each(fn, *lists):
    return [fn(*args) for args in zip(*lists)]


def _mm(a, b):
    return jnp.dot(a.astype(BF16), b.astype(BF16), preferred_element_type=F32)


def _wkv_prepare(probs, k_a, hooks=()):
    hooks = list(hooks)

    def run_hook():
        if hooks:
            hooks.pop(0)()

    revs, rs, ks, kks, bd_vs, lwxs, axs, w0s, a0s, csts = [list(t) for t in zip(*probs)]
    head0 = csts[0][0]
    eye = csts[0][5]
    c = rs[0].shape[0]
    c2 = 2 * c
    logw = _each(lambda w0, lwx: -jnp.exp(-_softplus(-(w0 + lwx)) - 0.5), w0s, lwxs)
    a = _each(lambda a0, ax: jax.nn.sigmoid(a0 + ax), a0s, axs)
    kd = _each(lambda k, a_: k * (1.0 + (a_ - 1.0) * k_a), ks, a)
    b = _each(lambda kk, a_: kk * a_, kks, a)

    def cumsum(lw, cst):
        hi, mid, lo = _split3(lw)
        c3 = jnp.dot(cst[2], jnp.concatenate([hi, mid, lo], axis=1), preferred_element_type=F32)
        return c3[:, :LANES] + c3[:, LANES:2 * LANES] + c3[:, 2 * LANES:]

    cum = _each(cumsum, logw, csts)
    total = _each(lambda cm, rev: cm[0:1] if rev else cm[c - 1:c], cum, revs)
    kap_t = _each(lambda kk, cm, lw: kk * jnp.exp(cm - lw), kks, cum, logw)
    r_t = _each(lambda r, cm: r * jnp.exp(cm), rs, cum)
    igam = _each(lambda cm: jnp.exp(-cm), cum)
    tail = _each(lambda t, cm: jnp.exp(t - cm), total, cum)

    bd_kap = _each(lambda x: _bd(x, head0), kap_t)
    bd_r = _each(lambda x: _bd(x, head0), r_t)
    lhs = _each(lambda p, q: jnp.concatenate([p, q], axis=0).astype(BF16), bd_kap, bd_r)
    rhs = _each(lambda kd_, b_, ig: jnp.concatenate([_bd(kd_ * ig, head0), _bd(b_ * ig, head0)],
                                                    axis=0).astype(BF16), kd, b, igam)
    gmat = _each(lambda l_, r_: lax.dot_general(l_, r_, (((1,), (1,)), ((), ())),
                                                preferred_element_type=F32), lhs, rhs)
    a_kb = _each(lambda g, cst: jnp.where(cst[3], g[:c2, c2:], 0.0), gmat, csts)
    run_hook()

    tinv = _each(lambda akb, cst: eye - jnp.where(cst[6][0], akb, 0.0), a_kb, csts)
    for lv in range(1, len(csts[0][6])):
        y = _each(lambda akb, x, cst: _mm(jnp.where(cst[6][lv], akb, 0.0), x), a_kb, tinv, csts)
        tinv = _each(lambda x, y_: x - _mm(x, y_), tinv, y)
        if lv % 2 == 0:
            run_hook()

    a_kr = _each(lambda g, cst: jnp.concatenate([jnp.where(cst[3], g[:c2, :c2], 0.0),
                                                 jnp.where(cst[4], g[c2:, :c2], 0.0)], axis=0), gmat, csts)
    av = _each(_mm, a_kr, bd_vs)
    while hooks:
        run_hook()
    wu = _each(lambda x, kap, av_: _mm(x, jnp.concatenate([kap, av_[:c2]], axis=1)), tinv, bd_kap, av)
    rb = _each(lambda g, cst, wu_: _mm(jnp.where(cst[4], g[c2:, c2:], 0.0), wu_), gmat, csts, wu)
    bct = _each(lambda b_, tl: _bd(b_ * tl, head0).T, b, tail)
    kct = _each(lambda kd_, tl: _bd(kd_ * tl, head0).T, kd, tail)
    mn = _each(_mm, bct, wu)
    kv = _each(_mm, kct, bd_vs)
    out = []
    for i in range(len(probs)):
        rk = bd_r[i] - rb[i][:, :LANES]
        ov = av[i][c2:] - rb[i][:, LANES:]
        m_mat = jnp.where(eye > 0.0, jnp.exp(total[i]), 0.0) - mn[i][:, :LANES]
        n_mat = kv[i] - mn[i][:, LANES:]
        out.append((jnp.concatenate([rk, m_mat], axis=0).astype(BF16), ov[:c] + ov[c:], n_mat))
    return out


def _wkv_apply(rm_lhs, ov, n_mat, h_state):
    c = ov.shape[0]
    rm = jnp.dot(rm_lhs, h_state.astype(BF16), preferred_element_type=F32)
    return rm[:c] + rm[c:2 * c] + ov, rm[2 * c:] + n_mat


def _wkv_kernel(r_ref, k_ref, v_ref, lw1_ref, a1_ref, g_ref, wdec_ref, wa_ref,
                w0_ref, a0_ref, kk_ref, ka_ref, rk_ref, lnw_ref, lnb_ref,
                z_ref, of_scr, ob_scr, rm_scr, ov_scr, n_scr, *, n_ctx_chunks, n_chunks, unroll):
    c = CHUNK
    cst_f = _wkv_constants(False)
    cst_b = _wkv_constants(True)
    head0, ones_bd = cst_f[0], cst_f[1]

    w0f, w0b = w0_ref[0:1, :], w0_ref[1:2, :]
    a0f, a0b = a0_ref[0:1, :], a0_ref[1:2, :]
    k_k = kk_ref[...]
    k_a = ka_ref[...]

    def load(ref, rows):
        return ref[0, rows, :].astype(F32)

    def lora(x_ref, w_ref, rows, rev):
        d0 = int(rev) * LANES
        return jnp.dot(x_ref[0, rows, d0:d0 + LANES], w_ref[int(rev)], preferred_element_type=F32)

    def chunk_of(i, rev):
        if not rev:
            return i
        return jnp.where(i < n_ctx_chunks, n_ctx_chunks - 1 - i, n_chunks + n_ctx_chunks - 1 - i)

    def prepare_group(grp, hooks=()):
        probs, slots = [], []
        for u in range(unroll):
            for rev, w0, a0, cst in ((False, w0f, a0f, cst_f), (True, w0b, a0b, cst_b)):
                ci = chunk_of(grp * unroll + u, rev)
                rows = pl.ds(pl.multiple_of(ci * c, c), c)
                k = load(k_ref, rows)
                kk0 = k * k_k
                kk = kk0 * lax.rsqrt(jnp.maximum(_seg_sum(kk0 * kk0, ones_bd), 1e-24))
                bd_v = _bd(load(v_ref, rows), head0).astype(BF16)
                probs.append((rev, load(r_ref, rows), k, kk, bd_v, lora(lw1_ref, wdec_ref, rows, rev),
                              lora(a1_ref, wa_ref, rows, rev), w0, a0, cst))
                slots.append(ci + n_chunks * int(rev))
        for slot, (rm_lhs, ov, n_mat) in zip(slots, _wkv_prepare(probs, k_a, hooks)):
            rm_scr[slot] = rm_lhs
            ov_scr[slot] = ov
            n_scr[slot] = n_mat

    def state_steps(grp, state):
        def step(i):
            for rev, o_scr in ((False, of_scr), (True, ob_scr)):
                ci = chunk_of(i, rev)
                slot = ci + n_chunks * int(rev)
                o, state[int(rev)] = _wkv_apply(rm_scr[slot], ov_scr[slot], n_scr[slot], state[int(rev)])
                o_scr[pl.ds(pl.multiple_of(ci * c, c), c), :] = o
        return [functools.partial(step, grp * unroll + u) for u in range(unroll)]

    n_groups = n_chunks // unroll
    prepare_group(0)

    def body(grp, carry):
        state = list(carry)
        prepare_group(grp, state_steps(grp - 1, state))
        return tuple(state)

    zero = jnp.zeros((2 * c, LANES), F32)
    state = list(lax.fori_loop(1, n_groups, body, (zero, zero)))
    for step in state_steps(n_groups - 1, state):
        step()

    r_k = rk_ref[...]
    lnw = lnw_ref[...]
    lnb = lnb_ref[...]
    blk = TOKEN_TILE
    n_ctx = n_ctx_chunks * c
    inv_n = 1.0 / HEAD_SIZE

    def read_body(j, _):
        rows = pl.ds(pl.multiple_of(n_ctx + j * blk, blk), blk)
        o = of_scr[rows, :] + ob_scr[rows, :]
        mean = _seg_sum(o, ones_bd) * inv_n
        dev = o - mean
        var = _seg_sum(dev * dev, ones_bd) * inv_n
        on = dev * lax.rsqrt(var + LNX_EPS) * lnw + lnb
        r = load(r_ref, rows)
        k = load(k_ref, rows)
        v = load(v_ref, rows)
        a_sum = (jax.nn.sigmoid(a0f + lora(a1_ref, wa_ref, rows, False))
                 + jax.nn.sigmoid(a0b + lora(a1_ref, wa_ref, rows, True)))
        kd_sum = k * (2.0 + (a_sum - 2.0) * k_a)
        bonus = _seg_sum(r * kd_sum * r_k, ones_bd) * v
        z = (on + bonus) * g_ref[0, rows, :]
        z_ref[0, pl.ds(pl.multiple_of(j * blk, blk), blk), :] = z.astype(z_ref.dtype)
        return 0

    lax.fori_loop(0, (n_chunks - n_ctx_chunks) * c // blk, read_body, 0)


def _wkv(r, k, v, lw1, a1, g, w_dec, w_a, dec_w0, a0, k_k, k_a, r_k, lnx_w, lnx_b, n_ctx):
    b, s, d = r.shape
    l = s - n_ctx
    npair = d // LANES
    seq = lambda bi, p: (bi, 0, p)
    vec = lambda bi, p: (0, p)
    sblk = pl.BlockSpec((1, s, LANES), seq)
    rank_blk = pl.BlockSpec((1, s, 2 * LANES), lambda bi, p: (bi, 0, 0))
    up_blk = pl.BlockSpec((2, LANES, LANES), lambda bi, p: (0, 0, p))
    n_chunks = s // CHUNK
    unroll = next(u for u in (6, 4, 2, 1) if n_chunks % u == 0)
    scratch = [pltpu.VMEM((s, LANES), F32), pltpu.VMEM((s, LANES), F32),
               pltpu.VMEM((2 * n_chunks, 4 * CHUNK, LANES), BF16),
               pltpu.VMEM((2 * n_chunks, CHUNK, LANES), F32),
               pltpu.VMEM((2 * n_chunks, 2 * CHUNK, LANES), F32)]
    est = (2 * (3 * s * LANES * 2 + 2 * s * 2 * LANES * 2 + s * LANES * 4 + l * LANES * 2) + 2 * s * LANES * 4
           + 2 * n_chunks * CHUNK * LANES * (4 * 2 + 4 + 2 * 4))
    return pl.pallas_call(
        functools.partial(_wkv_kernel, n_ctx_chunks=n_ctx // CHUNK, n_chunks=n_chunks, unroll=unroll),
        out_shape=jax.ShapeDtypeStruct((b, l, d), BF16),
        grid=(b, npair),
        in_specs=[sblk, sblk, sblk, rank_blk, rank_blk, sblk, up_blk, up_blk,
                  pl.BlockSpec((2, LANES), vec), pl.BlockSpec((2, LANES), vec),
                  pl.BlockSpec((1, LANES), vec), pl.BlockSpec((1, LANES), vec), pl.BlockSpec((1, LANES), vec),
                  pl.BlockSpec((1, LANES), vec), pl.BlockSpec((1, LANES), vec)],
        out_specs=pl.BlockSpec((1, l, LANES), seq),
        scratch_shapes=scratch,
        compiler_params=_cparams(("parallel", "parallel"), est),
        name="wkv_scan",
    )(r, k, v, lw1, a1, g, w_dec, w_a, dec_w0, a0, k_k.reshape(1, d), k_a.reshape(1, d), r_k.reshape(1, d),
      lnx_w.reshape(1, d), lnx_b.reshape(1, d))


def _proj_post_kernel(a_ref, w_ref, b_ref, x_ref, g1_ref, g2_ref, gate_ref, sh_ref, sc_ref, xo_ref, h_ref):
    y = jnp.dot(a_ref[...], w_ref[...], preferred_element_type=F32) + b_ref[...]
    x = x_ref[...] + gate_ref[0] * (_rms(y) * g1_ref[...])
    xo_ref[...] = x
    h_ref[...] = _norm_mod(x, g2_ref[...], sh_ref[0], sc_ref[0]).astype(h_ref.dtype)


def _proj_post(a, w, bias, x2, mod3, layer, g1, g2, l, tm=512, name="proj_post"):
    m, k = a.shape
    d = w.shape[1]
    per_b = l // tm
    row = lambda j: pl.BlockSpec((1, 1, d), lambda i: (_mod_row(layer, i // per_b, j), 0, 0))
    tile = pl.BlockSpec((tm, d), lambda i: (i, 0))
    vec = pl.BlockSpec((1, d), lambda i: (0, 0))
    est = k * d * 2 + 2 * (tm * k * 2 + tm * d * (4 + 4 + 2)) + 2 * tm * d * 4
    return pl.pallas_call(
        _proj_post_kernel,
        out_shape=[jax.ShapeDtypeStruct((m, d), F32), jax.ShapeDtypeStruct((m, d), BF16)],
        grid=(m // tm,),
        in_specs=[pl.BlockSpec((tm, k), lambda i: (i, 0)),
                  pl.BlockSpec((k, d), lambda i: (0, 0), pipeline_mode=pl.Buffered(1)),
                  vec, tile, vec, vec, row(2), row(3), row(4)],
        out_specs=[tile, tile],
        compiler_params=_cparams(("parallel",), est),
        name=name,
    )(a, w, bias.reshape(1, d).astype(F32), x2, g1.reshape(1, d), g2.reshape(1, d), mod3, mod3, mod3)


def _mlp_kernel(h_ref, wu_ref, wd_ref, x_ref, g_ref, gate_ref, o_ref, acc_ref):
    kf = pl.program_id(1)

    @pl.when(kf == 0)
    def _():
        acc_ref[...] = jnp.zeros_like(acc_ref)

    u = jnp.dot(h_ref[...], wu_ref[...], preferred_element_type=F32)
    u = jnp.square(jnp.maximum(u, 0.0)).astype(BF16)
    acc_ref[...] += jnp.dot(u, wd_ref[...], preferred_element_type=F32)

    @pl.when(kf == pl.num_programs(1) - 1)
    def _():
        o_ref[...] = x_ref[...] + gate_ref[0] * (_rms(acc_ref[...]) * g_ref[...])


def _mlp(h2, w_up, w_down, x2, mod3, layer, g3, l, tm=512, tf=1024):
    m, d = h2.shape
    dff = w_up.shape[1]
    per_b = l // tm
    est = 2 * (tm * d * 2 + 2 * d * tf * 2 + 2 * tm * d * 4) + tm * d * 4 + tm * tf * 6
    return pl.pallas_call(
        _mlp_kernel,
        out_shape=jax.ShapeDtypeStruct((m, d), F32),
        grid=(m // tm, dff // tf),
        in_specs=[pl.BlockSpec((tm, d), lambda i, f: (i, 0)),
                  pl.BlockSpec((d, tf), lambda i, f: (0, f)),
                  pl.BlockSpec((tf, d), lambda i, f: (f, 0)),
                  pl.BlockSpec((tm, d), lambda i, f: (i, 0)),
                  pl.BlockSpec((1, d), lambda i, f: (0, 0)),
                  pl.BlockSpec((1, 1, d), lambda i, f: (_mod_row(layer, i // per_b, 5), 0, 0))],
        out_specs=pl.BlockSpec((tm, d), lambda i, f: (i, 0)),
        scratch_shapes=[pltpu.VMEM((tm, d), F32)],
        compiler_params=_cparams(("parallel", "arbitrary"), est),
        name="mlp",
    )(h2, w_up, w_down, x2, g3.reshape(1, d), mod3)


def _pre_norm_kernel(x_ref, g_ref, sh_ref, sc_ref, h_ref):
    h_ref[...] = _norm_mod(x_ref[...], g_ref[...], sh_ref[0], sc_ref[0]).astype(h_ref.dtype)


def _pre_norm(x2, mod3, layer, g0, l, tm=512, fixed_row=None, out_dtype=BF16):
    m, d = x2.shape
    per_b = l // tm
    if fixed_row is None:
        row = lambda j: pl.BlockSpec((1, 1, d), lambda i: (_mod_row(layer, i // per_b, j), 0, 0))
    else:
        row = lambda j: pl.BlockSpec((1, 1, d), lambda i: (_mod_row(layer, fixed_row, j), 0, 0))
    tile = pl.BlockSpec((tm, d), lambda i: (i, 0))
    return pl.pallas_call(
        _pre_norm_kernel,
        out_shape=jax.ShapeDtypeStruct((m, d), out_dtype),
        grid=(m // tm,),
        in_specs=[tile, pl.BlockSpec((1, d), lambda i: (0, 0)), row(0), row(1)],
        out_specs=tile,
        compiler_params=_cparams(("parallel",), 2 * tm * d * 6),
        name="pre_norm",
    )(x2, g0.reshape(1, d), mod3, mod3)


def _filter_kernel(z_ref, w1_ref, w2_ref, w3_ref, b_ref, fr_ref, w4_ref, t_ref, dl_ref, o_ref, hid_ref):
    hp = lax.Precision.HIGHEST

    @pl.when(pl.program_id(0) == 0)
    def _():
        b = b_ref[...]
        fr = fr_ref[...]
        z = jnp.sin(fr[0:1] * (jnp.dot(z_ref[...], w1_ref[...], precision=hp, preferred_element_type=F32) + b[0:1]))
        z = jnp.sin(fr[1:2] * (jnp.dot(z, w2_ref[...], precision=hp, preferred_element_type=F32) + b[1:2]))
        hid_ref[...] = jnp.sin(fr[2:3] * (jnp.dot(z, w3_ref[...], precision=hp, preferred_element_type=F32)
                                          + b[2:3]))

    filt = jnp.dot(hid_ref[...], w4_ref[...], precision=hp, preferred_element_type=F32)
    o_ref[...] = filt * jnp.exp(-t_ref[...] * dl_ref[...])


def _hyena_filters(l, d, f_w1, f_w23, f_w4, f_b, f_freq):
    t = jnp.linspace(0.0, 1.0, l, dtype=F32)[:, None]
    bands = (HY_EMB_DIM - 1) // 2
    freqs = jnp.linspace(1e-4, bands - 1, bands, dtype=F32)[None, :]
    ang = (2.0 * math.pi / l) * jnp.arange(l, dtype=F32)[:, None] * freqs
    z = jnp.concatenate([t, jnp.cos(ang), -jnp.sin(ang)], axis=-1)
    e, f = f_w1.shape
    pad = lambda a_, r, c: jnp.pad(a_.astype(F32), ((0, r - a_.shape[0]), (0, c - a_.shape[1])))
    zp = pad(z, l, LANES)
    w1 = pad(f_w1, LANES, LANES)
    w2 = pad(f_w23[0], LANES, LANES)
    w3 = pad(f_w23[1], LANES, LANES)
    bb = pad(f_b, 8, LANES)
    fr = pad(f_freq, 8, LANES)
    n = f_w4.shape[1]
    w4 = pad(f_w4, LANES, n)
    max_decay = math.log(HY_DECAY_TARGET) / HY_FAST_DECAY
    min_decay = math.log(HY_DECAY_TARGET) / HY_SLOW_DECAY
    deltas = jnp.abs(jnp.linspace(min_decay, max_decay, d, dtype=F32))[None, :]
    tn = _largest_tile(d, 1024, LANES)
    per_d = d // tn
    sq = pl.BlockSpec((LANES, LANES), lambda j: (0, 0))
    small = pl.BlockSpec((8, LANES), lambda j: (0, 0))
    return pl.pallas_call(
        _filter_kernel,
        out_shape=jax.ShapeDtypeStruct((l, n), F32),
        grid=(n // tn,),
        in_specs=[pl.BlockSpec((l, LANES), lambda j: (0, 0)), sq, sq, sq, small, small,
                  pl.BlockSpec((LANES, tn), lambda j: (0, j)),
                  pl.BlockSpec((l, 1), lambda j: (0, 0)),
                  pl.BlockSpec((1, tn), lambda j: (0, j % per_d))],
        out_specs=pl.BlockSpec((l, tn), lambda j: (0, j)),
        scratch_shapes=[pltpu.VMEM((l, LANES), F32)],
        compiler_params=_cparams(("arbitrary",), 4 * l * tn * 4),
        name="hyena_filters",
    )(zp, w1, w2, w3, bb, fr, w4, t, deltas)


def _dft_matrices(l):
    n = 2 * l
    k = jnp.arange(l, dtype=jnp.int32)[:, None]
    t = jnp.arange(l, dtype=jnp.int32)[None, :]
    ang = ((k * t) % n).astype(F32) * (2.0 * math.pi / n)
    cos = jnp.cos(ang)
    msin = jnp.where(k == 0, jnp.where(t % 2 == 0, 1.0, -1.0), -jnp.sin(ang))
    fwd = jnp.concatenate([cos, msin], axis=0)
    return fwd.astype(BF16), fwd.T.astype(BF16)


def _short_conv(z, cw, cb):
    n = z.shape[0]
    row = lax.broadcasted_iota(jnp.int32, (n, 1), 0)
    prev = jnp.where(row != 0, pltpu.roll(z, 1, 0), 0.0)
    nxt = jnp.where(row != n - 1, pltpu.roll(z, n - 1, 0), 0.0)
    return prev * cw[0:1] + z * cw[1:2] + nxt * cw[2:3] + cb


def _spec_kernel(f_ref, hf_ref, hb_ref, o_ref):
    l, tn = hf_ref.shape
    row = lax.broadcasted_iota(jnp.int32, (l, 1), 0)
    hb = jnp.where(row == 0, 0.0, hb_ref[...])
    h2 = jnp.concatenate([hf_ref[...], hb], axis=1).astype(BF16)
    u = jnp.dot(f_ref[...], h2, preferred_element_type=F32)
    row2 = lax.broadcasted_iota(jnp.int32, (2 * l, 1), 0)
    sign = jnp.where(row2 > l, -1.0, 1.0)
    scale = jnp.where((row2 == 0) | (row2 == l), 1.0 / (2 * l), 2.0 / (2 * l))
    o_ref[0] = (u[:, :tn] + sign * u[:, tn:]) * scale


def _spectrum(fwd, filt, d, tn=256):
    l = filt.shape[0]
    n2 = fwd.shape[0]
    orders = filt.shape[1] // (2 * d)
    per = d // tn
    return pl.pallas_call(
        _spec_kernel,
        out_shape=jax.ShapeDtypeStruct((orders, n2, d), F32),
        grid=(orders, per),
        in_specs=[pl.BlockSpec((n2, l), lambda o, j: (0, 0), pipeline_mode=pl.Buffered(1)),
                  pl.BlockSpec((l, tn), lambda o, j: (0, o * per + j)),
                  pl.BlockSpec((l, tn), lambda o, j: (0, (orders + o) * per + j))],
        out_specs=pl.BlockSpec((1, n2, tn), lambda o, j: (o, 0, j)),
        compiler_params=_cparams(("parallel", "parallel"), n2 * l * 2 + 2 * (2 * l * tn * 4 + n2 * tn * 4)
                                 + 3 * n2 * tn * 4),
        name="filter_spectrum",
    )(fwd, filt, filt)


def _conv_fwd_kernel(f_ref, y_ref, k_ref, cw_ref, cb_ref, p_ref, *, short_conv, n_split):
    l = y_ref.shape[1]
    y = y_ref[0]
    if short_conv:
        y = _short_conv(y, cw_ref[...], cb_ref[...])
    yb = y.astype(BF16)
    rows = l // n_split
    for s in range(n_split):
        lo, hi = s * rows, (s + 1) * rows
        ure = jnp.dot(f_ref[lo:hi, :], yb, preferred_element_type=F32)
        uim = jnp.dot(f_ref[l + lo:l + hi, :], yb, preferred_element_type=F32)
        kre = k_ref[0, lo:hi, :]
        kim = k_ref[0, l + lo:l + hi, :]
        pre = ure * kre - uim * kim
        pim = ure * kim + uim * kre
        if s == 0:
            first = lax.broadcasted_iota(jnp.int32, (rows, 1), 0) == 0
            pre = jnp.where(first, ure * kre, pre)
            pim = jnp.where(first, uim * kim, pim)
        p_ref[0, lo:hi, :] = pre.astype(p_ref.dtype)
        p_ref[0, l + lo:l + hi, :] = pim.astype(p_ref.dtype)


def _conv_fwd(fwd, y, y_col0, kspec, order, conv_w, conv_b, short_conv, d, tn=256):
    b, l, _ = y.shape
    n2 = fwd.shape[0]
    off = y_col0 // tn
    est = n2 * l * 2 + 2 * (l * tn * 4 + n2 * tn * 4 + n2 * tn * 2) + 6 * l * tn * 4
    return pl.pallas_call(
        functools.partial(_conv_fwd_kernel, short_conv=short_conv, n_split=2),
        out_shape=jax.ShapeDtypeStruct((b, n2, d), BF16),
        grid=(d // tn, b),
        in_specs=[pl.BlockSpec((n2, l), lambda j, bi: (0, 0), pipeline_mode=pl.Buffered(1)),
                  pl.BlockSpec((1, l, tn), lambda j, bi: (bi, 0, j + off)),
                  pl.BlockSpec((1, n2, tn), lambda j, bi: (order, 0, j)),
                  pl.BlockSpec((3, tn), lambda j, bi: (0, j + off)),
                  pl.BlockSpec((1, tn), lambda j, bi: (0, j + off))],
        out_specs=pl.BlockSpec((1, n2, tn), lambda j, bi: (bi, 0, j)),
        compiler_params=_cparams(("parallel", "parallel"), est),
        name="hyena_conv_fwd",
    )(fwd, y, kspec, conv_w, conv_b)


def _conv_inv_kernel(ft_ref, p_ref, yp_ref, gt_ref, cwy_ref, cby_ref, cwg_ref, cbg_ref, sk_ref, o_ref,
                     *, short_conv_prev):
    conv = jnp.dot(ft_ref[...], p_ref[0], preferred_element_type=F32)
    yp = yp_ref[0]
    if short_conv_prev:
        yp = _short_conv(yp, cwy_ref[...], cby_ref[...])
    gate = _short_conv(gt_ref[0], cwg_ref[...], cbg_ref[...])
    o_ref[0] = (gate * (conv + sk_ref[...] * yp)).astype(o_ref.dtype)


def _conv_inv(finv, p, yprev, yprev_col0, short_conv_prev, z, gate_col0, conv_w, conv_b, skip, out_dtype, tn=256):
    b, n2, d = p.shape
    l = n2 // 2
    offy = yprev_col0 // tn
    offg = gate_col0 // tn
    est = l * n2 * 2 + 2 * (n2 * tn * 2 + 3 * l * tn * 4) + 6 * l * tn * 4
    return pl.pallas_call(
        functools.partial(_conv_inv_kernel, short_conv_prev=short_conv_prev),
        out_shape=jax.ShapeDtypeStruct((b, l, d), out_dtype),
        grid=(d // tn, b),
        in_specs=[pl.BlockSpec((l, n2), lambda j, bi: (0, 0), pipeline_mode=pl.Buffered(1)),
                  pl.BlockSpec((1, n2, tn), lambda j, bi: (bi, 0, j)),
                  pl.BlockSpec((1, l, tn), lambda j, bi: (bi, 0, j + offy)),
                  pl.BlockSpec((1, l, tn), lambda j, bi: (bi, 0, j + offg)),
                  pl.BlockSpec((3, tn), lambda j, bi: (0, j + offy)),
                  pl.BlockSpec((1, tn), lambda j, bi: (0, j + offy)),
                  pl.BlockSpec((3, tn), lambda j, bi: (0, j + offg)),
                  pl.BlockSpec((1, tn), lambda j, bi: (0, j + offg)),
                  pl.BlockSpec((1, tn), lambda j, bi: (0, j))],
        out_specs=pl.BlockSpec((1, l, tn), lambda j, bi: (bi, 0, j)),
        compiler_params=_cparams(("parallel", "parallel"), est),
        name="hyena_conv_inv",
    )(finv, p, yprev, z, conv_w, conv_b, conv_w, conv_b, skip)


def _hyena_mixer(h2, b, l, in_w, in_b, conv_w, conv_b, f_w1, f_w23, f_w4, f_b, f_freq, skip, out_w, out_b):
    d = h2.shape[1]
    z = _matmul(h2, in_w.astype(BF16), bias=in_b, name="hyena_in").reshape(b, l, 3 * d)
    filt = _hyena_filters(l, d, f_w1, f_w23, f_w4, f_b, f_freq)
    fwd, finv = _dft_matrices(l)
    kspec = _spectrum(fwd, filt, d)
    skip = skip.astype(F32)
    p0 = _conv_fwd(fwd, z, 0, kspec, 0, conv_w, conv_b.reshape(1, -1), True, d)
    y1 = _conv_inv(finv, p0, z, 0, True, z, d, conv_w, conv_b.reshape(1, -1), skip[0:1], F32)
    p1 = _conv_fwd(fwd, y1, 0, kspec, 1, conv_w, conv_b.reshape(1, -1), False, d)
    y2 = _conv_inv(finv, p1, y1, 0, False, z, 2 * d, conv_w, conv_b.reshape(1, -1), skip[1:2], BF16)
    return y2.reshape(b * l, d), out_w.astype(BF16), out_b


def _lora_in(w):
    pad = lambda m: jnp.pad(m, ((0, 0), (0, LANES - m.shape[1])))
    return jnp.concatenate([pad(w[0]), pad(w[1])], axis=1)


def _lora_out(w):
    return jnp.pad(w, ((0, 0), (0, LANES - w.shape[1]), (0, 0)))


def _rwkv_mixer(x, ctx, mod3, layer, g0, mu, w_r, w_k, w_v, w_o, dec_w0, dec_w1, dec_w2, a0, a1, a2,
                g1, g2, k_k, k_a, r_k, lnx_w, lnx_b):
    b, l, d = x.shape
    n_ctx = ctx.shape[1]
    s = n_ctx + l
    xr, xw, xk, xv, xa, xg = [t.reshape(b * s, d) for t in _pre_rwkv(x, ctx, mod3, layer, g0, mu)]
    bf = lambda w: w.astype(BF16)
    r = _matmul(xr, bf(w_r), out_dtype=BF16, name="rwkv_r").reshape(b, s, d)
    k = _matmul(xk, bf(w_k), out_dtype=BF16, name="rwkv_k").reshape(b, s, d)
    v = _matmul(xv, bf(w_v), out_dtype=BF16, name="rwkv_v").reshape(b, s, d)
    assert dec_w1.shape[2] <= LANES and a1.shape[2] <= LANES
    lw1 = _matmul(xw, bf(_lora_in(dec_w1)), act="tanh", out_dtype=BF16, name="rwkv_dec1").reshape(b, s, 2 * LANES)
    a1o = _matmul(xa, bf(_lora_in(a1)), out_dtype=BF16, name="rwkv_a1").reshape(b, s, 2 * LANES)
    g1o = _matmul(xg, bf(g1), act="sigmoid", out_dtype=BF16, name="rwkv_g1")
    g = _matmul(g1o, bf(g2), name="rwkv_g2").reshape(b, s, d)
    zz = _wkv(r, k, v, lw1, a1o, g, bf(_lora_out(dec_w2)), bf(_lora_out(a2)), dec_w0, a0, k_k, k_a, r_k,
              lnx_w, lnx_b, n_ctx)
    return zz.reshape(b * l, d), bf(w_o), jnp.zeros((d,), F32)


def kernel(x, c, ctx, c_ctx, ada_w, ada_b, norm_g, mlp_up, mlp_down, rw_mu, rw_w_r, rw_w_k, rw_w_v, rw_w_o, rw_dec_w0, rw_dec_w1, rw_dec_w2, rw_a0, rw_a1, rw_a2, rw_g1, rw_g2, rw_k_k, rw_k_a, rw_r_k, rw_lnx_w, rw_lnx_b, hy_in_w, hy_in_b, hy_conv_w, hy_conv_b, hy_f_w1, hy_f_w23, hy_f_w4, hy_f_b, hy_f_freq, hy_skip, hy_out_w, hy_out_b):
    b, l, d = x.shape
    depth = ada_w.shape[0]
    assert b < MOD_ROWS
    c_rows = jnp.zeros((MOD_ROWS, d), F32).at[:b].set(c).at[b].set(c_ctx)
    mod3 = _ada_mod(c_rows, ada_w, ada_b)
    x2 = x.reshape(b * l, d)
    xc = ctx
    for i in range(depth):
        kind, j = i % 2, i // 2
        ctx_live = any(q % 2 == 0 for q in range(i + 1, depth))
        assert not ctx_live, "context-stream update is not implemented for this depth"
        if kind == 0:
            mixed = _rwkv_mixer(x2.reshape(b, l, d), xc, mod3, i, norm_g[i, 0], rw_mu[j], rw_w_r[j], rw_w_k[j],
                                rw_w_v[j], rw_w_o[j], rw_dec_w0[j], rw_dec_w1[j], rw_dec_w2[j], rw_a0[j],
                                rw_a1[j], rw_a2[j], rw_g1[j], rw_g2[j], rw_k_k[j], rw_k_a[j], rw_r_k[j],
                                rw_lnx_w[j], rw_lnx_b[j])
        else:
            h = _pre_norm(x2, mod3, i, norm_g[i, 0], l)
            mixed = _hyena_mixer(h, b, l, hy_in_w[j], hy_in_b[j], hy_conv_w[j], hy_conv_b[j], hy_f_w1[j],
                                 hy_f_w23[j], hy_f_w4[j], hy_f_b[j], hy_f_freq[j], hy_skip[j], hy_out_w[j],
                                 hy_out_b[j])
        x2, h2 = _proj_post(*mixed, x2, mod3, i, norm_g[i, 1], norm_g[i, 2], l,
                            name="rwkv_out_post" if kind == 0 else "hyena_out_post")
        x2 = _mlp(h2, mlp_up[i].astype(BF16), mlp_down[i].astype(BF16), x2, mod3, i, norm_g[i, 3], l)
    return x2.reshape(b, l, d)
```

```python
import functools
import math

import jax
import jax.numpy as jnp
from jax import lax
from jax.experimental import pallas as pl
from jax.experimental.pallas import tpu as pltpu

F32 = jnp.float32
BF16 = jnp.bfloat16

HEAD_SIZE = 64
GRID_W = 64
N_MOD = 6
NORM_EPS = 1e-6
LNX_EPS = 64e-5
HY_FAST_DECAY = 0.3
HY_SLOW_DECAY = 1.5
HY_DECAY_TARGET = 1e-2
HY_EMB_DIM = 33
MOD_ROWS = 16
CHUNK = 64
LANES = 128
TOKEN_TILE = 256
VMEM_LIMIT_CAP = 60000 * 1024


def _cparams(sem, est_bytes):
    limit = int(min(max(2 * est_bytes, 32 * 1024 * 1024), VMEM_LIMIT_CAP))
    return pltpu.CompilerParams(dimension_semantics=sem, vmem_limit_bytes=limit)


def _largest_tile(n, cap, align):
    t = min(cap, n) // align * align
    while t > align and n % t:
        t -= align
    assert t > 0 and n % t == 0, (n, cap, align)
    return t


def _split3(x):
    hi = x.astype(BF16)
    r1 = x - hi.astype(F32)
    mid = r1.astype(BF16)
    lo = (r1 - mid.astype(F32)).astype(BF16)
    return hi, mid, lo


def _rms(x):
    return x * lax.rsqrt(jnp.mean(x * x, axis=-1, keepdims=True) + NORM_EPS)


def _norm_mod(x, g, shift, scale):
    return (_rms(x) * g) * (1.0 + scale) + shift


def _softplus(y):
    return jnp.maximum(y, 0.0) + jnp.log1p(jnp.exp(-jnp.abs(y)))


def _ada_kernel(c_ref, w_ref, b_ref, o_ref):
    c = c_ref[...]
    s = c * jax.nn.sigmoid(c)
    s_hi = s.astype(BF16)
    s_lo = (s - s_hi.astype(F32)).astype(BF16)
    w = w_ref[0]
    w_hi = w.astype(BF16)
    w_lo = (w - w_hi.astype(F32)).astype(BF16)
    p = jnp.dot(jnp.concatenate([s_hi, s_lo], axis=0), w_hi, preferred_element_type=F32)
    q = jnp.dot(s_hi, w_lo, preferred_element_type=F32)
    o_ref[0] = p[:MOD_ROWS] + p[MOD_ROWS:] + q + b_ref[0]


def _ada_mod(c_rows, ada_w, ada_b):
    depth, d, n = ada_w.shape
    tn = 1024
    out = pl.pallas_call(
        _ada_kernel,
        out_shape=jax.ShapeDtypeStruct((depth, MOD_ROWS, n), F32),
        grid=(depth, n // tn),
        in_specs=[pl.BlockSpec((MOD_ROWS, d), lambda l, j: (0, 0)),
                  pl.BlockSpec((1, d, tn), lambda l, j: (l, 0, j)),
                  pl.BlockSpec((1, 1, tn), lambda l, j: (l, 0, j))],
        out_specs=pl.BlockSpec((1, MOD_ROWS, tn), lambda l, j: (l, 0, j)),
        compiler_params=_cparams(("parallel", "parallel"), 2 * d * tn * 4 + 4 * d * tn),
        name="ada_mod",
    )(c_rows, ada_w, ada_b.reshape(depth, 1, n))
    return out.reshape(depth * MOD_ROWS * N_MOD, 1, d)


def _mod_row(layer, row, j):
    return (layer * MOD_ROWS + row) * N_MOD + j


def _pre_rwkv_kernel(hc_ref, sc_ref_ctx, xm_ref, xu_ref, xd_ref, g_ref, sh_ref, sc_ref, mu_ref,
                     wdec_ref, wa_ref, wg_ref, xr_ref, xk_ref, xv_ref, lw1_ref, a1_ref, g1_ref,
                     mw_scr, ma_scr, mg_scr, *, n_tiles):
    t = pl.program_id(1)
    d = xm_ref.shape[-1]
    g = g_ref[...]
    mu = mu_ref[...]
    big = {0: xr_ref, 2: xk_ref, 3: xv_ref}
    small = {1: mw_scr, 4: ma_scr, 5: mg_scr}

    def emit(h, s, lo, hi):
        xx = s - h
        for j in range(6):
            mix = (h + xx * mu[j:j + 1, lo:hi]).astype(BF16)
            if j in big:
                big[j][0, :, lo:hi] = mix
            else:
                small[j][:, lo:hi] = mix

    @pl.when(t == 0)
    def _():
        emit(hc_ref[0], sc_ref_ctx[0], 0, d)

    @pl.when(t > 0)
    def _():
        sh = sh_ref[0]
        sc = sc_ref[0]
        xm = xm_ref[0].reshape(TOKEN_TILE, d)
        hm = _norm_mod(xm, g, sh, sc)
        hu = _norm_mod(xu_ref[0, 0], g, sh, sc) * jnp.where(t > 1, 1.0, 0.0)
        hd = _norm_mod(xd_ref[0, 0], g, sh, sc) * jnp.where(t < n_tiles, 1.0, 0.0)
        col = lax.broadcasted_iota(jnp.int32, (TOKEN_TILE, 1), 0) & (GRID_W - 1)
        q = d // 4
        left = jnp.where(col != 0, pltpu.roll(hm[:, :q], 1, 0), 0.0)
        right = jnp.where(col != GRID_W - 1, pltpu.roll(hm[:, q:2 * q], TOKEN_TILE - 1, 0), 0.0)
        up = jnp.concatenate([hu[:, 2 * q:3 * q], hm[:TOKEN_TILE - GRID_W, 2 * q:3 * q]], axis=0)
        down = jnp.concatenate([hm[GRID_W:, 3 * q:], hd[:, 3 * q:]], axis=0)
        emit(hm[:, :q], left, 0, q)
        emit(hm[:, q:2 * q], right, q, 2 * q)
        emit(hm[:, 2 * q:3 * q], up, 2 * q, 3 * q)
        emit(hm[:, 3 * q:], down, 3 * q, d)

    lw1_ref[0] = jnp.tanh(jnp.dot(mw_scr[...], wdec_ref[...], preferred_element_type=F32)).astype(BF16)
    a1_ref[0] = jnp.dot(ma_scr[...], wa_ref[...], preferred_element_type=F32).astype(BF16)
    g1_ref[0] = jax.nn.sigmoid(jnp.dot(mg_scr[...], wg_ref[...], preferred_element_type=F32)).astype(BF16)


def _shift_seq(x):
    half = x.shape[-1] // 2
    p = jnp.pad(x, ((0, 0), (1, 1), (0, 0)))
    return jnp.concatenate([p[:, :-2, :half], p[:, 2:, half:]], axis=-1)


def _pre_rwkv(x, ctx, mod3, layer, g0, mu, w_dec1, w_a1, w_g1):
    b, l, d = x.shape
    ranks = (w_dec1.shape[1], w_a1.shape[1], w_g1.shape[1])
    n_ctx = ctx.shape[1]
    assert n_ctx == TOKEN_TILE and l % TOKEN_TILE == 0
    n_tiles = l // TOKEN_TILE
    rows_per_tile = TOKEN_TILE // GRID_W
    n_rows = l // GRID_W
    x4 = x.reshape(b, n_rows, GRID_W, d)
    s = n_ctx + l
    hc = _pre_norm(ctx.reshape(b * n_ctx, d), mod3, layer, g0, n_ctx, tm=n_ctx, fixed_row=b,
                   out_dtype=F32).reshape(b, n_ctx, d)
    sc = _shift_seq(hc)

    def mrow(j):
        return pl.BlockSpec((1, 1, d), lambda bi, t: (_mod_row(layer, bi, j), 0, 0))

    main = lambda bi, t: (bi, jnp.maximum(t - 1, 0), 0, 0)
    up = lambda bi, t: (bi, jnp.maximum((t - 1) * rows_per_tile - 1, 0), 0, 0)
    down = lambda bi, t: (bi, jnp.minimum(jnp.maximum(t, 1) * rows_per_tile, n_rows - 1), 0, 0)
    whole = lambda w: pl.BlockSpec(w.shape, lambda bi, t: (0, 0))
    outs = pl.pallas_call(
        functools.partial(_pre_rwkv_kernel, n_tiles=n_tiles),
        out_shape=[jax.ShapeDtypeStruct((b, s, d), BF16)] * 3
                  + [jax.ShapeDtypeStruct((b, s, rk), BF16) for rk in ranks],
        grid=(b, n_tiles + 1),
        in_specs=[pl.BlockSpec((1, n_ctx, d), lambda bi, t: (bi, 0, 0)),
                  pl.BlockSpec((1, n_ctx, d), lambda bi, t: (bi, 0, 0)),
                  pl.BlockSpec((1, rows_per_tile, GRID_W, d), main),
                  pl.BlockSpec((1, 1, GRID_W, d), up),
                  pl.BlockSpec((1, 1, GRID_W, d), down),
                  pl.BlockSpec((1, d), lambda bi, t: (0, 0)),
                  mrow(0), mrow(1),
                  pl.BlockSpec((6, d), lambda bi, t: (0, 0)),
                  whole(w_dec1), whole(w_a1), whole(w_g1)],
        out_specs=[pl.BlockSpec((1, TOKEN_TILE, d), lambda bi, t: (bi, t, 0))] * 3
                  + [pl.BlockSpec((1, TOKEN_TILE, rk), lambda bi, t: (bi, t, 0)) for rk in ranks],
        scratch_shapes=[pltpu.VMEM((TOKEN_TILE, d), BF16)] * 3,
        compiler_params=_cparams(("parallel", "arbitrary"),
                                 2 * (3 * TOKEN_TILE * d * 4 + 2 * GRID_W * d * 4 + 3 * TOKEN_TILE * d * 2
                                      + d * sum(ranks) * 2) + 3 * TOKEN_TILE * d * 2),
        name="rwkv_pre",
    )(hc, sc, x4, x4, x4, g0.reshape(1, d), mod3, mod3, mu, w_dec1, w_a1, w_g1)
    return outs


def _mm_kernel(a_ref, w_ref, *rest, act, has_bias):
    o_ref = rest[-1]
    acc = jnp.dot(a_ref[...], w_ref[...], preferred_element_type=F32)
    if has_bias:
        acc = acc + rest[0][...]
    if act == "tanh":
        acc = jnp.tanh(acc)
    elif act == "sigmoid":
        acc = jax.nn.sigmoid(acc)
    o_ref[...] = acc.astype(o_ref.dtype)


def _matmul(a, w, bias=None, act=None, out_dtype=F32, tm=1024, tn=1024, name="matmul"):
    m, k = a.shape
    n = w.shape[1]
    tm = _largest_tile(m, tm, 8)
    tn = _largest_tile(n, tn, LANES)
    in_specs = [pl.BlockSpec((tm, k), lambda i, j: (i, 0)),
                pl.BlockSpec((k, tn), lambda i, j: (0, j))]
    args = [a, w]
    if bias is not None:
        in_specs.append(pl.BlockSpec((1, tn), lambda i, j: (0, j)))
        args.append(bias.reshape(1, n).astype(F32))
    est = 2 * (tm * k * 2 + k * tn * 2 + tm * tn * jnp.dtype(out_dtype).itemsize) + tm * tn * 4
    return pl.pallas_call(
        functools.partial(_mm_kernel, act=act, has_bias=bias is not None),
        out_shape=jax.ShapeDtypeStruct((m, n), out_dtype),
        grid=(m // tm, n // tn),
        in_specs=in_specs,
        out_specs=pl.BlockSpec((tm, tn), lambda i, j: (i, j)),
        compiler_params=_cparams(("parallel", "parallel"), est),
        name=name,
    )(*args)


def _seg_sum(x, ones_bd):
    hi, mid, lo = _split3(x)
    r = x.shape[0]
    p = jnp.dot(jnp.concatenate([hi, mid, lo], axis=0), ones_bd, preferred_element_type=F32)
    return p[:r] + p[r:2 * r] + p[2 * r:]


def _bd(x, head0):
    return jnp.concatenate([jnp.where(head0, x, 0.0), jnp.where(head0, 0.0, x)], axis=0)


def _wkv_constants(rev):
    c = CHUNK
    shift = int(math.log2(c))
    head0 = lax.broadcasted_iota(jnp.int32, (1, LANES), 1) < HEAD_SIZE
    row2 = lax.broadcasted_iota(jnp.int32, (2 * c, 2 * c), 0)
    col2 = lax.broadcasted_iota(jnp.int32, (2 * c, 2 * c), 1)
    same = (row2 >> shift) == (col2 >> shift)
    tt = row2 & (c - 1)
    ss = col2 & (c - 1)
    if rev:
        tt, ss = ss, tt
    ones_bd = jnp.where(same, 1.0, 0.0).astype(BF16)
    eye = jnp.where(row2 == col2, 1.0, 0.0).astype(F32)
    rowc = lax.broadcasted_iota(jnp.int32, (c, c), 0)
    colc = lax.broadcasted_iota(jnp.int32, (c, c), 1)
    tri = jnp.where(colc >= rowc if rev else colc <= rowc, 1.0, 0.0).astype(BF16)
    merge = tuple(((tt >> lv) == (ss >> lv) + 1) & ((tt >> (lv + 1)) == (ss >> (lv + 1))) for lv in range(shift))
    return head0, ones_bd, tri, ss < tt, ss <= tt, eye, merge


def _each(fn, *lists):
    return [fn(*args) for args in zip(*lists)]


def _mm(a, b):
    return jnp.dot(a.astype(BF16), b.astype(BF16), preferred_element_type=F32)


def _wkv_prepare(probs, k_a, hooks=()):
    hooks = list(hooks)

    def run_hook():
        if hooks:
            hooks.pop(0)()

    revs, rs, ks, kks, bd_vs, lwxs, axs, w0s, a0s, csts = [list(t) for t in zip(*probs)]
    head0 = csts[0][0]
    eye = csts[0][5]
    c = rs[0].shape[0]
    c2 = 2 * c
    logw = _each(lambda w0, lwx: -jnp.exp(-_softplus(-(w0 + lwx)) - 0.5), w0s, lwxs)
    a = _each(lambda a0, ax: jax.nn.sigmoid(a0 + ax), a0s, axs)
    kd = _each(lambda k, a_: k * (1.0 + (a_ - 1.0) * k_a), ks, a)
    b = _each(lambda kk, a_: kk * a_, kks, a)

    def cumsum(lw, cst):
        hi, mid, lo = _split3(lw)
        c3 = jnp.dot(cst[2], jnp.concatenate([hi, mid, lo], axis=1), preferred_element_type=F32)
        return c3[:, :LANES] + c3[:, LANES:2 * LANES] + c3[:, 2 * LANES:]

    cum = _each(cumsum, logw, csts)
    total = _each(lambda cm, rev: cm[0:1] if rev else cm[c - 1:c], cum, revs)
    kap_t = _each(lambda kk, cm, lw: kk * jnp.exp(cm - lw), kks, cum, logw)
    r_t = _each(lambda r, cm: r * jnp.exp(cm), rs, cum)
    igam = _each(lambda cm: jnp.exp(-cm), cum)
    tail = _each(lambda t, cm: jnp.exp(t - cm), total, cum)

    bd_kap = _each(lambda x: _bd(x, head0), kap_t)
    bd_r = _each(lambda x: _bd(x, head0), r_t)
    lhs = _each(lambda p, q: jnp.concatenate([p, q], axis=0).astype(BF16), bd_kap, bd_r)
    rhs = _each(lambda kd_, b_, ig: jnp.concatenate([_bd(kd_ * ig, head0), _bd(b_ * ig, head0)],
                                                    axis=0).astype(BF16), kd, b, igam)
    gmat = _each(lambda l_, r_: lax.dot_general(l_, r_, (((1,), (1,)), ((), ())),
                                                preferred_element_type=F32), lhs, rhs)
    a_kb = _each(lambda g, cst: jnp.where(cst[3], g[:c2, c2:], 0.0), gmat, csts)
    run_hook()

    tinv = _each(lambda akb, cst: eye - jnp.where(cst[6][0], akb, 0.0), a_kb, csts)
    for lv in range(1, len(csts[0][6])):
        y = _each(lambda akb, x, cst: _mm(jnp.where(cst[6][lv], akb, 0.0), x), a_kb, tinv, csts)
        tinv = _each(lambda x, y_: x - _mm(x, y_), tinv, y)
        if lv % 2 == 0:
            run_hook()

    a_kr = _each(lambda g, cst: jnp.concatenate([jnp.where(cst[3], g[:c2, :c2], 0.0),
                                                 jnp.where(cst[4], g[c2:, :c2], 0.0)], axis=0), gmat, csts)
    av = _each(_mm, a_kr, bd_vs)
    while hooks:
        run_hook()
    wu = _each(lambda x, kap, av_: _mm(x, jnp.concatenate([kap, av_[:c2]], axis=1)), tinv, bd_kap, av)
    rb = _each(lambda g, cst, wu_: _mm(jnp.where(cst[4], g[c2:, c2:], 0.0), wu_), gmat, csts, wu)
    bct = _each(lambda b_, tl: _bd(b_ * tl, head0).T, b, tail)
    kct = _each(lambda kd_, tl: _bd(kd_ * tl, head0).T, kd, tail)
    mn = _each(_mm, bct, wu)
    kv = _each(_mm, kct, bd_vs)
    out = []
    for i in range(len(probs)):
        rk = bd_r[i] - rb[i][:, :LANES]
        ov = av[i][c2:] - rb[i][:, LANES:]
        m_mat = jnp.where(eye > 0.0, jnp.exp(total[i]), 0.0) - mn[i][:, :LANES]
        n_mat = kv[i] - mn[i][:, LANES:]
        out.append((jnp.concatenate([rk, m_mat], axis=0).astype(BF16), ov[:c] + ov[c:], n_mat))
    return out


def _wkv_apply(rm_lhs, ov, n_mat, h_state):
    c = ov.shape[0]
    rm = jnp.dot(rm_lhs, h_state.astype(BF16), preferred_element_type=F32)
    return rm[:c] + rm[c:2 * c] + ov, rm[2 * c:] + n_mat


def _wkv_kernel(r_ref, k_ref, v_ref, lw1_ref, a1_ref, g1_ref, wdec_ref, wa_ref, wg_ref,
                w0_ref, a0_ref, kk_ref, ka_ref, rk_ref, lnw_ref, lnb_ref,
                z_ref, of_scr, ob_scr, rm_scr, ov_scr, n_scr, *, n_ctx_chunks, n_chunks, unroll):
    c = CHUNK
    cst_f = _wkv_constants(False)
    cst_b = _wkv_constants(True)
    head0, ones_bd = cst_f[0], cst_f[1]

    w0f, w0b = w0_ref[0:1, :], w0_ref[1:2, :]
    a0f, a0b = a0_ref[0:1, :], a0_ref[1:2, :]
    k_k = kk_ref[...]
    k_a = ka_ref[...]

    def load(ref, rows):
        return ref[0, rows, :].astype(F32)

    def lora(x_ref, w_ref, rows, rev):
        d0 = int(rev) * LANES
        return jnp.dot(x_ref[0, rows, d0:d0 + LANES], w_ref[int(rev)], preferred_element_type=F32)

    def chunk_of(i, rev):
        if not rev:
            return i
        return jnp.where(i < n_ctx_chunks, n_ctx_chunks - 1 - i, n_chunks + n_ctx_chunks - 1 - i)

    def prepare_group(grp, hooks=()):
        probs, slots = [], []
        for u in range(unroll):
            for rev, w0, a0, cst in ((False, w0f, a0f, cst_f), (True, w0b, a0b, cst_b)):
                ci = chunk_of(grp * unroll + u, rev)
                rows = pl.ds(pl.multiple_of(ci * c, c), c)
                k = load(k_ref, rows)
                kk0 = k * k_k
                kk = kk0 * lax.rsqrt(jnp.maximum(_seg_sum(kk0 * kk0, ones_bd), 1e-24))
                bd_v = _bd(load(v_ref, rows), head0).astype(BF16)
                probs.append((rev, load(r_ref, rows), k, kk, bd_v, lora(lw1_ref, wdec_ref, rows, rev),
                              lora(a1_ref, wa_ref, rows, rev), w0, a0, cst))
                slots.append(ci + n_chunks * int(rev))
        for slot, (rm_lhs, ov, n_mat) in zip(slots, _wkv_prepare(probs, k_a, hooks)):
            rm_scr[slot] = rm_lhs
            ov_scr[slot] = ov
            n_scr[slot] = n_mat

    def state_steps(grp, state):
        def step(i):
            for rev, o_scr in ((False, of_scr), (True, ob_scr)):
                ci = chunk_of(i, rev)
                slot = ci + n_chunks * int(rev)
                o, state[int(rev)] = _wkv_apply(rm_scr[slot], ov_scr[slot], n_scr[slot], state[int(rev)])
                o_scr[pl.ds(pl.multiple_of(ci * c, c), c), :] = o
        return [functools.partial(step, grp * unroll + u) for u in range(unroll)]

    n_groups = n_chunks // unroll
    prepare_group(0)

    def body(grp, carry):
        state = list(carry)
        prepare_group(grp, state_steps(grp - 1, state))
        return tuple(state)

    zero = jnp.zeros((2 * c, LANES), F32)
    state = list(lax.fori_loop(1, n_groups, body, (zero, zero)))
    for step in state_steps(n_groups - 1, state):
        step()

    r_k = rk_ref[...]
    lnw = lnw_ref[...]
    lnb = lnb_ref[...]
    blk = TOKEN_TILE
    n_ctx = n_ctx_chunks * c
    inv_n = 1.0 / HEAD_SIZE

    def read_body(j, _):
        rows = pl.ds(pl.multiple_of(n_ctx + j * blk, blk), blk)
        o = of_scr[rows, :] + ob_scr[rows, :]
        mean = _seg_sum(o, ones_bd) * inv_n
        dev = o - mean
        var = _seg_sum(dev * dev, ones_bd) * inv_n
        on = dev * lax.rsqrt(var + LNX_EPS) * lnw + lnb
        r = load(r_ref, rows)
        k = load(k_ref, rows)
        v = load(v_ref, rows)
        a_sum = (jax.nn.sigmoid(a0f + lora(a1_ref, wa_ref, rows, False))
                 + jax.nn.sigmoid(a0b + lora(a1_ref, wa_ref, rows, True)))
        kd_sum = k * (2.0 + (a_sum - 2.0) * k_a)
        bonus = _seg_sum(r * kd_sum * r_k, ones_bd) * v
        gate = jnp.dot(g1_ref[0, rows, :], wg_ref[...], preferred_element_type=F32)
        z = (on + bonus) * gate
        z_ref[0, pl.ds(pl.multiple_of(j * blk, blk), blk), :] = z.astype(z_ref.dtype)
        return 0

    lax.fori_loop(0, (n_chunks - n_ctx_chunks) * c // blk, read_body, 0)


def _wkv(r, k, v, lw1, a1, g1, w_dec, w_a, w_g, dec_w0, a0, k_k, k_a, r_k, lnx_w, lnx_b, n_ctx):
    b, s, d = r.shape
    l = s - n_ctx
    npair = d // LANES
    seq = lambda bi, p: (bi, 0, p)
    vec = lambda bi, p: (0, p)
    sblk = pl.BlockSpec((1, s, LANES), seq)
    rank_blk = pl.BlockSpec((1, s, 2 * LANES), lambda bi, p: (bi, 0, 0))
    up_blk = pl.BlockSpec((2, LANES, LANES), lambda bi, p: (0, 0, p))
    n_chunks = s // CHUNK
    unroll = next(u for u in (6, 4, 2, 1) if n_chunks % u == 0)
    scratch = [pltpu.VMEM((s, LANES), F32), pltpu.VMEM((s, LANES), F32),
               pltpu.VMEM((2 * n_chunks, 4 * CHUNK, LANES), BF16),
               pltpu.VMEM((2 * n_chunks, CHUNK, LANES), F32),
               pltpu.VMEM((2 * n_chunks, 2 * CHUNK, LANES), F32)]
    est = (2 * (3 * s * LANES * 2 + 2 * s * 2 * LANES * 2 + s * g1.shape[2] * 2 + l * LANES * 2) + 2 * s * LANES * 4
           + 2 * n_chunks * CHUNK * LANES * (4 * 2 + 4 + 2 * 4))
    return pl.pallas_call(
        functools.partial(_wkv_kernel, n_ctx_chunks=n_ctx // CHUNK, n_chunks=n_chunks, unroll=unroll),
        out_shape=jax.ShapeDtypeStruct((b, l, d), BF16),
        grid=(b, npair),
        in_specs=[sblk, sblk, sblk, rank_blk, rank_blk,
                  pl.BlockSpec((1, s, g1.shape[2]), lambda bi, p: (bi, 0, 0)), up_blk, up_blk,
                  pl.BlockSpec((g1.shape[2], LANES), lambda bi, p: (0, p)),
                  pl.BlockSpec((2, LANES), vec), pl.BlockSpec((2, LANES), vec),
                  pl.BlockSpec((1, LANES), vec), pl.BlockSpec((1, LANES), vec), pl.BlockSpec((1, LANES), vec),
                  pl.BlockSpec((1, LANES), vec), pl.BlockSpec((1, LANES), vec)],
        out_specs=pl.BlockSpec((1, l, LANES), seq),
        scratch_shapes=scratch,
        compiler_params=_cparams(("parallel", "parallel"), est),
        name="wkv_scan",
    )(r, k, v, lw1, a1, g1, w_dec, w_a, w_g, dec_w0, a0, k_k.reshape(1, d), k_a.reshape(1, d), r_k.reshape(1, d),
      lnx_w.reshape(1, d), lnx_b.reshape(1, d))


def _proj_post_kernel(a_ref, w_ref, b_ref, x_ref, g1_ref, g2_ref, gate_ref, sh_ref, sc_ref, xo_ref, h_ref):
    y = jnp.dot(a_ref[...], w_ref[...], preferred_element_type=F32) + b_ref[...]
    x = x_ref[...] + gate_ref[0] * (_rms(y) * g1_ref[...])
    xo_ref[...] = x
    h_ref[...] = _norm_mod(x, g2_ref[...], sh_ref[0], sc_ref[0]).astype(h_ref.dtype)


def _proj_post(a, w, bias, x2, mod3, layer, g1, g2, l, tm=512, name="proj_post"):
    m, k = a.shape
    d = w.shape[1]
    per_b = l // tm
    row = lambda j: pl.BlockSpec((1, 1, d), lambda i: (_mod_row(layer, i // per_b, j), 0, 0))
    tile = pl.BlockSpec((tm, d), lambda i: (i, 0))
    vec = pl.BlockSpec((1, d), lambda i: (0, 0))
    est = k * d * 2 + 2 * (tm * k * 2 + tm * d * (4 + 4 + 2)) + 2 * tm * d * 4
    return pl.pallas_call(
        _proj_post_kernel,
        out_shape=[jax.ShapeDtypeStruct((m, d), F32), jax.ShapeDtypeStruct((m, d), BF16)],
        grid=(m // tm,),
        in_specs=[pl.BlockSpec((tm, k), lambda i: (i, 0)),
                  pl.BlockSpec((k, d), lambda i: (0, 0), pipeline_mode=pl.Buffered(1)),
                  vec, tile, vec, vec, row(2), row(3), row(4)],
        out_specs=[tile, tile],
        compiler_params=_cparams(("parallel",), est),
        name=name,
    )(a, w, bias.reshape(1, d).astype(F32), x2, g1.reshape(1, d), g2.reshape(1, d), mod3, mod3, mod3)


def _mlp_kernel(h_ref, wu_ref, wd_ref, x_ref, g_ref, gate_ref, *rest, with_next):
    if with_next:
        gn_ref, shn_ref, scn_ref, o_ref, hn_ref, acc_ref = rest
    else:
        o_ref, acc_ref = rest
    kf = pl.program_id(1)

    @pl.when(kf == 0)
    def _():
        acc_ref[...] = jnp.zeros_like(acc_ref)

    u = jnp.dot(h_ref[...], wu_ref[...], preferred_element_type=F32)
    u = jnp.square(jnp.maximum(u, 0.0)).astype(BF16)
    acc_ref[...] += jnp.dot(u, wd_ref[...], preferred_element_type=F32)

    @pl.when(kf == pl.num_programs(1) - 1)
    def _():
        x = x_ref[...] + gate_ref[0] * (_rms(acc_ref[...]) * g_ref[...])
        o_ref[...] = x
        if with_next:
            hn_ref[...] = _norm_mod(x, gn_ref[...], shn_ref[0], scn_ref[0]).astype(hn_ref.dtype)


def _mlp(h2, w_up, w_down, x2, mod3, layer, g3, l, next_g0=None, tm=512, tf=1024):
    m, d = h2.shape
    dff = w_up.shape[1]
    per_b = l // tm
    with_next = next_g0 is not None
    row = lambda lay, j: pl.BlockSpec((1, 1, d), lambda i, f: (_mod_row(lay, i // per_b, j), 0, 0))
    tile = pl.BlockSpec((tm, d), lambda i, f: (i, 0))
    vec = pl.BlockSpec((1, d), lambda i, f: (0, 0))
    in_specs = [tile, pl.BlockSpec((d, tf), lambda i, f: (0, f)), pl.BlockSpec((tf, d), lambda i, f: (f, 0)),
                tile, vec, row(layer, 5)]
    args = [h2, w_up, w_down, x2, g3.reshape(1, d), mod3]
    out_shape = [jax.ShapeDtypeStruct((m, d), F32)]
    if with_next:
        in_specs += [vec, row(layer + 1, 0), row(layer + 1, 1)]
        args += [next_g0.reshape(1, d), mod3, mod3]
        out_shape.append(jax.ShapeDtypeStruct((m, d), BF16))
    est = 2 * (tm * d * 2 + 2 * d * tf * 2 + 2 * tm * d * 4 + tm * d * 2) + tm * d * 4 + tm * tf * 6
    outs = pl.pallas_call(
        functools.partial(_mlp_kernel, with_next=with_next),
        out_shape=out_shape,
        grid=(m // tm, dff // tf),
        in_specs=in_specs,
        out_specs=[tile] * len(out_shape),
        scratch_shapes=[pltpu.VMEM((tm, d), F32)],
        compiler_params=_cparams(("parallel", "arbitrary"), est),
        name="mlp",
    )(*args)
    return outs if with_next else (outs[0], None)


def _pre_norm_kernel(x_ref, g_ref, sh_ref, sc_ref, h_ref):
    h_ref[...] = _norm_mod(x_ref[...], g_ref[...], sh_ref[0], sc_ref[0]).astype(h_ref.dtype)


def _pre_norm(x2, mod3, layer, g0, l, tm=512, fixed_row=None, out_dtype=BF16):
    m, d = x2.shape
    per_b = l // tm
    if fixed_row is None:
        row = lambda j: pl.BlockSpec((1, 1, d), lambda i: (_mod_row(layer, i // per_b, j), 0, 0))
    else:
        row = lambda j: pl.BlockSpec((1, 1, d), lambda i: (_mod_row(layer, fixed_row, j), 0, 0))
    tile = pl.BlockSpec((tm, d), lambda i: (i, 0))
    return pl.pallas_call(
        _pre_norm_kernel,
        out_shape=jax.ShapeDtypeStruct((m, d), out_dtype),
        grid=(m // tm,),
        in_specs=[tile, pl.BlockSpec((1, d), lambda i: (0, 0)), row(0), row(1)],
        out_specs=tile,
        compiler_params=_cparams(("parallel",), 2 * tm * d * 6),
        name="pre_norm",
    )(x2, g0.reshape(1, d), mod3, mod3)


def _filter_kernel(z_ref, w1_ref, w2_ref, w3_ref, b_ref, fr_ref, w4_ref, t_ref, dl_ref, o_ref, hid_ref):
    hp = lax.Precision.HIGHEST

    @pl.when(pl.program_id(0) == 0)
    def _():
        b = b_ref[...]
        fr = fr_ref[...]
        z = jnp.sin(fr[0:1] * (jnp.dot(z_ref[...], w1_ref[...], precision=hp, preferred_element_type=F32) + b[0:1]))
        z = jnp.sin(fr[1:2] * (jnp.dot(z, w2_ref[...], precision=hp, preferred_element_type=F32) + b[1:2]))
        hid_ref[...] = jnp.sin(fr[2:3] * (jnp.dot(z, w3_ref[...], precision=hp, preferred_element_type=F32)
                                          + b[2:3]))

    filt = jnp.dot(hid_ref[...], w4_ref[...], precision=hp, preferred_element_type=F32)
    o_ref[...] = filt * jnp.exp(-t_ref[...] * dl_ref[...])


def _hyena_filters(l, d, f_w1, f_w23, f_w4, f_b, f_freq):
    t = jnp.linspace(0.0, 1.0, l, dtype=F32)[:, None]
    bands = (HY_EMB_DIM - 1) // 2
    freqs = jnp.linspace(1e-4, bands - 1, bands, dtype=F32)[None, :]
    ang = (2.0 * math.pi / l) * jnp.arange(l, dtype=F32)[:, None] * freqs
    z = jnp.concatenate([t, jnp.cos(ang), -jnp.sin(ang)], axis=-1)
    e, f = f_w1.shape
    pad = lambda a_, r, c: jnp.pad(a_.astype(F32), ((0, r - a_.shape[0]), (0, c - a_.shape[1])))
    zp = pad(z, l, LANES)
    w1 = pad(f_w1, LANES, LANES)
    w2 = pad(f_w23[0], LANES, LANES)
    w3 = pad(f_w23[1], LANES, LANES)
    bb = pad(f_b, 8, LANES)
    fr = pad(f_freq, 8, LANES)
    n = f_w4.shape[1]
    w4 = pad(f_w4, LANES, n)
    max_decay = math.log(HY_DECAY_TARGET) / HY_FAST_DECAY
    min_decay = math.log(HY_DECAY_TARGET) / HY_SLOW_DECAY
    deltas = jnp.abs(jnp.linspace(min_decay, max_decay, d, dtype=F32))[None, :]
    tn = _largest_tile(d, 1024, LANES)
    per_d = d // tn
    sq = pl.BlockSpec((LANES, LANES), lambda j: (0, 0))
    small = pl.BlockSpec((8, LANES), lambda j: (0, 0))
    return pl.pallas_call(
        _filter_kernel,
        out_shape=jax.ShapeDtypeStruct((l, n), F32),
        grid=(n // tn,),
        in_specs=[pl.BlockSpec((l, LANES), lambda j: (0, 0)), sq, sq, sq, small, small,
                  pl.BlockSpec((LANES, tn), lambda j: (0, j)),
                  pl.BlockSpec((l, 1), lambda j: (0, 0)),
                  pl.BlockSpec((1, tn), lambda j: (0, j % per_d))],
        out_specs=pl.BlockSpec((l, tn), lambda j: (0, j)),
        scratch_shapes=[pltpu.VMEM((l, LANES), F32)],
        compiler_params=_cparams(("arbitrary",), 4 * l * tn * 4),
        name="hyena_filters",
    )(zp, w1, w2, w3, bb, fr, w4, t, deltas)


def _dft_matrices(l):
    n = 2 * l
    k = jnp.arange(l, dtype=jnp.int32)[:, None]
    t = jnp.arange(l, dtype=jnp.int32)[None, :]
    ang = ((k * t) % n).astype(F32) * (2.0 * math.pi / n)
    cos = jnp.cos(ang)
    msin = jnp.where(k == 0, jnp.where(t % 2 == 0, 1.0, -1.0), -jnp.sin(ang))
    fwd = jnp.concatenate([cos, msin], axis=0)
    return fwd.astype(BF16), fwd.T.astype(BF16)


def _short_conv(z, cw, cb):
    n = z.shape[0]
    row = lax.broadcasted_iota(jnp.int32, (n, 1), 0)
    prev = jnp.where(row != 0, pltpu.roll(z, 1, 0), 0.0)
    nxt = jnp.where(row != n - 1, pltpu.roll(z, n - 1, 0), 0.0)
    return prev * cw[0:1] + z * cw[1:2] + nxt * cw[2:3] + cb


def _spec_kernel(f_ref, hf_ref, hb_ref, o_ref):
    l, tn = hf_ref.shape
    row = lax.broadcasted_iota(jnp.int32, (l, 1), 0)
    hb = jnp.where(row == 0, 0.0, hb_ref[...])
    h2 = jnp.concatenate([hf_ref[...], hb], axis=1).astype(BF16)
    u = jnp.dot(f_ref[...], h2, preferred_element_type=F32)
    row2 = lax.broadcasted_iota(jnp.int32, (2 * l, 1), 0)
    sign = jnp.where(row2 > l, -1.0, 1.0)
    scale = jnp.where((row2 == 0) | (row2 == l), 1.0 / (2 * l), 2.0 / (2 * l))
    o_ref[0] = (u[:, :tn] + sign * u[:, tn:]) * scale


def _spectrum(fwd, filt, d, tn=256):
    l = filt.shape[0]
    n2 = fwd.shape[0]
    orders = filt.shape[1] // (2 * d)
    per = d // tn
    return pl.pallas_call(
        _spec_kernel,
        out_shape=jax.ShapeDtypeStruct((orders, n2, d), F32),
        grid=(orders, per),
        in_specs=[pl.BlockSpec((n2, l), lambda o, j: (0, 0), pipeline_mode=pl.Buffered(1)),
                  pl.BlockSpec((l, tn), lambda o, j: (0, o * per + j)),
                  pl.BlockSpec((l, tn), lambda o, j: (0, (orders + o) * per + j))],
        out_specs=pl.BlockSpec((1, n2, tn), lambda o, j: (o, 0, j)),
        compiler_params=_cparams(("parallel", "parallel"), n2 * l * 2 + 2 * (2 * l * tn * 4 + n2 * tn * 4)
                                 + 3 * n2 * tn * 4),
        name="filter_spectrum",
    )(fwd, filt, filt)


def _conv_fwd_kernel(f_ref, y_ref, k_ref, cw_ref, cb_ref, p_ref, *, short_conv, n_split):
    l = y_ref.shape[1]
    y = y_ref[0]
    if short_conv:
        y = _short_conv(y, cw_ref[...], cb_ref[...])
    yb = y.astype(BF16)
    rows = l // n_split
    for s in range(n_split):
        lo, hi = s * rows, (s + 1) * rows
        ure = jnp.dot(f_ref[lo:hi, :], yb, preferred_element_type=F32)
        uim = jnp.dot(f_ref[l + lo:l + hi, :], yb, preferred_element_type=F32)
        kre = k_ref[0, lo:hi, :]
        kim = k_ref[0, l + lo:l + hi, :]
        pre = ure * kre - uim * kim
        pim = ure * kim + uim * kre
        if s == 0:
            first = lax.broadcasted_iota(jnp.int32, (rows, 1), 0) == 0
            pre = jnp.where(first, ure * kre, pre)
            pim = jnp.where(first, uim * kim, pim)
        p_ref[0, lo:hi, :] = pre.astype(p_ref.dtype)
        p_ref[0, l + lo:l + hi, :] = pim.astype(p_ref.dtype)


def _conv_fwd(fwd, y, y_col0, kspec, order, conv_w, conv_b, short_conv, d, tn=256):
    b, l, _ = y.shape
    n2 = fwd.shape[0]
    off = y_col0 // tn
    est = n2 * l * 2 + 2 * (l * tn * 4 + n2 * tn * 4 + n2 * tn * 2) + 6 * l * tn * 4
    return pl.pallas_call(
        functools.partial(_conv_fwd_kernel, short_conv=short_conv, n_split=2),
        out_shape=jax.ShapeDtypeStruct((b, n2, d), BF16),
        grid=(d // tn, b),
        in_specs=[pl.BlockSpec((n2, l), lambda j, bi: (0, 0), pipeline_mode=pl.Buffered(1)),
                  pl.BlockSpec((1, l, tn), lambda j, bi: (bi, 0, j + off)),
                  pl.BlockSpec((1, n2, tn), lambda j, bi: (order, 0, j)),
                  pl.BlockSpec((3, tn), lambda j, bi: (0, j + off)),
                  pl.BlockSpec((1, tn), lambda j, bi: (0, j + off))],
        out_specs=pl.BlockSpec((1, n2, tn), lambda j, bi: (bi, 0, j)),
        compiler_params=_cparams(("parallel", "parallel"), est),
        name="hyena_conv_fwd",
    )(fwd, y, kspec, conv_w, conv_b)


def _conv_inv_kernel(ft_ref, p_ref, yp_ref, gt_ref, cwy_ref, cby_ref, cwg_ref, cbg_ref, sk_ref, o_ref,
                     *, short_conv_prev):
    conv = jnp.dot(ft_ref[...], p_ref[0], preferred_element_type=F32)
    yp = yp_ref[0]
    if short_conv_prev:
        yp = _short_conv(yp, cwy_ref[...], cby_ref[...])
    gate = _short_conv(gt_ref[0], cwg_ref[...], cbg_ref[...])
    o_ref[0] = (gate * (conv + sk_ref[...] * yp)).astype(o_ref.dtype)


def _conv_inv(finv, p, yprev, yprev_col0, short_conv_prev, z, gate_col0, conv_w, conv_b, skip, out_dtype, tn=256):
    b, n2, d = p.shape
    l = n2 // 2
    offy = yprev_col0 // tn
    offg = gate_col0 // tn
    est = l * n2 * 2 + 2 * (n2 * tn * 2 + 3 * l * tn * 4) + 6 * l * tn * 4
    return pl.pallas_call(
        functools.partial(_conv_inv_kernel, short_conv_prev=short_conv_prev),
        out_shape=jax.ShapeDtypeStruct((b, l, d), out_dtype),
        grid=(d // tn, b),
        in_specs=[pl.BlockSpec((l, n2), lambda j, bi: (0, 0), pipeline_mode=pl.Buffered(1)),
                  pl.BlockSpec((1, n2, tn), lambda j, bi: (bi, 0, j)),
                  pl.BlockSpec((1, l, tn), lambda j, bi: (bi, 0, j + offy)),
                  pl.BlockSpec((1, l, tn), lambda j, bi: (bi, 0, j + offg)),
                  pl.BlockSpec((3, tn), lambda j, bi: (0, j + offy)),
                  pl.BlockSpec((1, tn), lambda j, bi: (0, j + offy)),
                  pl.BlockSpec((3, tn), lambda j, bi: (0, j + offg)),
                  pl.BlockSpec((1, tn), lambda j, bi: (0, j + offg)),
                  pl.BlockSpec((1, tn), lambda j, bi: (0, j))],
        out_specs=pl.BlockSpec((1, l, tn), lambda j, bi: (bi, 0, j)),
        compiler_params=_cparams(("parallel", "parallel"), est),
        name="hyena_conv_inv",
    )(finv, p, yprev, z, conv_w, conv_b, conv_w, conv_b, skip)


def _hyena_mixer(h2, b, l, in_w, in_b, conv_w, conv_b, f_w1, f_w23, f_w4, f_b, f_freq, skip, out_w, out_b):
    d = h2.shape[1]
    z = _matmul(h2, in_w.astype(BF16), bias=in_b, name="hyena_in").reshape(b, l, 3 * d)
    filt = _hyena_filters(l, d, f_w1, f_w23, f_w4, f_b, f_freq)
    fwd, finv = _dft_matrices(l)
    kspec = _spectrum(fwd, filt, d)
    skip = skip.astype(F32)
    p0 = _conv_fwd(fwd, z, 0, kspec, 0, conv_w, conv_b.reshape(1, -1), True, d)
    y1 = _conv_inv(finv, p0, z, 0, True, z, d, conv_w, conv_b.reshape(1, -1), skip[0:1], F32)
    p1 = _conv_fwd(fwd, y1, 0, kspec, 1, conv_w, conv_b.reshape(1, -1), False, d)
    y2 = _conv_inv(finv, p1, y1, 0, False, z, 2 * d, conv_w, conv_b.reshape(1, -1), skip[1:2], BF16)
    return y2.reshape(b * l, d), out_w.astype(BF16), out_b


def _lora_in(w):
    pad = lambda m: jnp.pad(m, ((0, 0), (0, LANES - m.shape[1])))
    return jnp.concatenate([pad(w[0]), pad(w[1])], axis=1)


def _lora_out(w):
    return jnp.pad(w, ((0, 0), (0, LANES - w.shape[1]), (0, 0)))


def _rwkv_mixer(x, ctx, mod3, layer, g0, mu, w_r, w_k, w_v, w_o, dec_w0, dec_w1, dec_w2, a0, a1, a2,
                g1, g2, k_k, k_a, r_k, lnx_w, lnx_b):
    b, l, d = x.shape
    n_ctx = ctx.shape[1]
    s = n_ctx + l
    bf = lambda w: w.astype(BF16)
    assert dec_w1.shape[2] <= LANES and a1.shape[2] <= LANES
    xr, xk, xv, lw1, a1o, g1o = _pre_rwkv(x, ctx, mod3, layer, g0, mu, bf(_lora_in(dec_w1)), bf(_lora_in(a1)),
                                          bf(g1))
    r = _matmul(xr.reshape(b * s, d), bf(w_r), out_dtype=BF16, name="rwkv_r").reshape(b, s, d)
    k = _matmul(xk.reshape(b * s, d), bf(w_k), out_dtype=BF16, name="rwkv_k").reshape(b, s, d)
    v = _matmul(xv.reshape(b * s, d), bf(w_v), out_dtype=BF16, name="rwkv_v").reshape(b, s, d)
    zz = _wkv(r, k, v, lw1, a1o, g1o, bf(_lora_out(dec_w2)), bf(_lora_out(a2)), bf(g2), dec_w0, a0, k_k, k_a, r_k,
              lnx_w, lnx_b, n_ctx)
    return zz.reshape(b * l, d), bf(w_o), jnp.zeros((d,), F32)


def kernel(x, c, ctx, c_ctx, ada_w, ada_b, norm_g, mlp_up, mlp_down, rw_mu, rw_w_r, rw_w_k, rw_w_v, rw_w_o, rw_dec_w0, rw_dec_w1, rw_dec_w2, rw_a0, rw_a1, rw_a2, rw_g1, rw_g2, rw_k_k, rw_k_a, rw_r_k, rw_lnx_w, rw_lnx_b, hy_in_w, hy_in_b, hy_conv_w, hy_conv_b, hy_f_w1, hy_f_w23, hy_f_w4, hy_f_b, hy_f_freq, hy_skip, hy_out_w, hy_out_b):
    b, l, d = x.shape
    depth = ada_w.shape[0]
    assert b < MOD_ROWS
    c_rows = jnp.zeros((MOD_ROWS, d), F32).at[:b].set(c).at[b].set(c_ctx)
    mod3 = _ada_mod(c_rows, ada_w, ada_b)
    x2 = x.reshape(b * l, d)
    xc = ctx
    h_next = None
    for i in range(depth):
        kind, j = i % 2, i // 2
        ctx_live = any(q % 2 == 0 for q in range(i + 1, depth))
        assert not ctx_live, "context-stream update is not implemented for this depth"
        if kind == 0:
            mixed = _rwkv_mixer(x2.reshape(b, l, d), xc, mod3, i, norm_g[i, 0], rw_mu[j], rw_w_r[j], rw_w_k[j],
                                rw_w_v[j], rw_w_o[j], rw_dec_w0[j], rw_dec_w1[j], rw_dec_w2[j], rw_a0[j],
                                rw_a1[j], rw_a2[j], rw_g1[j], rw_g2[j], rw_k_k[j], rw_k_a[j], rw_r_k[j],
                                rw_lnx_w[j], rw_lnx_b[j])
        else:
            h = h_next if h_next is not None else _pre_norm(x2, mod3, i, norm_g[i, 0], l)
            mixed = _hyena_mixer(h, b, l, hy_in_w[j], hy_in_b[j], hy_conv_w[j], hy_conv_b[j], hy_f_w1[j],
                                 hy_f_w23[j], hy_f_w4[j], hy_f_b[j], hy_f_freq[j], hy_skip[j], hy_out_w[j],
                                 hy_out_b[j])
        x2, h2 = _proj_post(*mixed, x2, mod3, i, norm_g[i, 1], norm_g[i, 2], l,
                            name="rwkv_out_post" if kind == 0 else "hyena_out_post")
        next_is_hyena = i + 1 < depth and (i + 1) % 2 == 1
        x2, h_next = _mlp(h2, mlp_up[i].astype(BF16), mlp_down[i].astype(BF16), x2, mod3, i, norm_g[i, 3], l,
                          next_g0=norm_g[i + 1, 0] if next_is_hyena else None)
    return x2.reshape(b, l, d)
```

```python
import functools
import math

import jax
import jax.numpy as jnp
from jax import lax
from jax.experimental import pallas as pl
from jax.experimental.pallas import tpu as pltpu

F32 = jnp.float32
BF16 = jnp.bfloat16

HEAD_SIZE = 64
GRID_W = 64
N_MOD = 6
NORM_EPS = 1e-6
LNX_EPS = 64e-5
HY_FAST_DECAY = 0.3
HY_SLOW_DECAY = 1.5
HY_DECAY_TARGET = 1e-2
HY_EMB_DIM = 33
MOD_ROWS = 16
CHUNK = 64
LANES = 128
TOKEN_TILE = 256
VMEM_LIMIT_CAP = 60000 * 1024


def _cparams(sem, est_bytes):
    limit = int(min(max(2 * est_bytes, 32 * 1024 * 1024), VMEM_LIMIT_CAP))
    return pltpu.CompilerParams(dimension_semantics=sem, vmem_limit_bytes=limit)


def _largest_tile(n, cap, align):
    t = min(cap, n) // align * align
    while t > align and n % t:
        t -= align
    assert t > 0 and n % t == 0, (n, cap, align)
    return t


def _split3(x):
    hi = x.astype(BF16)
    r1 = x - hi.astype(F32)
    mid = r1.astype(BF16)
    lo = (r1 - mid.astype(F32)).astype(BF16)
    return hi, mid, lo


def _rms(x):
    return x * lax.rsqrt(jnp.mean(x * x, axis=-1, keepdims=True) + NORM_EPS)


def _norm_mod(x, g, shift, scale):
    return (_rms(x) * g) * (1.0 + scale) + shift


def _softplus(y):
    return jnp.maximum(y, 0.0) + jnp.log1p(jnp.exp(-jnp.abs(y)))


def _ada_kernel(c_ref, w_ref, b_ref, o_ref):
    c = c_ref[...]
    s = c * jax.nn.sigmoid(c)
    s_hi = s.astype(BF16)
    s_lo = (s - s_hi.astype(F32)).astype(BF16)
    w = w_ref[0]
    w_hi = w.astype(BF16)
    w_lo = (w - w_hi.astype(F32)).astype(BF16)
    p = jnp.dot(jnp.concatenate([s_hi, s_lo], axis=0), w_hi, preferred_element_type=F32)
    q = jnp.dot(s_hi, w_lo, preferred_element_type=F32)
    o_ref[0] = p[:MOD_ROWS] + p[MOD_ROWS:] + q + b_ref[0]


def _ada_mod(c_rows, ada_w, ada_b):
    depth, d, n = ada_w.shape
    tn = 1024
    out = pl.pallas_call(
        _ada_kernel,
        out_shape=jax.ShapeDtypeStruct((depth, MOD_ROWS, n), F32),
        grid=(depth, n // tn),
        in_specs=[pl.BlockSpec((MOD_ROWS, d), lambda l, j: (0, 0)),
                  pl.BlockSpec((1, d, tn), lambda l, j: (l, 0, j)),
                  pl.BlockSpec((1, 1, tn), lambda l, j: (l, 0, j))],
        out_specs=pl.BlockSpec((1, MOD_ROWS, tn), lambda l, j: (l, 0, j)),
        compiler_params=_cparams(("parallel", "parallel"), 2 * d * tn * 4 + 4 * d * tn),
        name="ada_mod",
    )(c_rows, ada_w, ada_b.reshape(depth, 1, n))
    return out.reshape(depth * MOD_ROWS * N_MOD, 1, d)


def _mod_row(layer, row, j):
    return (layer * MOD_ROWS + row) * N_MOD + j


def _pre_rwkv_kernel(hc_ref, sc_ref_ctx, xm_ref, xu_ref, xd_ref, g_ref, sh_ref, sc_ref, mu_ref,
                     wdec_ref, wa_ref, wg_ref, xr_ref, xk_ref, xv_ref, lw1_ref, a1_ref, g1_ref,
                     mw_scr, ma_scr, mg_scr, *, n_tiles):
    t = pl.program_id(1)
    d = xm_ref.shape[-1]
    g = g_ref[...]
    mu = mu_ref[...]
    big = {0: xr_ref, 2: xk_ref, 3: xv_ref}
    small = {1: mw_scr, 4: ma_scr, 5: mg_scr}

    def emit(h, s, lo, hi):
        xx = s - h
        for j in range(6):
            mix = (h + xx * mu[j:j + 1, lo:hi]).astype(BF16)
            if j in big:
                big[j][0, :, lo:hi] = mix
            else:
                small[j][:, lo:hi] = mix

    @pl.when(t == 0)
    def _():
        emit(hc_ref[0], sc_ref_ctx[0], 0, d)

    @pl.when(t > 0)
    def _():
        sh = sh_ref[0]
        sc = sc_ref[0]
        xm = xm_ref[0].reshape(TOKEN_TILE, d)
        hm = _norm_mod(xm, g, sh, sc)
        hu = _norm_mod(xu_ref[0, 0], g, sh, sc) * jnp.where(t > 1, 1.0, 0.0)
        hd = _norm_mod(xd_ref[0, 0], g, sh, sc) * jnp.where(t < n_tiles, 1.0, 0.0)
        col = lax.broadcasted_iota(jnp.int32, (TOKEN_TILE, 1), 0) & (GRID_W - 1)
        q = d // 4
        left = jnp.where(col != 0, pltpu.roll(hm[:, :q], 1, 0), 0.0)
        right = jnp.where(col != GRID_W - 1, pltpu.roll(hm[:, q:2 * q], TOKEN_TILE - 1, 0), 0.0)
        up = jnp.concatenate([hu[:, 2 * q:3 * q], hm[:TOKEN_TILE - GRID_W, 2 * q:3 * q]], axis=0)
        down = jnp.concatenate([hm[GRID_W:, 3 * q:], hd[:, 3 * q:]], axis=0)
        emit(hm[:, :q], left, 0, q)
        emit(hm[:, q:2 * q], right, q, 2 * q)
        emit(hm[:, 2 * q:3 * q], up, 2 * q, 3 * q)
        emit(hm[:, 3 * q:], down, 3 * q, d)

    lw1_ref[0] = jnp.tanh(jnp.dot(mw_scr[...], wdec_ref[...], preferred_element_type=F32)).astype(BF16)
    a1_ref[0] = jnp.dot(ma_scr[...], wa_ref[...], preferred_element_type=F32).astype(BF16)
    g1_ref[0] = jax.nn.sigmoid(jnp.dot(mg_scr[...], wg_ref[...], preferred_element_type=F32)).astype(BF16)


def _shift_seq(x):
    half = x.shape[-1] // 2
    p = jnp.pad(x, ((0, 0), (1, 1), (0, 0)))
    return jnp.concatenate([p[:, :-2, :half], p[:, 2:, half:]], axis=-1)


def _pre_rwkv(x, ctx, mod3, layer, g0, mu, w_dec1, w_a1, w_g1):
    b, l, d = x.shape
    ranks = (w_dec1.shape[1], w_a1.shape[1], w_g1.shape[1])
    n_ctx = ctx.shape[1]
    assert n_ctx == TOKEN_TILE and l % TOKEN_TILE == 0
    n_tiles = l // TOKEN_TILE
    rows_per_tile = TOKEN_TILE // GRID_W
    n_rows = l // GRID_W
    x4 = x.reshape(b, n_rows, GRID_W, d)
    s = n_ctx + l
    hc = _pre_norm(ctx.reshape(b * n_ctx, d), mod3, layer, g0, n_ctx, tm=n_ctx, fixed_row=b,
                   out_dtype=F32).reshape(b, n_ctx, d)
    sc = _shift_seq(hc)

    def mrow(j):
        return pl.BlockSpec((1, 1, d), lambda bi, t: (_mod_row(layer, bi, j), 0, 0))

    main = lambda bi, t: (bi, jnp.maximum(t - 1, 0), 0, 0)
    up = lambda bi, t: (bi, jnp.maximum((t - 1) * rows_per_tile - 1, 0), 0, 0)
    down = lambda bi, t: (bi, jnp.minimum(jnp.maximum(t, 1) * rows_per_tile, n_rows - 1), 0, 0)
    whole = lambda w: pl.BlockSpec(w.shape, lambda bi, t: (0, 0))
    outs = pl.pallas_call(
        functools.partial(_pre_rwkv_kernel, n_tiles=n_tiles),
        out_shape=[jax.ShapeDtypeStruct((b, s, d), BF16)] * 3
                  + [jax.ShapeDtypeStruct((b, s, rk), BF16) for rk in ranks],
        grid=(b, n_tiles + 1),
        in_specs=[pl.BlockSpec((1, n_ctx, d), lambda bi, t: (bi, 0, 0)),
                  pl.BlockSpec((1, n_ctx, d), lambda bi, t: (bi, 0, 0)),
                  pl.BlockSpec((1, rows_per_tile, GRID_W, d), main),
                  pl.BlockSpec((1, 1, GRID_W, d), up),
                  pl.BlockSpec((1, 1, GRID_W, d), down),
                  pl.BlockSpec((1, d), lambda bi, t: (0, 0)),
                  mrow(0), mrow(1),
                  pl.BlockSpec((6, d), lambda bi, t: (0, 0)),
                  whole(w_dec1), whole(w_a1), whole(w_g1)],
        out_specs=[pl.BlockSpec((1, TOKEN_TILE, d), lambda bi, t: (bi, t, 0))] * 3
                  + [pl.BlockSpec((1, TOKEN_TILE, rk), lambda bi, t: (bi, t, 0)) for rk in ranks],
        scratch_shapes=[pltpu.VMEM((TOKEN_TILE, d), BF16)] * 3,
        compiler_params=_cparams(("parallel", "arbitrary"),
                                 2 * (3 * TOKEN_TILE * d * 4 + 2 * GRID_W * d * 4 + 3 * TOKEN_TILE * d * 2
                                      + d * sum(ranks) * 2) + 3 * TOKEN_TILE * d * 2),
        name="rwkv_pre",
    )(hc, sc, x4, x4, x4, g0.reshape(1, d), mod3, mod3, mu, w_dec1, w_a1, w_g1)
    return outs


def _mm_kernel(a_ref, w_ref, *rest, act, has_bias):
    o_ref = rest[-1]
    acc = jnp.dot(a_ref[...], w_ref[...], preferred_element_type=F32)
    if has_bias:
        acc = acc + rest[0][...]
    if act == "tanh":
        acc = jnp.tanh(acc)
    elif act == "sigmoid":
        acc = jax.nn.sigmoid(acc)
    o_ref[...] = acc.astype(o_ref.dtype)


def _matmul(a, w, bias=None, act=None, out_dtype=F32, tm=1024, tn=1024, name="matmul"):
    m, k = a.shape
    n = w.shape[1]
    tm = _largest_tile(m, tm, 8)
    tn = _largest_tile(n, tn, LANES)
    in_specs = [pl.BlockSpec((tm, k), lambda i, j: (i, 0)),
                pl.BlockSpec((k, tn), lambda i, j: (0, j))]
    args = [a, w]
    if bias is not None:
        in_specs.append(pl.BlockSpec((1, tn), lambda i, j: (0, j)))
        args.append(bias.reshape(1, n).astype(F32))
    est = 2 * (tm * k * 2 + k * tn * 2 + tm * tn * jnp.dtype(out_dtype).itemsize) + tm * tn * 4
    return pl.pallas_call(
        functools.partial(_mm_kernel, act=act, has_bias=bias is not None),
        out_shape=jax.ShapeDtypeStruct((m, n), out_dtype),
        grid=(m // tm, n // tn),
        in_specs=in_specs,
        out_specs=pl.BlockSpec((tm, tn), lambda i, j: (i, j)),
        compiler_params=_cparams(("parallel", "parallel"), est),
        name=name,
    )(*args)


def _seg_sum(x, ones_bd):
    hi, mid, lo = _split3(x)
    r = x.shape[0]
    p = jnp.dot(jnp.concatenate([hi, mid, lo], axis=0), ones_bd, preferred_element_type=F32)
    return p[:r] + p[r:2 * r] + p[2 * r:]


def _bd(x, head0):
    return jnp.concatenate([jnp.where(head0, x, 0.0), jnp.where(head0, 0.0, x)], axis=0)


def _wkv_constants(rev):
    c = CHUNK
    shift = int(math.log2(c))
    head0 = lax.broadcasted_iota(jnp.int32, (1, LANES), 1) < HEAD_SIZE
    row2 = lax.broadcasted_iota(jnp.int32, (2 * c, 2 * c), 0)
    col2 = lax.broadcasted_iota(jnp.int32, (2 * c, 2 * c), 1)
    same = (row2 >> shift) == (col2 >> shift)
    tt = row2 & (c - 1)
    ss = col2 & (c - 1)
    if rev:
        tt, ss = ss, tt
    ones_bd = jnp.where(same, 1.0, 0.0).astype(BF16)
    eye = jnp.where(row2 == col2, 1.0, 0.0).astype(F32)
    rowc = lax.broadcasted_iota(jnp.int32, (c, c), 0)
    colc = lax.broadcasted_iota(jnp.int32, (c, c), 1)
    tri = jnp.where(colc >= rowc if rev else colc <= rowc, 1.0, 0.0).astype(BF16)
    merge = tuple(((tt >> lv) == (ss >> lv) + 1) & ((tt >> (lv + 1)) == (ss >> (lv + 1))) for lv in range(shift))
    return head0, ones_bd, tri, ss < tt, ss <= tt, eye, merge


def _each(fn, *lists):
    return [fn(*args) for args in zip(*lists)]


def _mm(a, b):
    return jnp.dot(a.astype(BF16), b.astype(BF16), preferred_element_type=F32)


def _wkv_prepare(probs, k_a, hooks=()):
    hooks = list(hooks)

    def run_hook():
        if hooks:
            hooks.pop(0)()

    revs, rs, ks, kks, bd_vs, lwxs, axs, w0s, a0s, csts = [list(t) for t in zip(*probs)]
    head0 = csts[0][0]
    eye = csts[0][5]
    c = rs[0].shape[0]
    c2 = 2 * c
    logw = _each(lambda w0, lwx: -jnp.exp(-_softplus(-(w0 + lwx)) - 0.5), w0s, lwxs)
    a = _each(lambda a0, ax: jax.nn.sigmoid(a0 + ax), a0s, axs)
    kd = _each(lambda k, a_: k * (1.0 + (a_ - 1.0) * k_a), ks, a)
    b = _each(lambda kk, a_: kk * a_, kks, a)

    def cumsum(lw, cst):
        hi, mid, lo = _split3(lw)
        c3 = jnp.dot(cst[2], jnp.concatenate([hi, mid, lo], axis=1), preferred_element_type=F32)
        return c3[:, :LANES] + c3[:, LANES:2 * LANES] + c3[:, 2 * LANES:]

    cum = _each(cumsum, logw, csts)
    total = _each(lambda cm, rev: cm[0:1] if rev else cm[c - 1:c], cum, revs)
    kap_t = _each(lambda kk, cm, lw: kk * jnp.exp(cm - lw), kks, cum, logw)
    r_t = _each(lambda r, cm: r * jnp.exp(cm), rs, cum)
    igam = _each(lambda cm: jnp.exp(-cm), cum)
    tail = _each(lambda t, cm: jnp.exp(t - cm), total, cum)

    bd_kap = _each(lambda x: _bd(x, head0).astype(BF16), kap_t)
    lhs = _each(lambda kap, rt: jnp.concatenate([kap, _bd(rt, head0).astype(BF16)], axis=0), bd_kap, r_t)
    rhs = _each(lambda kd_, b_, ig: jnp.concatenate([_bd(kd_ * ig, head0), _bd(b_ * ig, head0)],
                                                    axis=0).astype(BF16), kd, b, igam)
    bct = _each(lambda b_, tl: _bd(b_ * tl, head0).T.astype(BF16), b, tail)
    kct = _each(lambda kd_, tl: _bd(kd_ * tl, head0).T.astype(BF16), kd, tail)
    gmat = _each(lambda l_, r_: lax.dot_general(l_, r_, (((1,), (1,)), ((), ())),
                                                preferred_element_type=F32), lhs, rhs)
    a_kb = _each(lambda g, cst: jnp.where(cst[3], g[:c2, c2:], 0.0), gmat, csts)
    a_rb = _each(lambda g, cst: jnp.where(cst[4], g[c2:, c2:], 0.0).astype(BF16), gmat, csts)
    a_kr = _each(lambda g, cst: jnp.concatenate([jnp.where(cst[3], g[:c2, :c2], 0.0),
                                                 jnp.where(cst[4], g[c2:, :c2], 0.0)], axis=0).astype(BF16),
                 gmat, csts)
    av = _each(_mm, a_kr, bd_vs)
    x0 = _each(lambda kap, av_: jnp.concatenate([kap, av_[:c2].astype(BF16)], axis=1), bd_kap, av)
    ov0 = _each(lambda av_: av_[c2:c2 + c] + av_[c2 + c:], av)
    run_hook()

    tinv = _each(lambda akb, cst: eye - jnp.where(cst[6][0], akb, 0.0), a_kb, csts)
    for lv in range(1, len(csts[0][6])):
        y = _each(lambda akb, x, cst: _mm(jnp.where(cst[6][lv], akb, 0.0), x), a_kb, tinv, csts)
        tinv = _each(lambda x, y_: x - _mm(x, y_), tinv, y)
        if lv % 2 == 0:
            run_hook()
    while hooks:
        run_hook()

    wu = _each(lambda x, x0_: _mm(x, x0_).astype(BF16), tinv, x0)
    rb = _each(_mm, a_rb, wu)
    mn = _each(_mm, bct, wu)
    kv = _each(_mm, kct, bd_vs)
    out = []
    for i in range(len(probs)):
        rk = _bd(r_t[i], head0) - rb[i][:, :LANES]
        ov = ov0[i] - (rb[i][:c, LANES:] + rb[i][c:, LANES:])
        m_mat = jnp.where(eye > 0.0, jnp.exp(total[i]), 0.0) - mn[i][:, :LANES]
        n_mat = kv[i] - mn[i][:, LANES:]
        out.append((jnp.concatenate([rk, m_mat], axis=0).astype(BF16), ov, n_mat))
    return out


def _wkv_apply(rm_lhs, ov, n_mat, h_state):
    c = ov.shape[0]
    rm = jnp.dot(rm_lhs, h_state.astype(BF16), preferred_element_type=F32)
    return rm[:c] + rm[c:2 * c] + ov, rm[2 * c:] + n_mat


def _wkv_kernel(r_ref, k_ref, v_ref, lw1_ref, a1_ref, g1_ref, wdec_ref, wa_ref, wg_ref,
                w0_ref, a0_ref, kk_ref, ka_ref, rk_ref, lnw_ref, lnb_ref,
                z_ref, of_scr, ob_scr, rm_scr, ov_scr, n_scr, *, n_ctx_chunks, n_chunks, unroll):
    c = CHUNK
    cst_f = _wkv_constants(False)
    cst_b = _wkv_constants(True)
    head0, ones_bd = cst_f[0], cst_f[1]

    w0f, w0b = w0_ref[0:1, :], w0_ref[1:2, :]
    a0f, a0b = a0_ref[0:1, :], a0_ref[1:2, :]
    k_k = kk_ref[...]
    k_a = ka_ref[...]

    def load(ref, rows):
        return ref[0, rows, :].astype(F32)

    def lora(x_ref, w_ref, rows, rev):
        d0 = int(rev) * LANES
        return jnp.dot(x_ref[0, rows, d0:d0 + LANES], w_ref[int(rev)], preferred_element_type=F32)

    def chunk_of(i, rev):
        if not rev:
            return i
        return jnp.where(i < n_ctx_chunks, n_ctx_chunks - 1 - i, n_chunks + n_ctx_chunks - 1 - i)

    def prepare_group(grp, hooks=()):
        probs, slots = [], []
        for u in range(unroll):
            for rev, w0, a0, cst in ((False, w0f, a0f, cst_f), (True, w0b, a0b, cst_b)):
                ci = chunk_of(grp * unroll + u, rev)
                rows = pl.ds(pl.multiple_of(ci * c, c), c)
                k = load(k_ref, rows)
                kk0 = k * k_k
                kk = kk0 * lax.rsqrt(jnp.maximum(_seg_sum(kk0 * kk0, ones_bd), 1e-24))
                bd_v = _bd(load(v_ref, rows), head0).astype(BF16)
                probs.append((rev, load(r_ref, rows), k, kk, bd_v, lora(lw1_ref, wdec_ref, rows, rev),
                              lora(a1_ref, wa_ref, rows, rev), w0, a0, cst))
                slots.append(ci + n_chunks * int(rev))
        for slot, (rm_lhs, ov, n_mat) in zip(slots, _wkv_prepare(probs, k_a, hooks)):
            rm_scr[slot] = rm_lhs
            ov_scr[slot] = ov
            n_scr[slot] = n_mat

    def state_steps(grp, state):
        def step(i):
            for rev, o_scr in ((False, of_scr), (True, ob_scr)):
                ci = chunk_of(i, rev)
                slot = ci + n_chunks * int(rev)
                o, state[int(rev)] = _wkv_apply(rm_scr[slot], ov_scr[slot], n_scr[slot], state[int(rev)])
                o_scr[pl.ds(pl.multiple_of(ci * c, c), c), :] = o
        return [functools.partial(step, grp * unroll + u) for u in range(unroll)]

    n_groups = n_chunks // unroll
    prepare_group(0)

    def body(grp, carry):
        state = list(carry)
        prepare_group(grp, state_steps(grp - 1, state))
        return tuple(state)

    zero = jnp.zeros((2 * c, LANES), F32)
    state = list(lax.fori_loop(1, n_groups, body, (zero, zero)))
    for step in state_steps(n_groups - 1, state):
        step()

    r_k = rk_ref[...]
    lnw = lnw_ref[...]
    lnb = lnb_ref[...]
    blk = TOKEN_TILE
    n_ctx = n_ctx_chunks * c
    inv_n = 1.0 / HEAD_SIZE

    n_blk = (n_chunks - n_ctx_chunks) * c // blk
    par = 2 if n_blk % 2 == 0 else 1

    def read_body(j, _):
        rows = [pl.ds(pl.multiple_of(n_ctx + (j * par + u) * blk, blk), blk) for u in range(par)]
        seg = lambda xs: [_seg_sum(x, ones_bd) for x in xs]
        a_f = [lora(a1_ref, wa_ref, rw, False) for rw in rows]
        a_b = [lora(a1_ref, wa_ref, rw, True) for rw in rows]
        gate = [jnp.dot(g1_ref[0, rw, :], wg_ref[...], preferred_element_type=F32) for rw in rows]
        o = [of_scr[rw, :] + ob_scr[rw, :] for rw in rows]
        dev = [o_ - m_ * inv_n for o_, m_ in zip(o, seg(o))]
        var = seg([d_ * d_ for d_ in dev])
        kd_sum = [load(k_ref, rw) * (2.0 + (jax.nn.sigmoid(a0f + af_) + jax.nn.sigmoid(a0b + ab_) - 2.0) * k_a)
                  for rw, af_, ab_ in zip(rows, a_f, a_b)]
        rkk = seg([load(r_ref, rw) * kd_ * r_k for rw, kd_ in zip(rows, kd_sum)])
        for u in range(par):
            on = dev[u] * lax.rsqrt(var[u] * inv_n + LNX_EPS) * lnw + lnb
            z = (on + rkk[u] * load(v_ref, rows[u])) * gate[u]
            z_ref[0, pl.ds(pl.multiple_of((j * par + u) * blk, blk), blk), :] = z.astype(z_ref.dtype)
        return 0

    lax.fori_loop(0, n_blk // par, read_body, 0)


def _wkv(r, k, v, lw1, a1, g1, w_dec, w_a, w_g, dec_w0, a0, k_k, k_a, r_k, lnx_w, lnx_b, n_ctx):
    b, s, d = r.shape
    l = s - n_ctx
    npair = d // LANES
    seq = lambda bi, p: (bi, 0, p)
    vec = lambda bi, p: (0, p)
    sblk = pl.BlockSpec((1, s, LANES), seq)
    rank_blk = pl.BlockSpec((1, s, 2 * LANES), lambda bi, p: (bi, 0, 0))
    up_blk = pl.BlockSpec((2, LANES, LANES), lambda bi, p: (0, 0, p))
    n_chunks = s // CHUNK
    unroll = next(u for u in (6, 4, 2, 1) if n_chunks % u == 0)
    scratch = [pltpu.VMEM((s, LANES), F32), pltpu.VMEM((s, LANES), F32),
               pltpu.VMEM((2 * n_chunks, 4 * CHUNK, LANES), BF16),
               pltpu.VMEM((2 * n_chunks, CHUNK, LANES), F32),
               pltpu.VMEM((2 * n_chunks, 2 * CHUNK, LANES), F32)]
    est = (2 * (3 * s * LANES * 2 + 2 * s * 2 * LANES * 2 + s * g1.shape[2] * 2 + l * LANES * 2) + 2 * s * LANES * 4
           + 2 * n_chunks * CHUNK * LANES * (4 * 2 + 4 + 2 * 4))
    return pl.pallas_call(
        functools.partial(_wkv_kernel, n_ctx_chunks=n_ctx // CHUNK, n_chunks=n_chunks, unroll=unroll),
        out_shape=jax.ShapeDtypeStruct((b, l, d), BF16),
        grid=(b, npair),
        in_specs=[sblk, sblk, sblk, rank_blk, rank_blk,
                  pl.BlockSpec((1, s, g1.shape[2]), lambda bi, p: (bi, 0, 0)), up_blk, up_blk,
                  pl.BlockSpec((g1.shape[2], LANES), lambda bi, p: (0, p)),
                  pl.BlockSpec((2, LANES), vec), pl.BlockSpec((2, LANES), vec),
                  pl.BlockSpec((1, LANES), vec), pl.BlockSpec((1, LANES), vec), pl.BlockSpec((1, LANES), vec),
                  pl.BlockSpec((1, LANES), vec), pl.BlockSpec((1, LANES), vec)],
        out_specs=pl.BlockSpec((1, l, LANES), seq),
        scratch_shapes=scratch,
        compiler_params=_cparams(("parallel", "parallel"), est),
        name="wkv_scan",
    )(r, k, v, lw1, a1, g1, w_dec, w_a, w_g, dec_w0, a0, k_k.reshape(1, d), k_a.reshape(1, d), r_k.reshape(1, d),
      lnx_w.reshape(1, d), lnx_b.reshape(1, d))


def _proj_post_kernel(a_ref, w_ref, b_ref, x_ref, g1_ref, g2_ref, gate_ref, sh_ref, sc_ref, xo_ref, h_ref):
    y = jnp.dot(a_ref[...], w_ref[...], preferred_element_type=F32) + b_ref[...]
    x = x_ref[...] + gate_ref[0] * (_rms(y) * g1_ref[...])
    xo_ref[...] = x
    h_ref[...] = _norm_mod(x, g2_ref[...], sh_ref[0], sc_ref[0]).astype(h_ref.dtype)


def _proj_post(a, w, bias, x2, mod3, layer, g1, g2, l, tm=512, name="proj_post"):
    m, k = a.shape
    d = w.shape[1]
    per_b = l // tm
    row = lambda j: pl.BlockSpec((1, 1, d), lambda i: (_mod_row(layer, i // per_b, j), 0, 0))
    tile = pl.BlockSpec((tm, d), lambda i: (i, 0))
    vec = pl.BlockSpec((1, d), lambda i: (0, 0))
    est = k * d * 2 + 2 * (tm * k * 2 + tm * d * (4 + 4 + 2)) + 2 * tm * d * 4
    return pl.pallas_call(
        _proj_post_kernel,
        out_shape=[jax.ShapeDtypeStruct((m, d), F32), jax.ShapeDtypeStruct((m, d), BF16)],
        grid=(m // tm,),
        in_specs=[pl.BlockSpec((tm, k), lambda i: (i, 0)),
                  pl.BlockSpec((k, d), lambda i: (0, 0), pipeline_mode=pl.Buffered(1)),
                  vec, tile, vec, vec, row(2), row(3), row(4)],
        out_specs=[tile, tile],
        compiler_params=_cparams(("parallel",), est),
        name=name,
    )(a, w, bias.reshape(1, d).astype(F32), x2, g1.reshape(1, d), g2.reshape(1, d), mod3, mod3, mod3)


def _mlp_kernel(h_ref, wu_ref, wd_ref, x_ref, g_ref, gate_ref, *rest, with_next):
    if with_next:
        gn_ref, shn_ref, scn_ref, o_ref, hn_ref, acc_ref = rest
    else:
        o_ref, acc_ref = rest
    kf = pl.program_id(1)

    @pl.when(kf == 0)
    def _():
        acc_ref[...] = jnp.zeros_like(acc_ref)

    u = jnp.dot(h_ref[...], wu_ref[...], preferred_element_type=F32)
    u = jnp.square(jnp.maximum(u, 0.0)).astype(BF16)
    acc_ref[...] += jnp.dot(u, wd_ref[...], preferred_element_type=F32)

    @pl.when(kf == pl.num_programs(1) - 1)
    def _():
        x = x_ref[...] + gate_ref[0] * (_rms(acc_ref[...]) * g_ref[...])
        o_ref[...] = x
        if with_next:
            hn_ref[...] = _norm_mod(x, gn_ref[...], shn_ref[0], scn_ref[0]).astype(hn_ref.dtype)


def _mlp(h2, w_up, w_down, x2, mod3, layer, g3, l, next_g0=None, tm=512, tf=1024):
    m, d = h2.shape
    dff = w_up.shape[1]
    per_b = l // tm
    with_next = next_g0 is not None
    row = lambda lay, j: pl.BlockSpec((1, 1, d), lambda i, f: (_mod_row(lay, i // per_b, j), 0, 0))
    tile = pl.BlockSpec((tm, d), lambda i, f: (i, 0))
    vec = pl.BlockSpec((1, d), lambda i, f: (0, 0))
    in_specs = [tile, pl.BlockSpec((d, tf), lambda i, f: (0, f)), pl.BlockSpec((tf, d), lambda i, f: (f, 0)),
                tile, vec, row(layer, 5)]
    args = [h2, w_up, w_down, x2, g3.reshape(1, d), mod3]
    out_shape = [jax.ShapeDtypeStruct((m, d), F32)]
    if with_next:
        in_specs += [vec, row(layer + 1, 0), row(layer + 1, 1)]
        args += [next_g0.reshape(1, d), mod3, mod3]
        out_shape.append(jax.ShapeDtypeStruct((m, d), BF16))
    est = 2 * (tm * d * 2 + 2 * d * tf * 2 + 2 * tm * d * 4 + tm * d * 2) + tm * d * 4 + tm * tf * 6
    outs = pl.pallas_call(
        functools.partial(_mlp_kernel, with_next=with_next),
        out_shape=out_shape,
        grid=(m // tm, dff // tf),
        in_specs=in_specs,
        out_specs=[tile] * len(out_shape),
        scratch_shapes=[pltpu.VMEM((tm, d), F32)],
        compiler_params=_cparams(("parallel", "arbitrary"), est),
        name="mlp",
    )(*args)
    return outs if with_next else (outs[0], None)


def _pre_norm_kernel(x_ref, g_ref, sh_ref, sc_ref, h_ref):
    h_ref[...] = _norm_mod(x_ref[...], g_ref[...], sh_ref[0], sc_ref[0]).astype(h_ref.dtype)


def _pre_norm(x2, mod3, layer, g0, l, tm=512, fixed_row=None, out_dtype=BF16):
    m, d = x2.shape
    per_b = l // tm
    if fixed_row is None:
        row = lambda j: pl.BlockSpec((1, 1, d), lambda i: (_mod_row(layer, i // per_b, j), 0, 0))
    else:
        row = lambda j: pl.BlockSpec((1, 1, d), lambda i: (_mod_row(layer, fixed_row, j), 0, 0))
    tile = pl.BlockSpec((tm, d), lambda i: (i, 0))
    return pl.pallas_call(
        _pre_norm_kernel,
        out_shape=jax.ShapeDtypeStruct((m, d), out_dtype),
        grid=(m // tm,),
        in_specs=[tile, pl.BlockSpec((1, d), lambda i: (0, 0)), row(0), row(1)],
        out_specs=tile,
        compiler_params=_cparams(("parallel",), 2 * tm * d * 6),
        name="pre_norm",
    )(x2, g0.reshape(1, d), mod3, mod3)


def _filter_kernel(z_ref, w1_ref, w2_ref, w3_ref, b_ref, fr_ref, w4_ref, t_ref, dl_ref, o_ref, hid_ref):
    hp = lax.Precision.HIGHEST

    @pl.when(pl.program_id(0) == 0)
    def _():
        b = b_ref[...]
        fr = fr_ref[...]
        z = jnp.sin(fr[0:1] * (jnp.dot(z_ref[...], w1_ref[...], precision=hp, preferred_element_type=F32) + b[0:1]))
        z = jnp.sin(fr[1:2] * (jnp.dot(z, w2_ref[...], precision=hp, preferred_element_type=F32) + b[1:2]))
        hid_ref[...] = jnp.sin(fr[2:3] * (jnp.dot(z, w3_ref[...], precision=hp, preferred_element_type=F32)
                                          + b[2:3]))

    filt = jnp.dot(hid_ref[...], w4_ref[...], precision=hp, preferred_element_type=F32)
    o_ref[...] = filt * jnp.exp(-t_ref[...] * dl_ref[...])


def _hyena_filters(l, d, f_w1, f_w23, f_w4, f_b, f_freq):
    t = jnp.linspace(0.0, 1.0, l, dtype=F32)[:, None]
    bands = (HY_EMB_DIM - 1) // 2
    freqs = jnp.linspace(1e-4, bands - 1, bands, dtype=F32)[None, :]
    ang = (2.0 * math.pi / l) * jnp.arange(l, dtype=F32)[:, None] * freqs
    z = jnp.concatenate([t, jnp.cos(ang), -jnp.sin(ang)], axis=-1)
    e, f = f_w1.shape
    pad = lambda a_, r, c: jnp.pad(a_.astype(F32), ((0, r - a_.shape[0]), (0, c - a_.shape[1])))
    zp = pad(z, l, LANES)
    w1 = pad(f_w1, LANES, LANES)
    w2 = pad(f_w23[0], LANES, LANES)
    w3 = pad(f_w23[1], LANES, LANES)
    bb = pad(f_b, 8, LANES)
    fr = pad(f_freq, 8, LANES)
    n = f_w4.shape[1]
    w4 = pad(f_w4, LANES, n)
    max_decay = math.log(HY_DECAY_TARGET) / HY_FAST_DECAY
    min_decay = math.log(HY_DECAY_TARGET) / HY_SLOW_DECAY
    deltas = jnp.abs(jnp.linspace(min_decay, max_decay, d, dtype=F32))[None, :]
    tn = _largest_tile(d, 1024, LANES)
    per_d = d // tn
    sq = pl.BlockSpec((LANES, LANES), lambda j: (0, 0))
    small = pl.BlockSpec((8, LANES), lambda j: (0, 0))
    return pl.pallas_call(
        _filter_kernel,
        out_shape=jax.ShapeDtypeStruct((l, n), F32),
        grid=(n // tn,),
        in_specs=[pl.BlockSpec((l, LANES), lambda j: (0, 0)), sq, sq, sq, small, small,
                  pl.BlockSpec((LANES, tn), lambda j: (0, j)),
                  pl.BlockSpec((l, 1), lambda j: (0, 0)),
                  pl.BlockSpec((1, tn), lambda j: (0, j % per_d))],
        out_specs=pl.BlockSpec((l, tn), lambda j: (0, j)),
        scratch_shapes=[pltpu.VMEM((l, LANES), F32)],
        compiler_params=_cparams(("arbitrary",), 4 * l * tn * 4),
        name="hyena_filters",
    )(zp, w1, w2, w3, bb, fr, w4, t, deltas)


def _dft_matrices(l):
    n = 2 * l
    k = jnp.arange(l, dtype=jnp.int32)[:, None]
    t = jnp.arange(l, dtype=jnp.int32)[None, :]
    ang = ((k * t) % n).astype(F32) * (2.0 * math.pi / n)
    cos = jnp.cos(ang)
    msin = jnp.where(k == 0, jnp.where(t % 2 == 0, 1.0, -1.0), -jnp.sin(ang))
    fwd = jnp.concatenate([cos, msin], axis=0)
    return fwd.astype(BF16), fwd.T.astype(BF16)


def _short_conv(z, cw, cb):
    n = z.shape[0]
    row = lax.broadcasted_iota(jnp.int32, (n, 1), 0)
    prev = jnp.where(row != 0, pltpu.roll(z, 1, 0), 0.0)
    nxt = jnp.where(row != n - 1, pltpu.roll(z, n - 1, 0), 0.0)
    return prev * cw[0:1] + z * cw[1:2] + nxt * cw[2:3] + cb


def _spec_kernel(f_ref, hf_ref, hb_ref, o_ref):
    l, tn = hf_ref.shape
    row = lax.broadcasted_iota(jnp.int32, (l, 1), 0)
    hb = jnp.where(row == 0, 0.0, hb_ref[...])
    h2 = jnp.concatenate([hf_ref[...], hb], axis=1).astype(BF16)
    u = jnp.dot(f_ref[...], h2, preferred_element_type=F32)
    row2 = lax.broadcasted_iota(jnp.int32, (2 * l, 1), 0)
    sign = jnp.where(row2 > l, -1.0, 1.0)
    scale = jnp.where((row2 == 0) | (row2 == l), 1.0 / (2 * l), 2.0 / (2 * l))
    o_ref[0] = (u[:, :tn] + sign * u[:, tn:]) * scale


def _spectrum(fwd, filt, d, tn=256):
    l = filt.shape[0]
    n2 = fwd.shape[0]
    orders = filt.shape[1] // (2 * d)
    per = d // tn
    return pl.pallas_call(
        _spec_kernel,
        out_shape=jax.ShapeDtypeStruct((orders, n2, d), F32),
        grid=(orders, per),
        in_specs=[pl.BlockSpec((n2, l), lambda o, j: (0, 0), pipeline_mode=pl.Buffered(1)),
                  pl.BlockSpec((l, tn), lambda o, j: (0, o * per + j)),
                  pl.BlockSpec((l, tn), lambda o, j: (0, (orders + o) * per + j))],
        out_specs=pl.BlockSpec((1, n2, tn), lambda o, j: (o, 0, j)),
        compiler_params=_cparams(("parallel", "parallel"), n2 * l * 2 + 2 * (2 * l * tn * 4 + n2 * tn * 4)
                                 + 3 * n2 * tn * 4),
        name="filter_spectrum",
    )(fwd, filt, filt)


def _conv_fwd_kernel(f_ref, y_ref, k_ref, cw_ref, cb_ref, p_ref, *, short_conv, n_split):
    l = y_ref.shape[1]
    y = y_ref[0]
    if short_conv:
        y = _short_conv(y, cw_ref[...], cb_ref[...])
    yb = y.astype(BF16)
    rows = l // n_split
    for s in range(n_split):
        lo, hi = s * rows, (s + 1) * rows
        ure = jnp.dot(f_ref[lo:hi, :], yb, preferred_element_type=F32)
        uim = jnp.dot(f_ref[l + lo:l + hi, :], yb, preferred_element_type=F32)
        kre = k_ref[0, lo:hi, :]
        kim = k_ref[0, l + lo:l + hi, :]
        pre = ure * kre - uim * kim
        pim = ure * kim + uim * kre
        if s == 0:
            first = lax.broadcasted_iota(jnp.int32, (rows, 1), 0) == 0
            pre = jnp.where(first, ure * kre, pre)
            pim = jnp.where(first, uim * kim, pim)
        p_ref[0, lo:hi, :] = pre.astype(p_ref.dtype)
        p_ref[0, l + lo:l + hi, :] = pim.astype(p_ref.dtype)


def _conv_fwd(fwd, y, y_col0, kspec, order, conv_w, conv_b, short_conv, d, tn=256):
    b, l, _ = y.shape
    n2 = fwd.shape[0]
    off = y_col0 // tn
    est = n2 * l * 2 + 2 * (l * tn * 4 + n2 * tn * 4 + n2 * tn * 2) + 6 * l * tn * 4
    return pl.pallas_call(
        functools.partial(_conv_fwd_kernel, short_conv=short_conv, n_split=2),
        out_shape=jax.ShapeDtypeStruct((b, n2, d), BF16),
        grid=(d // tn, b),
        in_specs=[pl.BlockSpec((n2, l), lambda j, bi: (0, 0), pipeline_mode=pl.Buffered(1)),
                  pl.BlockSpec((1, l, tn), lambda j, bi: (bi, 0, j + off)),
                  pl.BlockSpec((1, n2, tn), lambda j, bi: (order, 0, j)),
                  pl.BlockSpec((3, tn), lambda j, bi: (0, j + off)),
                  pl.BlockSpec((1, tn), lambda j, bi: (0, j + off))],
        out_specs=pl.BlockSpec((1, n2, tn), lambda j, bi: (bi, 0, j)),
        compiler_params=_cparams(("parallel", "parallel"), est),
        name="hyena_conv_fwd",
    )(fwd, y, kspec, conv_w, conv_b)


def _conv_inv_kernel(ft_ref, p_ref, yp_ref, gt_ref, cwy_ref, cby_ref, cwg_ref, cbg_ref, sk_ref, o_ref,
                     *, short_conv_prev):
    conv = jnp.dot(ft_ref[...], p_ref[0], preferred_element_type=F32)
    yp = yp_ref[0]
    if short_conv_prev:
        yp = _short_conv(yp, cwy_ref[...], cby_ref[...])
    gate = _short_conv(gt_ref[0], cwg_ref[...], cbg_ref[...])
    o_ref[0] = (gate * (conv + sk_ref[...] * yp)).astype(o_ref.dtype)


def _conv_inv(finv, p, yprev, yprev_col0, short_conv_prev, z, gate_col0, conv_w, conv_b, skip, out_dtype, tn=256):
    b, n2, d = p.shape
    l = n2 // 2
    offy = yprev_col0 // tn
    offg = gate_col0 // tn
    est = l * n2 * 2 + 2 * (n2 * tn * 2 + 3 * l * tn * 4) + 6 * l * tn * 4
    return pl.pallas_call(
        functools.partial(_conv_inv_kernel, short_conv_prev=short_conv_prev),
        out_shape=jax.ShapeDtypeStruct((b, l, d), out_dtype),
        grid=(d // tn, b),
        in_specs=[pl.BlockSpec((l, n2), lambda j, bi: (0, 0), pipeline_mode=pl.Buffered(1)),
                  pl.BlockSpec((1, n2, tn), lambda j, bi: (bi, 0, j)),
                  pl.BlockSpec((1, l, tn), lambda j, bi: (bi, 0, j + offy)),
                  pl.BlockSpec((1, l, tn), lambda j, bi: (bi, 0, j + offg)),
                  pl.BlockSpec((3, tn), lambda j, bi: (0, j + offy)),
                  pl.BlockSpec((1, tn), lambda j, bi: (0, j + offy)),
                  pl.BlockSpec((3, tn), lambda j, bi: (0, j + offg)),
                  pl.BlockSpec((1, tn), lambda j, bi: (0, j + offg)),
                  pl.BlockSpec((1, tn), lambda j, bi: (0, j))],
        out_specs=pl.BlockSpec((1, l, tn), lambda j, bi: (bi, 0, j)),
        compiler_params=_cparams(("parallel", "parallel"), est),
        name="hyena_conv_inv",
    )(finv, p, yprev, z, conv_w, conv_b, conv_w, conv_b, skip)


def _hyena_mixer(h2, b, l, in_w, in_b, conv_w, conv_b, f_w1, f_w23, f_w4, f_b, f_freq, skip, out_w, out_b):
    d = h2.shape[1]
    z = _matmul(h2, in_w.astype(BF16), bias=in_b, name="hyena_in").reshape(b, l, 3 * d)
    filt = _hyena_filters(l, d, f_w1, f_w23, f_w4, f_b, f_freq)
    fwd, finv = _dft_matrices(l)
    kspec = _spectrum(fwd, filt, d)
    skip = skip.astype(F32)
    p0 = _conv_fwd(fwd, z, 0, kspec, 0, conv_w, conv_b.reshape(1, -1), True, d)
    y1 = _conv_inv(finv, p0, z, 0, True, z, d, conv_w, conv_b.reshape(1, -1), skip[0:1], F32)
    p1 = _conv_fwd(fwd, y1, 0, kspec, 1, conv_w, conv_b.reshape(1, -1), False, d)
    y2 = _conv_inv(finv, p1, y1, 0, False, z, 2 * d, conv_w, conv_b.reshape(1, -1), skip[1:2], BF16)
    return y2.reshape(b * l, d), out_w.astype(BF16), out_b


def _lora_in(w):
    pad = lambda m: jnp.pad(m, ((0, 0), (0, LANES - m.shape[1])))
    return jnp.concatenate([pad(w[0]), pad(w[1])], axis=1)


def _lora_out(w):
    return jnp.pad(w, ((0, 0), (0, LANES - w.shape[1]), (0, 0)))


def _rwkv_mixer(x, ctx, mod3, layer, g0, mu, w_r, w_k, w_v, w_o, dec_w0, dec_w1, dec_w2, a0, a1, a2,
                g1, g2, k_k, k_a, r_k, lnx_w, lnx_b):
    b, l, d = x.shape
    n_ctx = ctx.shape[1]
    s = n_ctx + l
    bf = lambda w: w.astype(BF16)
    assert dec_w1.shape[2] <= LANES and a1.shape[2] <= LANES
    xr, xk, xv, lw1, a1o, g1o = _pre_rwkv(x, ctx, mod3, layer, g0, mu, bf(_lora_in(dec_w1)), bf(_lora_in(a1)),
                                          bf(g1))
    r = _matmul(xr.reshape(b * s, d), bf(w_r), out_dtype=BF16, name="rwkv_r").reshape(b, s, d)
    k = _matmul(xk.reshape(b * s, d), bf(w_k), out_dtype=BF16, name="rwkv_k").reshape(b, s, d)
    v = _matmul(xv.reshape(b * s, d), bf(w_v), out_dtype=BF16, name="rwkv_v").reshape(b, s, d)
    zz = _wkv(r, k, v, lw1, a1o, g1o, bf(_lora_out(dec_w2)), bf(_lora_out(a2)), bf(g2), dec_w0, a0, k_k, k_a, r_k,
              lnx_w, lnx_b, n_ctx)
    return zz.reshape(b * l, d), bf(w_o), jnp.zeros((d,), F32)


def kernel(x, c, ctx, c_ctx, ada_w, ada_b, norm_g, mlp_up, mlp_down, rw_mu, rw_w_r, rw_w_k, rw_w_v, rw_w_o, rw_dec_w0, rw_dec_w1, rw_dec_w2, rw_a0, rw_a1, rw_a2, rw_g1, rw_g2, rw_k_k, rw_k_a, rw_r_k, rw_lnx_w, rw_lnx_b, hy_in_w, hy_in_b, hy_conv_w, hy_conv_b, hy_f_w1, hy_f_w23, hy_f_w4, hy_f_b, hy_f_freq, hy_skip, hy_out_w, hy_out_b):
    b, l, d = x.shape
    depth = ada_w.shape[0]
    assert b < MOD_ROWS
    c_rows = jnp.zeros((MOD_ROWS, d), F32).at[:b].set(c).at[b].set(c_ctx)
    mod3 = _ada_mod(c_rows, ada_w, ada_b)
    x2 = x.reshape(b * l, d)
    xc = ctx
    h_next = None
    for i in range(depth):
        kind, j = i % 2, i // 2
        ctx_live = any(q % 2 == 0 for q in range(i + 1, depth))
        assert not ctx_live, "context-stream update is not implemented for this depth"
        if kind == 0:
            mixed = _rwkv_mixer(x2.reshape(b, l, d), xc, mod3, i, norm_g[i, 0], rw_mu[j], rw_w_r[j], rw_w_k[j],
                                rw_w_v[j], rw_w_o[j], rw_dec_w0[j], rw_dec_w1[j], rw_dec_w2[j], rw_a0[j],
                                rw_a1[j], rw_a2[j], rw_g1[j], rw_g2[j], rw_k_k[j], rw_k_a[j], rw_r_k[j],
                                rw_lnx_w[j], rw_lnx_b[j])
        else:
            h = h_next if h_next is not None else _pre_norm(x2, mod3, i, norm_g[i, 0], l)
            mixed = _hyena_mixer(h, b, l, hy_in_w[j], hy_in_b[j], hy_conv_w[j], hy_conv_b[j], hy_f_w1[j],
                                 hy_f_w23[j], hy_f_w4[j], hy_f_b[j], hy_f_freq[j], hy_skip[j], hy_out_w[j],
                                 hy_out_b[j])
        x2, h2 = _proj_post(*mixed, x2, mod3, i, norm_g[i, 1], norm_g[i, 2], l,
                            name="rwkv_out_post" if kind == 0 else "hyena_out_post")
        next_is_hyena = i + 1 < depth and (i + 1) % 2 == 1
        x2, h_next = _mlp(h2, mlp_up[i].astype(BF16), mlp_down[i].astype(BF16), x2, mod3, i, norm_g[i, 3], l,
                          next_g0=norm_g[i + 1, 0] if next_is_hyena else None)
    return x2.reshape(b, l, d)
```

```python
import functools
import math

import jax
import jax.numpy as jnp
from jax import lax
from jax.experimental import pallas as pl
from jax.experimental.pallas import tpu as pltpu

F32 = jnp.float32
BF16 = jnp.bfloat16

HEAD_SIZE = 64
GRID_W = 64
N_MOD = 6
NORM_EPS = 1e-6
LNX_EPS = 64e-5
HY_FAST_DECAY = 0.3
HY_SLOW_DECAY = 1.5
HY_DECAY_TARGET = 1e-2
HY_EMB_DIM = 33
MOD_ROWS = 16
CHUNK = 64
LANES = 128
TOKEN_TILE = 256
VMEM_LIMIT_CAP = 60000 * 1024


def _cparams(sem, est_bytes):
    limit = int(min(max(2 * est_bytes, 32 * 1024 * 1024), VMEM_LIMIT_CAP))
    return pltpu.CompilerParams(dimension_semantics=sem, vmem_limit_bytes=limit)


def _largest_tile(n, cap, align):
    t = min(cap, n) // align * align
    while t > align and n % t:
        t -= align
    assert t > 0 and n % t == 0, (n, cap, align)
    return t


def _split3(x):
    hi = x.astype(BF16)
    r1 = x - hi.astype(F32)
    mid = r1.astype(BF16)
    lo = (r1 - mid.astype(F32)).astype(BF16)
    return hi, mid, lo


def _rms(x):
    return x * lax.rsqrt(jnp.mean(x * x, axis=-1, keepdims=True) + NORM_EPS)


def _norm_mod(x, g, shift, scale):
    return (_rms(x) * g) * (1.0 + scale) + shift


def _softplus(y):
    return jnp.maximum(y, 0.0) + jnp.log1p(jnp.exp(-jnp.abs(y)))


def _ada_kernel(c_ref, w_ref, b_ref, o_ref):
    c = c_ref[...]
    s = c * jax.nn.sigmoid(c)
    s_hi = s.astype(BF16)
    s_lo = (s - s_hi.astype(F32)).astype(BF16)
    w = w_ref[0]
    w_hi = w.astype(BF16)
    w_lo = (w - w_hi.astype(F32)).astype(BF16)
    p = jnp.dot(jnp.concatenate([s_hi, s_lo], axis=0), w_hi, preferred_element_type=F32)
    q = jnp.dot(s_hi, w_lo, preferred_element_type=F32)
    o_ref[0] = p[:MOD_ROWS] + p[MOD_ROWS:] + q + b_ref[0]


def _ada_mod(c_rows, ada_w, ada_b):
    depth, d, n = ada_w.shape
    tn = 1024
    out = pl.pallas_call(
        _ada_kernel,
        out_shape=jax.ShapeDtypeStruct((depth, MOD_ROWS, n), F32),
        grid=(depth, n // tn),
        in_specs=[pl.BlockSpec((MOD_ROWS, d), lambda l, j: (0, 0)),
                  pl.BlockSpec((1, d, tn), lambda l, j: (l, 0, j)),
                  pl.BlockSpec((1, 1, tn), lambda l, j: (l, 0, j))],
        out_specs=pl.BlockSpec((1, MOD_ROWS, tn), lambda l, j: (l, 0, j)),
        compiler_params=_cparams(("parallel", "parallel"), 2 * d * tn * 4 + 4 * d * tn),
        name="ada_mod",
    )(c_rows, ada_w, ada_b.reshape(depth, 1, n))
    return out.reshape(depth * MOD_ROWS * N_MOD, 1, d)


def _mod_row(layer, row, j):
    return (layer * MOD_ROWS + row) * N_MOD + j


def _pre_rwkv_kernel(hc_ref, sc_ref_ctx, xm_ref, xu_ref, xd_ref, g_ref, sh_ref, sc_ref, mu_ref,
                     wdec_ref, wa_ref, wg_ref, xr_ref, xk_ref, xv_ref, lw1_ref, a1_ref, g1_ref,
                     mw_scr, ma_scr, mg_scr, *, n_tiles):
    t = pl.program_id(1)
    d = xm_ref.shape[-1]
    g = g_ref[...]
    mu = mu_ref[...]
    big = {0: xr_ref, 2: xk_ref, 3: xv_ref}
    small = {1: mw_scr, 4: ma_scr, 5: mg_scr}

    def emit(h, s, lo, hi):
        xx = s - h
        for j in range(6):
            mix = (h + xx * mu[j:j + 1, lo:hi]).astype(BF16)
            if j in big:
                big[j][0, :, lo:hi] = mix
            else:
                small[j][:, lo:hi] = mix

    @pl.when(t == 0)
    def _():
        emit(hc_ref[0], sc_ref_ctx[0], 0, d)

    @pl.when(t > 0)
    def _():
        sh = sh_ref[0]
        sc = sc_ref[0]
        xm = xm_ref[0].reshape(TOKEN_TILE, d)
        hm = _norm_mod(xm, g, sh, sc)
        hu = _norm_mod(xu_ref[0, 0], g, sh, sc) * jnp.where(t > 1, 1.0, 0.0)
        hd = _norm_mod(xd_ref[0, 0], g, sh, sc) * jnp.where(t < n_tiles, 1.0, 0.0)
        col = lax.broadcasted_iota(jnp.int32, (TOKEN_TILE, 1), 0) & (GRID_W - 1)
        q = d // 4
        left = jnp.where(col != 0, pltpu.roll(hm[:, :q], 1, 0), 0.0)
        right = jnp.where(col != GRID_W - 1, pltpu.roll(hm[:, q:2 * q], TOKEN_TILE - 1, 0), 0.0)
        up = jnp.concatenate([hu[:, 2 * q:3 * q], hm[:TOKEN_TILE - GRID_W, 2 * q:3 * q]], axis=0)
        down = jnp.concatenate([hm[GRID_W:, 3 * q:], hd[:, 3 * q:]], axis=0)
        emit(hm[:, :q], left, 0, q)
        emit(hm[:, q:2 * q], right, q, 2 * q)
        emit(hm[:, 2 * q:3 * q], up, 2 * q, 3 * q)
        emit(hm[:, 3 * q:], down, 3 * q, d)

    lw1_ref[0] = jnp.tanh(jnp.dot(mw_scr[...], wdec_ref[...], preferred_element_type=F32)).astype(BF16)
    a1_ref[0] = jnp.dot(ma_scr[...], wa_ref[...], preferred_element_type=F32).astype(BF16)
    g1_ref[0] = jax.nn.sigmoid(jnp.dot(mg_scr[...], wg_ref[...], preferred_element_type=F32)).astype(BF16)


def _shift_seq(x):
    half = x.shape[-1] // 2
    p = jnp.pad(x, ((0, 0), (1, 1), (0, 0)))
    return jnp.concatenate([p[:, :-2, :half], p[:, 2:, half:]], axis=-1)


def _pre_rwkv(x, ctx, mod3, layer, g0, mu, w_dec1, w_a1, w_g1):
    b, l, d = x.shape
    ranks = (w_dec1.shape[1], w_a1.shape[1], w_g1.shape[1])
    n_ctx = ctx.shape[1]
    assert n_ctx == TOKEN_TILE and l % TOKEN_TILE == 0
    n_tiles = l // TOKEN_TILE
    rows_per_tile = TOKEN_TILE // GRID_W
    n_rows = l // GRID_W
    x4 = x.reshape(b, n_rows, GRID_W, d)
    s = n_ctx + l
    hc = _pre_norm(ctx.reshape(b * n_ctx, d), mod3, layer, g0, n_ctx, tm=n_ctx, fixed_row=b,
                   out_dtype=F32).reshape(b, n_ctx, d)
    sc = _shift_seq(hc)

    def mrow(j):
        return pl.BlockSpec((1, 1, d), lambda bi, t: (_mod_row(layer, bi, j), 0, 0))

    main = lambda bi, t: (bi, jnp.maximum(t - 1, 0), 0, 0)
    up = lambda bi, t: (bi, jnp.maximum((t - 1) * rows_per_tile - 1, 0), 0, 0)
    down = lambda bi, t: (bi, jnp.minimum(jnp.maximum(t, 1) * rows_per_tile, n_rows - 1), 0, 0)
    whole = lambda w: pl.BlockSpec(w.shape, lambda bi, t: (0, 0))
    outs = pl.pallas_call(
        functools.partial(_pre_rwkv_kernel, n_tiles=n_tiles),
        out_shape=[jax.ShapeDtypeStruct((b, s, d), BF16)] * 3
                  + [jax.ShapeDtypeStruct((b, s, rk), BF16) for rk in ranks],
        grid=(b, n_tiles + 1),
        in_specs=[pl.BlockSpec((1, n_ctx, d), lambda bi, t: (bi, 0, 0)),
                  pl.BlockSpec((1, n_ctx, d), lambda bi, t: (bi, 0, 0)),
                  pl.BlockSpec((1, rows_per_tile, GRID_W, d), main),
                  pl.BlockSpec((1, 1, GRID_W, d), up),
                  pl.BlockSpec((1, 1, GRID_W, d), down),
                  pl.BlockSpec((1, d), lambda bi, t: (0, 0)),
                  mrow(0), mrow(1),
                  pl.BlockSpec((6, d), lambda bi, t: (0, 0)),
                  whole(w_dec1), whole(w_a1), whole(w_g1)],
        out_specs=[pl.BlockSpec((1, TOKEN_TILE, d), lambda bi, t: (bi, t, 0))] * 3
                  + [pl.BlockSpec((1, TOKEN_TILE, rk), lambda bi, t: (bi, t, 0)) for rk in ranks],
        scratch_shapes=[pltpu.VMEM((TOKEN_TILE, d), BF16)] * 3,
        compiler_params=_cparams(("parallel", "arbitrary"),
                                 2 * (3 * TOKEN_TILE * d * 4 + 2 * GRID_W * d * 4 + 3 * TOKEN_TILE * d * 2
                                      + d * sum(ranks) * 2) + 3 * TOKEN_TILE * d * 2),
        name="rwkv_pre",
    )(hc, sc, x4, x4, x4, g0.reshape(1, d), mod3, mod3, mu, w_dec1, w_a1, w_g1)
    return outs


def _mm_kernel(a_ref, w_ref, *rest, has_bias):
    o_ref = rest[-1]
    acc = jnp.dot(a_ref[...], w_ref[...], preferred_element_type=F32)
    if has_bias:
        acc = acc + rest[0][...]
    o_ref[...] = acc.astype(o_ref.dtype)


def _matmul(a, w, bias=None, out_dtype=F32, tm=1024, tn=1024, name="matmul"):
    m, k = a.shape
    n = w.shape[1]
    tm = _largest_tile(m, tm, 8)
    tn = _largest_tile(n, tn, LANES)
    in_specs = [pl.BlockSpec((tm, k), lambda i, j: (i, 0)),
                pl.BlockSpec((k, tn), lambda i, j: (0, j))]
    args = [a, w]
    if bias is not None:
        in_specs.append(pl.BlockSpec((1, tn), lambda i, j: (0, j)))
        args.append(bias.reshape(1, n).astype(F32))
    est = 2 * (tm * k * 2 + k * tn * 2 + tm * tn * jnp.dtype(out_dtype).itemsize) + tm * tn * 4
    return pl.pallas_call(
        functools.partial(_mm_kernel, has_bias=bias is not None),
        out_shape=jax.ShapeDtypeStruct((m, n), out_dtype),
        grid=(m // tm, n // tn),
        in_specs=in_specs,
        out_specs=pl.BlockSpec((tm, tn), lambda i, j: (i, j)),
        compiler_params=_cparams(("parallel", "parallel"), est),
        name=name,
    )(*args)


def _seg_sum(x, ones_bd):
    hi, mid, lo = _split3(x)
    r = x.shape[0]
    p = jnp.dot(jnp.concatenate([hi, mid, lo], axis=0), ones_bd, preferred_element_type=F32)
    return p[:r] + p[r:2 * r] + p[2 * r:]


def _bd(x, head0):
    return jnp.concatenate([jnp.where(head0, x, 0.0), jnp.where(head0, 0.0, x)], axis=0)


def _wkv_constants(rev):
    c = CHUNK
    shift = int(math.log2(c))
    head0 = lax.broadcasted_iota(jnp.int32, (1, LANES), 1) < HEAD_SIZE
    row2 = lax.broadcasted_iota(jnp.int32, (2 * c, 2 * c), 0)
    col2 = lax.broadcasted_iota(jnp.int32, (2 * c, 2 * c), 1)
    same = (row2 >> shift) == (col2 >> shift)
    tt = row2 & (c - 1)
    ss = col2 & (c - 1)
    if rev:
        tt, ss = ss, tt
    ones_bd = jnp.where(same, 1.0, 0.0).astype(BF16)
    eye = jnp.where(row2 == col2, 1.0, 0.0).astype(F32)
    rowc = lax.broadcasted_iota(jnp.int32, (c, c), 0)
    colc = lax.broadcasted_iota(jnp.int32, (c, c), 1)
    tri = jnp.where(colc >= rowc if rev else colc <= rowc, 1.0, 0.0).astype(BF16)
    merge = tuple(((tt >> lv) == (ss >> lv) + 1) & ((tt >> (lv + 1)) == (ss >> (lv + 1))) for lv in range(shift))
    return head0, ones_bd, tri, ss < tt, ss <= tt, eye, merge


def _each(fn, *lists):
    return [fn(*args) for args in zip(*lists)]


def _mm(a, b):
    return jnp.dot(a.astype(BF16), b.astype(BF16), preferred_element_type=F32)


def _wkv_prepare(probs, k_a, hooks=()):
    hooks = list(hooks)

    def run_hook():
        if hooks:
            hooks.pop(0)()

    revs, rs, ks, kks, bd_vs, lwxs, axs, w0s, a0s, csts = [list(t) for t in zip(*probs)]
    head0 = csts[0][0]
    eye = csts[0][5]
    c = rs[0].shape[0]
    c2 = 2 * c
    logw = _each(lambda w0, lwx: -jnp.exp(-_softplus(-(w0 + lwx)) - 0.5), w0s, lwxs)
    a = _each(lambda a0, ax: jax.nn.sigmoid(a0 + ax), a0s, axs)
    kd = _each(lambda k, a_: k * (1.0 + (a_ - 1.0) * k_a), ks, a)
    b = _each(lambda kk, a_: kk * a_, kks, a)

    def cumsum(lw, cst):
        hi, mid, lo = _split3(lw)
        c3 = jnp.dot(cst[2], jnp.concatenate([hi, mid, lo], axis=1), preferred_element_type=F32)
        return c3[:, :LANES] + c3[:, LANES:2 * LANES] + c3[:, 2 * LANES:]

    cum = _each(cumsum, logw, csts)
    total = _each(lambda cm, rev: cm[0:1] if rev else cm[c - 1:c], cum, revs)
    kap_t = _each(lambda kk, cm, lw: kk * jnp.exp(cm - lw), kks, cum, logw)
    r_t = _each(lambda r, cm: r * jnp.exp(cm), rs, cum)
    igam = _each(lambda cm: jnp.exp(-cm), cum)
    tail = _each(lambda t, cm: jnp.exp(t - cm), total, cum)

    bd_kap = _each(lambda x: _bd(x, head0).astype(BF16), kap_t)
    lhs = _each(lambda kap, rt: jnp.concatenate([kap, _bd(rt, head0).astype(BF16)], axis=0), bd_kap, r_t)
    rhs = _each(lambda kd_, b_, ig: jnp.concatenate([_bd(kd_ * ig, head0), _bd(b_ * ig, head0)],
                                                    axis=0).astype(BF16), kd, b, igam)
    bct = _each(lambda b_, tl: _bd(b_ * tl, head0).T.astype(BF16), b, tail)
    kct = _each(lambda kd_, tl: _bd(kd_ * tl, head0).T.astype(BF16), kd, tail)
    gmat = _each(lambda l_, r_: lax.dot_general(l_, r_, (((1,), (1,)), ((), ())),
                                                preferred_element_type=F32), lhs, rhs)
    a_kb = _each(lambda g, cst: jnp.where(cst[3], g[:c2, c2:], 0.0), gmat, csts)
    a_rb = _each(lambda g, cst: jnp.where(cst[4], g[c2:, c2:], 0.0).astype(BF16), gmat, csts)
    a_kr = _each(lambda g, cst: jnp.concatenate([jnp.where(cst[3], g[:c2, :c2], 0.0),
                                                 jnp.where(cst[4], g[c2:, :c2], 0.0)], axis=0).astype(BF16),
                 gmat, csts)
    av = _each(_mm, a_kr, bd_vs)
    x0 = _each(lambda kap, av_: jnp.concatenate([kap, av_[:c2].astype(BF16)], axis=1), bd_kap, av)
    ov0 = _each(lambda av_: av_[c2:c2 + c] + av_[c2 + c:], av)
    run_hook()

    tinv = _each(lambda akb, cst: eye - jnp.where(cst[6][0], akb, 0.0), a_kb, csts)
    for lv in range(1, len(csts[0][6])):
        y = _each(lambda akb, x, cst: _mm(jnp.where(cst[6][lv], akb, 0.0), x), a_kb, tinv, csts)
        run_hook()
        tinv = _each(lambda x, y_: x - _mm(x, y_), tinv, y)
        run_hook()

    wu = _each(lambda x, x0_: _mm(x, x0_).astype(BF16), tinv, x0)
    run_hook()
    rb = _each(_mm, a_rb, wu)
    run_hook()
    mn = _each(_mm, bct, wu)
    while hooks:
        run_hook()
    kv = _each(_mm, kct, bd_vs)
    out = []
    for i in range(len(probs)):
        rk = _bd(r_t[i], head0) - rb[i][:, :LANES]
        ov = ov0[i] - (rb[i][:c, LANES:] + rb[i][c:, LANES:])
        m_mat = jnp.where(eye > 0.0, jnp.exp(total[i]), 0.0) - mn[i][:, :LANES]
        n_mat = kv[i] - mn[i][:, LANES:]
        out.append((jnp.concatenate([rk, m_mat], axis=0).astype(BF16), ov, n_mat))
    return out


def _wkv_apply(rm_lhs, ov, n_mat, h_state):
    c = ov.shape[0]
    rm = jnp.dot(rm_lhs, h_state.astype(BF16), preferred_element_type=F32)
    return rm[:c] + rm[c:2 * c] + ov, rm[2 * c:] + n_mat


def _wkv_kernel(r_ref, k_ref, v_ref, lw1_ref, a1_ref, g1_ref, wdec_ref, wa_ref, wg_ref,
                w0_ref, a0_ref, kk_ref, ka_ref, rk_ref, lnw_ref, lnb_ref,
                z_ref, of_scr, ob_scr, rm_scr, ov_scr, n_scr, *, n_ctx_chunks, n_chunks, unroll):
    c = CHUNK
    cst_f = _wkv_constants(False)
    cst_b = _wkv_constants(True)
    head0, ones_bd = cst_f[0], cst_f[1]

    w0f, w0b = w0_ref[0:1, :], w0_ref[1:2, :]
    a0f, a0b = a0_ref[0:1, :], a0_ref[1:2, :]
    k_k = kk_ref[...]
    k_a = ka_ref[...]

    def load(ref, rows):
        return ref[0, rows, :].astype(F32)

    def lora(x_ref, w_ref, rows, rev):
        d0 = int(rev) * LANES
        return jnp.dot(x_ref[0, rows, d0:d0 + LANES], w_ref[int(rev)], preferred_element_type=F32)

    def chunk_of(i, rev):
        if not rev:
            return i
        return jnp.where(i < n_ctx_chunks, n_ctx_chunks - 1 - i, n_chunks + n_ctx_chunks - 1 - i)

    def prepare_group(grp, hooks=()):
        probs, slots = [], []
        for u in range(unroll):
            for rev, w0, a0, cst in ((False, w0f, a0f, cst_f), (True, w0b, a0b, cst_b)):
                ci = chunk_of(grp * unroll + u, rev)
                rows = pl.ds(pl.multiple_of(ci * c, c), c)
                k = load(k_ref, rows)
                kk0 = k * k_k
                kk = kk0 * lax.rsqrt(jnp.maximum(_seg_sum(kk0 * kk0, ones_bd), 1e-24))
                bd_v = _bd(load(v_ref, rows), head0).astype(BF16)
                probs.append((rev, load(r_ref, rows), k, kk, bd_v, lora(lw1_ref, wdec_ref, rows, rev),
                              lora(a1_ref, wa_ref, rows, rev), w0, a0, cst))
                slots.append(ci + n_chunks * int(rev))
        for slot, (rm_lhs, ov, n_mat) in zip(slots, _wkv_prepare(probs, k_a, hooks)):
            rm_scr[slot] = rm_lhs
            ov_scr[slot] = ov
            n_scr[slot] = n_mat

    def state_steps(grp, state):
        def step(i):
            for rev, o_scr in ((False, of_scr), (True, ob_scr)):
                ci = chunk_of(i, rev)
                slot = ci + n_chunks * int(rev)
                o, state[int(rev)] = _wkv_apply(rm_scr[slot], ov_scr[slot], n_scr[slot], state[int(rev)])
                o_scr[pl.ds(pl.multiple_of(ci * c, c), c), :] = o
        return [functools.partial(step, grp * unroll + u) for u in range(unroll)]

    n_groups = n_chunks // unroll
    prepare_group(0)

    def body(grp, carry):
        state = list(carry)
        prepare_group(grp, state_steps(grp - 1, state))
        return tuple(state)

    zero = jnp.zeros((2 * c, LANES), F32)
    state = list(lax.fori_loop(1, n_groups, body, (zero, zero)))
    for step in state_steps(n_groups - 1, state):
        step()

    r_k = rk_ref[...]
    lnw = lnw_ref[...]
    lnb = lnb_ref[...]
    blk = TOKEN_TILE
    n_ctx = n_ctx_chunks * c
    inv_n = 1.0 / HEAD_SIZE

    n_blk = (n_chunks - n_ctx_chunks) * c // blk
    par = 2 if n_blk % 2 == 0 else 1

    def read_body(j, _):
        rows = [pl.ds(pl.multiple_of(n_ctx + (j * par + u) * blk, blk), blk) for u in range(par)]
        seg = lambda xs: [_seg_sum(x, ones_bd) for x in xs]
        a_f = [lora(a1_ref, wa_ref, rw, False) for rw in rows]
        a_b = [lora(a1_ref, wa_ref, rw, True) for rw in rows]
        gate = [jnp.dot(g1_ref[0, rw, :], wg_ref[...], preferred_element_type=F32) for rw in rows]
        o = [of_scr[rw, :] + ob_scr[rw, :] for rw in rows]
        dev = [o_ - m_ * inv_n for o_, m_ in zip(o, seg(o))]
        var = seg([d_ * d_ for d_ in dev])
        kd_sum = [load(k_ref, rw) * (2.0 + (jax.nn.sigmoid(a0f + af_) + jax.nn.sigmoid(a0b + ab_) - 2.0) * k_a)
                  for rw, af_, ab_ in zip(rows, a_f, a_b)]
        rkk = seg([load(r_ref, rw) * kd_ * r_k for rw, kd_ in zip(rows, kd_sum)])
        for u in range(par):
            on = dev[u] * lax.rsqrt(var[u] * inv_n + LNX_EPS) * lnw + lnb
            z = (on + rkk[u] * load(v_ref, rows[u])) * gate[u]
            z_ref[0, pl.ds(pl.multiple_of((j * par + u) * blk, blk), blk), :] = z.astype(z_ref.dtype)
        return 0

    lax.fori_loop(0, n_blk // par, read_body, 0)


def _wkv(r, k, v, lw1, a1, g1, w_dec, w_a, w_g, dec_w0, a0, k_k, k_a, r_k, lnx_w, lnx_b, n_ctx):
    b, s, d = r.shape
    l = s - n_ctx
    npair = d // LANES
    seq = lambda bi, p: (bi, 0, p)
    vec = lambda bi, p: (0, p)
    sblk = pl.BlockSpec((1, s, LANES), seq)
    rank_blk = pl.BlockSpec((1, s, 2 * LANES), lambda bi, p: (bi, 0, 0))
    up_blk = pl.BlockSpec((2, LANES, LANES), lambda bi, p: (0, 0, p))
    n_chunks = s // CHUNK
    unroll = next(u for u in (9, 6, 4, 2, 1) if n_chunks % u == 0)
    scratch = [pltpu.VMEM((s, LANES), F32), pltpu.VMEM((s, LANES), F32),
               pltpu.VMEM((2 * n_chunks, 4 * CHUNK, LANES), BF16),
               pltpu.VMEM((2 * n_chunks, CHUNK, LANES), F32),
               pltpu.VMEM((2 * n_chunks, 2 * CHUNK, LANES), F32)]
    est = (2 * (3 * s * LANES * 2 + 2 * s * 2 * LANES * 2 + s * g1.shape[2] * 2 + l * LANES * 2) + 2 * s * LANES * 4
           + 2 * n_chunks * CHUNK * LANES * (4 * 2 + 4 + 2 * 4))
    return pl.pallas_call(
        functools.partial(_wkv_kernel, n_ctx_chunks=n_ctx // CHUNK, n_chunks=n_chunks, unroll=unroll),
        out_shape=jax.ShapeDtypeStruct((b, l, d), BF16),
        grid=(b, npair),
        in_specs=[sblk, sblk, sblk, rank_blk, rank_blk,
                  pl.BlockSpec((1, s, g1.shape[2]), lambda bi, p: (bi, 0, 0)), up_blk, up_blk,
                  pl.BlockSpec((g1.shape[2], LANES), lambda bi, p: (0, p)),
                  pl.BlockSpec((2, LANES), vec), pl.BlockSpec((2, LANES), vec),
                  pl.BlockSpec((1, LANES), vec), pl.BlockSpec((1, LANES), vec), pl.BlockSpec((1, LANES), vec),
                  pl.BlockSpec((1, LANES), vec), pl.BlockSpec((1, LANES), vec)],
        out_specs=pl.BlockSpec((1, l, LANES), seq),
        scratch_shapes=scratch,
        compiler_params=_cparams(("parallel", "parallel"), est),
        name="wkv_scan",
    )(r, k, v, lw1, a1, g1, w_dec, w_a, w_g, dec_w0, a0, k_k.reshape(1, d), k_a.reshape(1, d), r_k.reshape(1, d),
      lnx_w.reshape(1, d), lnx_b.reshape(1, d))


def _proj_post_kernel(a_ref, w_ref, b_ref, x_ref, g1_ref, g2_ref, gate_ref, sh_ref, sc_ref, xo_ref, h_ref):
    y = jnp.dot(a_ref[...], w_ref[...], preferred_element_type=F32) + b_ref[...]
    x = x_ref[...] + gate_ref[0] * (_rms(y) * g1_ref[...])
    xo_ref[...] = x
    h_ref[...] = _norm_mod(x, g2_ref[...], sh_ref[0], sc_ref[0]).astype(h_ref.dtype)


def _proj_post(a, w, bias, x2, mod3, layer, g1, g2, l, tm=512, name="proj_post"):
    m, k = a.shape
    d = w.shape[1]
    per_b = l // tm
    row = lambda j: pl.BlockSpec((1, 1, d), lambda i: (_mod_row(layer, i // per_b, j), 0, 0))
    tile = pl.BlockSpec((tm, d), lambda i: (i, 0))
    vec = pl.BlockSpec((1, d), lambda i: (0, 0))
    est = k * d * 2 + 2 * (tm * k * 2 + tm * d * (4 + 4 + 2)) + 2 * tm * d * 4
    return pl.pallas_call(
        _proj_post_kernel,
        out_shape=[jax.ShapeDtypeStruct((m, d), F32), jax.ShapeDtypeStruct((m, d), BF16)],
        grid=(m // tm,),
        in_specs=[pl.BlockSpec((tm, k), lambda i: (i, 0)),
                  pl.BlockSpec((k, d), lambda i: (0, 0), pipeline_mode=pl.Buffered(1)),
                  vec, tile, vec, vec, row(2), row(3), row(4)],
        out_specs=[tile, tile],
        compiler_params=_cparams(("parallel",), est),
        name=name,
    )(a, w, bias.reshape(1, d).astype(F32), x2, g1.reshape(1, d), g2.reshape(1, d), mod3, mod3, mod3)


def _mlp_kernel(h_ref, wu_ref, wd_ref, x_ref, g_ref, gate_ref, *rest, with_next):
    if with_next:
        gn_ref, shn_ref, scn_ref, o_ref, hn_ref, acc_ref = rest
    else:
        o_ref, acc_ref = rest
    kf = pl.program_id(1)

    @pl.when(kf == 0)
    def _():
        acc_ref[...] = jnp.zeros_like(acc_ref)

    u = jnp.dot(h_ref[...], wu_ref[...], preferred_element_type=F32)
    u = jnp.square(jnp.maximum(u, 0.0)).astype(BF16)
    acc_ref[...] += jnp.dot(u, wd_ref[...], preferred_element_type=F32)

    @pl.when(kf == pl.num_programs(1) - 1)
    def _():
        x = x_ref[...] + gate_ref[0] * (_rms(acc_ref[...]) * g_ref[...])
        o_ref[...] = x
        if with_next:
            hn_ref[...] = _norm_mod(x, gn_ref[...], shn_ref[0], scn_ref[0]).astype(hn_ref.dtype)


def _mlp(h2, w_up, w_down, x2, mod3, layer, g3, l, next_g0=None, tm=512, tf=1024):
    m, d = h2.shape
    dff = w_up.shape[1]
    per_b = l // tm
    with_next = next_g0 is not None
    row = lambda lay, j: pl.BlockSpec((1, 1, d), lambda i, f: (_mod_row(lay, i // per_b, j), 0, 0))
    tile = pl.BlockSpec((tm, d), lambda i, f: (i, 0))
    vec = pl.BlockSpec((1, d), lambda i, f: (0, 0))
    in_specs = [tile, pl.BlockSpec((d, tf), lambda i, f: (0, f)), pl.BlockSpec((tf, d), lambda i, f: (f, 0)),
                tile, vec, row(layer, 5)]
    args = [h2, w_up, w_down, x2, g3.reshape(1, d), mod3]
    out_shape = [jax.ShapeDtypeStruct((m, d), F32)]
    if with_next:
        in_specs += [vec, row(layer + 1, 0), row(layer + 1, 1)]
        args += [next_g0.reshape(1, d), mod3, mod3]
        out_shape.append(jax.ShapeDtypeStruct((m, d), BF16))
    est = 2 * (tm * d * 2 + 2 * d * tf * 2 + 2 * tm * d * 4 + tm * d * 2) + tm * d * 4 + tm * tf * 6
    outs = pl.pallas_call(
        functools.partial(_mlp_kernel, with_next=with_next),
        out_shape=out_shape,
        grid=(m // tm, dff // tf),
        in_specs=in_specs,
        out_specs=[tile] * len(out_shape),
        scratch_shapes=[pltpu.VMEM((tm, d), F32)],
        compiler_params=_cparams(("parallel", "arbitrary"), est),
        name="mlp",
    )(*args)
    return outs if with_next else (outs[0], None)


def _pre_norm_kernel(x_ref, g_ref, sh_ref, sc_ref, h_ref):
    h_ref[...] = _norm_mod(x_ref[...], g_ref[...], sh_ref[0], sc_ref[0]).astype(h_ref.dtype)


def _pre_norm(x2, mod3, layer, g0, l, tm=512, fixed_row=None, out_dtype=BF16):
    m, d = x2.shape
    per_b = l // tm
    if fixed_row is None:
        row = lambda j: pl.BlockSpec((1, 1, d), lambda i: (_mod_row(layer, i // per_b, j), 0, 0))
    else:
        row = lambda j: pl.BlockSpec((1, 1, d), lambda i: (_mod_row(layer, fixed_row, j), 0, 0))
    tile = pl.BlockSpec((tm, d), lambda i: (i, 0))
    return pl.pallas_call(
        _pre_norm_kernel,
        out_shape=jax.ShapeDtypeStruct((m, d), out_dtype),
        grid=(m // tm,),
        in_specs=[tile, pl.BlockSpec((1, d), lambda i: (0, 0)), row(0), row(1)],
        out_specs=tile,
        compiler_params=_cparams(("parallel",), 2 * tm * d * 6),
        name="pre_norm",
    )(x2, g0.reshape(1, d), mod3, mod3)


def _filter_kernel(z_ref, w1_ref, w2_ref, w3_ref, b_ref, fr_ref, w4_ref, t_ref, dl_ref, o_ref, hid_ref):
    hp = lax.Precision.HIGHEST

    @pl.when(pl.program_id(0) == 0)
    def _():
        b = b_ref[...]
        fr = fr_ref[...]
        z = jnp.sin(fr[0:1] * (jnp.dot(z_ref[...], w1_ref[...], precision=hp, preferred_element_type=F32) + b[0:1]))
        z = jnp.sin(fr[1:2] * (jnp.dot(z, w2_ref[...], precision=hp, preferred_element_type=F32) + b[1:2]))
        hid_ref[...] = jnp.sin(fr[2:3] * (jnp.dot(z, w3_ref[...], precision=hp, preferred_element_type=F32)
                                          + b[2:3]))

    filt = jnp.dot(hid_ref[...], w4_ref[...], precision=hp, preferred_element_type=F32)
    o_ref[...] = filt * jnp.exp(-t_ref[...] * dl_ref[...])


def _hyena_filters(l, d, f_w1, f_w23, f_w4, f_b, f_freq):
    t = jnp.linspace(0.0, 1.0, l, dtype=F32)[:, None]
    bands = (HY_EMB_DIM - 1) // 2
    freqs = jnp.linspace(1e-4, bands - 1, bands, dtype=F32)[None, :]
    ang = (2.0 * math.pi / l) * jnp.arange(l, dtype=F32)[:, None] * freqs
    z = jnp.concatenate([t, jnp.cos(ang), -jnp.sin(ang)], axis=-1)
    pad = lambda a_, r, c: jnp.pad(a_.astype(F32), ((0, r - a_.shape[0]), (0, c - a_.shape[1])))
    zp = pad(z, l, LANES)
    w1 = pad(f_w1, LANES, LANES)
    w2 = pad(f_w23[0], LANES, LANES)
    w3 = pad(f_w23[1], LANES, LANES)
    bb = pad(f_b, 8, LANES)
    fr = pad(f_freq, 8, LANES)
    n = f_w4.shape[1]
    w4 = pad(f_w4, LANES, n)
    max_decay = math.log(HY_DECAY_TARGET) / HY_FAST_DECAY
    min_decay = math.log(HY_DECAY_TARGET) / HY_SLOW_DECAY
    deltas = jnp.abs(jnp.linspace(min_decay, max_decay, d, dtype=F32))[None, :]
    tn = _largest_tile(d, 1024, LANES)
    per_d = d // tn
    sq = pl.BlockSpec((LANES, LANES), lambda j: (0, 0))
    small = pl.BlockSpec((8, LANES), lambda j: (0, 0))
    return pl.pallas_call(
        _filter_kernel,
        out_shape=jax.ShapeDtypeStruct((l, n), F32),
        grid=(n // tn,),
        in_specs=[pl.BlockSpec((l, LANES), lambda j: (0, 0)), sq, sq, sq, small, small,
                  pl.BlockSpec((LANES, tn), lambda j: (0, j)),
                  pl.BlockSpec((l, 1), lambda j: (0, 0)),
                  pl.BlockSpec((1, tn), lambda j: (0, j % per_d))],
        out_specs=pl.BlockSpec((l, tn), lambda j: (0, j)),
        scratch_shapes=[pltpu.VMEM((l, LANES), F32)],
        compiler_params=_cparams(("arbitrary",), 4 * l * tn * 4),
        name="hyena_filters",
    )(zp, w1, w2, w3, bb, fr, w4, t, deltas)


def _dft_matrices(l):
    n = 2 * l
    k = jnp.arange(l, dtype=jnp.int32)[:, None]
    t = jnp.arange(l, dtype=jnp.int32)[None, :]
    ang = ((k * t) % n).astype(F32) * (2.0 * math.pi / n)
    cos = jnp.cos(ang)
    msin = jnp.where(k == 0, jnp.where(t % 2 == 0, 1.0, -1.0), -jnp.sin(ang))
    fwd = jnp.concatenate([cos, msin], axis=0)
    return fwd.astype(BF16), fwd.T.astype(BF16)


def _short_conv(z, cw, cb):
    n = z.shape[0]
    row = lax.broadcasted_iota(jnp.int32, (n, 1), 0)
    prev = jnp.where(row != 0, pltpu.roll(z, 1, 0), 0.0)
    nxt = jnp.where(row != n - 1, pltpu.roll(z, n - 1, 0), 0.0)
    return prev * cw[0:1] + z * cw[1:2] + nxt * cw[2:3] + cb


def _spec_kernel(f_ref, hf_ref, hb_ref, o_ref):
    l, tn = hf_ref.shape
    row = lax.broadcasted_iota(jnp.int32, (l, 1), 0)
    hb = jnp.where(row == 0, 0.0, hb_ref[...])
    h2 = jnp.concatenate([hf_ref[...], hb], axis=1).astype(BF16)
    u = jnp.dot(f_ref[...], h2, preferred_element_type=F32)
    row2 = lax.broadcasted_iota(jnp.int32, (2 * l, 1), 0)
    sign = jnp.where(row2 > l, -1.0, 1.0)
    scale = jnp.where((row2 == 0) | (row2 == l), 1.0 / (2 * l), 2.0 / (2 * l))
    o_ref[0] = (u[:, :tn] + sign * u[:, tn:]) * scale


def _spectrum(fwd, filt, d, tn=256):
    l = filt.shape[0]
    n2 = fwd.shape[0]
    orders = filt.shape[1] // (2 * d)
    per = d // tn
    return pl.pallas_call(
        _spec_kernel,
        out_shape=jax.ShapeDtypeStruct((orders, n2, d), F32),
        grid=(orders, per),
        in_specs=[pl.BlockSpec((n2, l), lambda o, j: (0, 0), pipeline_mode=pl.Buffered(1)),
                  pl.BlockSpec((l, tn), lambda o, j: (0, o * per + j)),
                  pl.BlockSpec((l, tn), lambda o, j: (0, (orders + o) * per + j))],
        out_specs=pl.BlockSpec((1, n2, tn), lambda o, j: (o, 0, j)),
        compiler_params=_cparams(("parallel", "parallel"), n2 * l * 2 + 2 * (2 * l * tn * 4 + n2 * tn * 4)
                                 + 3 * n2 * tn * 4),
        name="filter_spectrum",
    )(fwd, filt, filt)


def _conv_fwd_kernel(f_ref, y_ref, k_ref, cw_ref, cb_ref, p_ref, *, short_conv, n_split):
    l = y_ref.shape[1]
    y = y_ref[0]
    if short_conv:
        y = _short_conv(y, cw_ref[...], cb_ref[...])
    yb = y.astype(BF16)
    rows = l // n_split
    for s in range(n_split):
        lo, hi = s * rows, (s + 1) * rows
        ure = jnp.dot(f_ref[lo:hi, :], yb, preferred_element_type=F32)
        uim = jnp.dot(f_ref[l + lo:l + hi, :], yb, preferred_element_type=F32)
        kre = k_ref[0, lo:hi, :]
        kim = k_ref[0, l + lo:l + hi, :]
        pre = ure * kre - uim * kim
        pim = ure * kim + uim * kre
        if s == 0:
            first = lax.broadcasted_iota(jnp.int32, (rows, 1), 0) == 0
            pre = jnp.where(first, ure * kre, pre)
            pim = jnp.where(first, uim * kim, pim)
        p_ref[0, lo:hi, :] = pre.astype(p_ref.dtype)
        p_ref[0, l + lo:l + hi, :] = pim.astype(p_ref.dtype)


def _conv_fwd(fwd, y, y_col0, kspec, order, conv_w, conv_b, short_conv, d, tn=256):
    b, l, _ = y.shape
    n2 = fwd.shape[0]
    off = y_col0 // tn
    est = n2 * l * 2 + 2 * (l * tn * 4 + n2 * tn * 4 + n2 * tn * 2) + 6 * l * tn * 4
    return pl.pallas_call(
        functools.partial(_conv_fwd_kernel, short_conv=short_conv, n_split=2),
        out_shape=jax.ShapeDtypeStruct((b, n2, d), BF16),
        grid=(d // tn, b),
        in_specs=[pl.BlockSpec((n2, l), lambda j, bi: (0, 0), pipeline_mode=pl.Buffered(1)),
                  pl.BlockSpec((1, l, tn), lambda j, bi: (bi, 0, j + off)),
                  pl.BlockSpec((1, n2, tn), lambda j, bi: (order, 0, j)),
                  pl.BlockSpec((3, tn), lambda j, bi: (0, j + off)),
                  pl.BlockSpec((1, tn), lambda j, bi: (0, j + off))],
        out_specs=pl.BlockSpec((1, n2, tn), lambda j, bi: (bi, 0, j)),
        compiler_params=_cparams(("parallel", "parallel"), est),
        name="hyena_conv_fwd",
    )(fwd, y, kspec, conv_w, conv_b)


def _conv_inv_kernel(ft_ref, p_ref, yp_ref, gt_ref, cwy_ref, cby_ref, cwg_ref, cbg_ref, sk_ref, o_ref,
                     *, short_conv_prev):
    conv = jnp.dot(ft_ref[...], p_ref[0], preferred_element_type=F32)
    yp = yp_ref[0]
    if short_conv_prev:
        yp = _short_conv(yp, cwy_ref[...], cby_ref[...])
    gate = _short_conv(gt_ref[0], cwg_ref[...], cbg_ref[...])
    o_ref[0] = (gate * (conv + sk_ref[...] * yp)).astype(o_ref.dtype)


def _conv_inv(finv, p, yprev, yprev_col0, short_conv_prev, z, gate_col0, conv_w, conv_b, skip, out_dtype, tn=256):
    b, n2, d = p.shape
    l = n2 // 2
    offy = yprev_col0 // tn
    offg = gate_col0 // tn
    est = l * n2 * 2 + 2 * (n2 * tn * 2 + 3 * l * tn * 4) + 6 * l * tn * 4
    return pl.pallas_call(
        functools.partial(_conv_inv_kernel, short_conv_prev=short_conv_prev),
        out_shape=jax.ShapeDtypeStruct((b, l, d), out_dtype),
        grid=(d // tn, b),
        in_specs=[pl.BlockSpec((l, n2), lambda j, bi: (0, 0), pipeline_mode=pl.Buffered(1)),
                  pl.BlockSpec((1, n2, tn), lambda j, bi: (bi, 0, j)),
                  pl.BlockSpec((1, l, tn), lambda j, bi: (bi, 0, j + offy)),
                  pl.BlockSpec((1, l, tn), lambda j, bi: (bi, 0, j + offg)),
                  pl.BlockSpec((3, tn), lambda j, bi: (0, j + offy)),
                  pl.BlockSpec((1, tn), lambda j, bi: (0, j + offy)),
                  pl.BlockSpec((3, tn), lambda j, bi: (0, j + offg)),
                  pl.BlockSpec((1, tn), lambda j, bi: (0, j + offg)),
                  pl.BlockSpec((1, tn), lambda j, bi: (0, j))],
        out_specs=pl.BlockSpec((1, l, tn), lambda j, bi: (bi, 0, j)),
        compiler_params=_cparams(("parallel", "parallel"), est),
        name="hyena_conv_inv",
    )(finv, p, yprev, z, conv_w, conv_b, conv_w, conv_b, skip)


def _hyena_mixer(h2, b, l, in_w, in_b, conv_w, conv_b, f_w1, f_w23, f_w4, f_b, f_freq, skip, out_w, out_b):
    d = h2.shape[1]
    z = _matmul(h2, in_w.astype(BF16), bias=in_b, name="hyena_in").reshape(b, l, 3 * d)
    filt = _hyena_filters(l, d, f_w1, f_w23, f_w4, f_b, f_freq)
    fwd, finv = _dft_matrices(l)
    kspec = _spectrum(fwd, filt, d)
    skip = skip.astype(F32)
    p0 = _conv_fwd(fwd, z, 0, kspec, 0, conv_w, conv_b.reshape(1, -1), True, d)
    y1 = _conv_inv(finv, p0, z, 0, True, z, d, conv_w, conv_b.reshape(1, -1), skip[0:1], F32)
    p1 = _conv_fwd(fwd, y1, 0, kspec, 1, conv_w, conv_b.reshape(1, -1), False, d)
    y2 = _conv_inv(finv, p1, y1, 0, False, z, 2 * d, conv_w, conv_b.reshape(1, -1), skip[1:2], BF16)
    return y2.reshape(b * l, d), out_w.astype(BF16), out_b


def _lora_in(w):
    pad = lambda m: jnp.pad(m, ((0, 0), (0, LANES - m.shape[1])))
    return jnp.concatenate([pad(w[0]), pad(w[1])], axis=1)


def _lora_out(w):
    return jnp.pad(w, ((0, 0), (0, LANES - w.shape[1]), (0, 0)))


def _rwkv_mixer(x, ctx, mod3, layer, g0, mu, w_r, w_k, w_v, w_o, dec_w0, dec_w1, dec_w2, a0, a1, a2,
                g1, g2, k_k, k_a, r_k, lnx_w, lnx_b):
    b, l, d = x.shape
    n_ctx = ctx.shape[1]
    s = n_ctx + l
    bf = lambda w: w.astype(BF16)
    assert dec_w1.shape[2] <= LANES and a1.shape[2] <= LANES
    xr, xk, xv, lw1, a1o, g1o = _pre_rwkv(x, ctx, mod3, layer, g0, mu, bf(_lora_in(dec_w1)), bf(_lora_in(a1)),
                                          bf(g1))
    r = _matmul(xr.reshape(b * s, d), bf(w_r), out_dtype=BF16, name="rwkv_r").reshape(b, s, d)
    k = _matmul(xk.reshape(b * s, d), bf(w_k), out_dtype=BF16, name="rwkv_k").reshape(b, s, d)
    v = _matmul(xv.reshape(b * s, d), bf(w_v), out_dtype=BF16, name="rwkv_v").reshape(b, s, d)
    zz = _wkv(r, k, v, lw1, a1o, g1o, bf(_lora_out(dec_w2)), bf(_lora_out(a2)), bf(g2), dec_w0, a0, k_k, k_a, r_k,
              lnx_w, lnx_b, n_ctx)
    return zz.reshape(b * l, d), bf(w_o), jnp.zeros((d,), F32)


def kernel(x, c, ctx, c_ctx, ada_w, ada_b, norm_g, mlp_up, mlp_down, rw_mu, rw_w_r, rw_w_k, rw_w_v, rw_w_o, rw_dec_w0, rw_dec_w1, rw_dec_w2, rw_a0, rw_a1, rw_a2, rw_g1, rw_g2, rw_k_k, rw_k_a, rw_r_k, rw_lnx_w, rw_lnx_b, hy_in_w, hy_in_b, hy_conv_w, hy_conv_b, hy_f_w1, hy_f_w23, hy_f_w4, hy_f_b, hy_f_freq, hy_skip, hy_out_w, hy_out_b):
    b, l, d = x.shape
    depth = ada_w.shape[0]
    assert b < MOD_ROWS
    c_rows = jnp.zeros((MOD_ROWS, d), F32).at[:b].set(c).at[b].set(c_ctx)
    mod3 = _ada_mod(c_rows, ada_w, ada_b)
    x2 = x.reshape(b * l, d)
    xc = ctx
    h_next = None
    for i in range(depth):
        kind, j = i % 2, i // 2
        ctx_live = any(q % 2 == 0 for q in range(i + 1, depth))
        assert not ctx_live, "context-stream update is not implemented for this depth"
        if kind == 0:
            mixed = _rwkv_mixer(x2.reshape(b, l, d), xc, mod3, i, norm_g[i, 0], rw_mu[j], rw_w_r[j], rw_w_k[j],
                                rw_w_v[j], rw_w_o[j], rw_dec_w0[j], rw_dec_w1[j], rw_dec_w2[j], rw_a0[j],
                                rw_a1[j], rw_a2[j], rw_g1[j], rw_g2[j], rw_k_k[j], rw_k_a[j], rw_r_k[j],
                                rw_lnx_w[j], rw_lnx_b[j])
        else:
            h = h_next if h_next is not None else _pre_norm(x2, mod3, i, norm_g[i, 0], l)
            mixed = _hyena_mixer(h, b, l, hy_in_w[j], hy_in_b[j], hy_conv_w[j], hy_conv_b[j], hy_f_w1[j],
                                 hy_f_w23[j], hy_f_w4[j], hy_f_b[j], hy_f_freq[j], hy_skip[j], hy_out_w[j],
                                 hy_out_b[j])
        x2, h2 = _proj_post(*mixed, x2, mod3, i, norm_g[i, 1], norm_g[i, 2], l,
                            name="rwkv_out_post" if kind == 0 else "hyena_out_post")
        next_is_hyena = i + 1 < depth and (i + 1) % 2 == 1
        x2, h_next = _mlp(h2, mlp_up[i].astype(BF16), mlp_down[i].astype(BF16), x2, mod3, i, norm_g[i, 3], l,
                          next_g0=norm_g[i + 1, 0] if next_is_hyena else None)
    return x2.reshape(b, l, d)
```

```python
import functools
import math

import jax
import jax.numpy as jnp
from jax import lax
from jax.experimental import pallas as pl
from jax.experimental.pallas import tpu as pltpu

F32 = jnp.float32
BF16 = jnp.bfloat16

HEAD_SIZE = 64
GRID_W = 64
N_MOD = 6
NORM_EPS = 1e-6
LNX_EPS = 64e-5
HY_FAST_DECAY = 0.3
HY_SLOW_DECAY = 1.5
HY_DECAY_TARGET = 1e-2
HY_EMB_DIM = 33
MOD_ROWS = 16
CHUNK = 64
LANES = 128
TOKEN_TILE = 256
VMEM_LIMIT_CAP = 60000 * 1024


def _cparams(sem, est_bytes):
    limit = int(min(max(2 * est_bytes, 32 * 1024 * 1024), VMEM_LIMIT_CAP))
    return pltpu.CompilerParams(dimension_semantics=sem, vmem_limit_bytes=limit)


def _largest_tile(n, cap, align):
    t = min(cap, n) // align * align
    while t > align and n % t:
        t -= align
    assert t > 0 and n % t == 0, (n, cap, align)
    return t


def _split2(x):
    hi = x.astype(BF16)
    return hi, (x - hi.astype(F32)).astype(BF16)


def _rms(x):
    return x * lax.rsqrt(jnp.mean(x * x, axis=-1, keepdims=True) + NORM_EPS)


def _norm_mod(x, g, shift, scale):
    return (_rms(x) * g) * (1.0 + scale) + shift


def _softplus(y):
    return jnp.maximum(y, 0.0) + jnp.log1p(jnp.exp(-jnp.abs(y)))


def _ada_kernel(c_ref, w_ref, b_ref, o_ref):
    c = c_ref[...]
    s = c * jax.nn.sigmoid(c)
    s_hi = s.astype(BF16)
    s_lo = (s - s_hi.astype(F32)).astype(BF16)
    w = w_ref[0]
    w_hi = w.astype(BF16)
    w_lo = (w - w_hi.astype(F32)).astype(BF16)
    p = jnp.dot(jnp.concatenate([s_hi, s_lo], axis=0), w_hi, preferred_element_type=F32)
    q = jnp.dot(s_hi, w_lo, preferred_element_type=F32)
    o_ref[0] = p[:MOD_ROWS] + p[MOD_ROWS:] + q + b_ref[0]


def _ada_mod(c_rows, ada_w, ada_b):
    depth, d, n = ada_w.shape
    tn = 1024
    out = pl.pallas_call(
        _ada_kernel,
        out_shape=jax.ShapeDtypeStruct((depth, MOD_ROWS, n), F32),
        grid=(depth, n // tn),
        in_specs=[pl.BlockSpec((MOD_ROWS, d), lambda l, j: (0, 0)),
                  pl.BlockSpec((1, d, tn), lambda l, j: (l, 0, j)),
                  pl.BlockSpec((1, 1, tn), lambda l, j: (l, 0, j))],
        out_specs=pl.BlockSpec((1, MOD_ROWS, tn), lambda l, j: (l, 0, j)),
        compiler_params=_cparams(("parallel", "parallel"), 2 * d * tn * 4 + 4 * d * tn),
        name="ada_mod",
    )(c_rows, ada_w, ada_b.reshape(depth, 1, n))
    return out.reshape(depth * MOD_ROWS * N_MOD, 1, d)


def _mod_row(layer, row, j):
    return (layer * MOD_ROWS + row) * N_MOD + j


def _pre_rwkv_kernel(hc_ref, sc_ref_ctx, xm_ref, xu_ref, xd_ref, g_ref, sh_ref, sc_ref, mu_ref,
                     wdec_ref, wa_ref, wg_ref, xr_ref, xk_ref, xv_ref, lw1_ref, a1_ref, g1_ref,
                     mw_scr, ma_scr, mg_scr, *, n_tiles):
    t = pl.program_id(1)
    d = xm_ref.shape[-1]
    g = g_ref[...]
    mu = mu_ref[...]
    big = {0: xr_ref, 2: xk_ref, 3: xv_ref}
    small = {1: mw_scr, 4: ma_scr, 5: mg_scr}

    def emit(h, s, lo, hi):
        xx = s - h
        for j in range(6):
            mix = (h + xx * mu[j:j + 1, lo:hi]).astype(BF16)
            if j in big:
                big[j][0, :, lo:hi] = mix
            else:
                small[j][:, lo:hi] = mix

    @pl.when(t == 0)
    def _():
        emit(hc_ref[0], sc_ref_ctx[0], 0, d)

    @pl.when(t > 0)
    def _():
        sh = sh_ref[0]
        sc = sc_ref[0]
        xm = xm_ref[0].reshape(TOKEN_TILE, d)
        hm = _norm_mod(xm, g, sh, sc)
        hu = _norm_mod(xu_ref[0, 0], g, sh, sc) * jnp.where(t > 1, 1.0, 0.0)
        hd = _norm_mod(xd_ref[0, 0], g, sh, sc) * jnp.where(t < n_tiles, 1.0, 0.0)
        col = lax.broadcasted_iota(jnp.int32, (TOKEN_TILE, 1), 0) & (GRID_W - 1)
        q = d // 4
        left = jnp.where(col != 0, pltpu.roll(hm[:, :q], 1, 0), 0.0)
        right = jnp.where(col != GRID_W - 1, pltpu.roll(hm[:, q:2 * q], TOKEN_TILE - 1, 0), 0.0)
        up = jnp.concatenate([hu[:, 2 * q:3 * q], hm[:TOKEN_TILE - GRID_W, 2 * q:3 * q]], axis=0)
        down = jnp.concatenate([hm[GRID_W:, 3 * q:], hd[:, 3 * q:]], axis=0)
        emit(hm[:, :q], left, 0, q)
        emit(hm[:, q:2 * q], right, q, 2 * q)
        emit(hm[:, 2 * q:3 * q], up, 2 * q, 3 * q)
        emit(hm[:, 3 * q:], down, 3 * q, d)

    lw1_ref[0] = jnp.tanh(jnp.dot(mw_scr[...], wdec_ref[...], preferred_element_type=F32)).astype(BF16)
    a1_ref[0] = jnp.dot(ma_scr[...], wa_ref[...], preferred_element_type=F32).astype(BF16)
    g1_ref[0] = jax.nn.sigmoid(jnp.dot(mg_scr[...], wg_ref[...], preferred_element_type=F32)).astype(BF16)


def _shift_seq(x):
    half = x.shape[-1] // 2
    p = jnp.pad(x, ((0, 0), (1, 1), (0, 0)))
    return jnp.concatenate([p[:, :-2, :half], p[:, 2:, half:]], axis=-1)


def _pre_rwkv(x, ctx, mod3, layer, g0, mu, w_dec1, w_a1, w_g1):
    b, l, d = x.shape
    ranks = (w_dec1.shape[1], w_a1.shape[1], w_g1.shape[1])
    n_ctx = ctx.shape[1]
    assert n_ctx == TOKEN_TILE and l % TOKEN_TILE == 0
    n_tiles = l // TOKEN_TILE
    rows_per_tile = TOKEN_TILE // GRID_W
    n_rows = l // GRID_W
    x4 = x.reshape(b, n_rows, GRID_W, d)
    s = n_ctx + l
    hc = _pre_norm(ctx.reshape(b * n_ctx, d), mod3, layer, g0, n_ctx, tm=n_ctx, fixed_row=b,
                   out_dtype=F32).reshape(b, n_ctx, d)
    sc = _shift_seq(hc)

    def mrow(j):
        return pl.BlockSpec((1, 1, d), lambda bi, t: (_mod_row(layer, bi, j), 0, 0))

    main = lambda bi, t: (bi, jnp.maximum(t - 1, 0), 0, 0)
    up = lambda bi, t: (bi, jnp.maximum((t - 1) * rows_per_tile - 1, 0), 0, 0)
    down = lambda bi, t: (bi, jnp.minimum(jnp.maximum(t, 1) * rows_per_tile, n_rows - 1), 0, 0)
    whole = lambda w: pl.BlockSpec(w.shape, lambda bi, t: (0, 0))
    outs = pl.pallas_call(
        functools.partial(_pre_rwkv_kernel, n_tiles=n_tiles),
        out_shape=[jax.ShapeDtypeStruct((b, s, d), BF16)] * 3
                  + [jax.ShapeDtypeStruct((b, s, rk), BF16) for rk in ranks],
        grid=(b, n_tiles + 1),
        in_specs=[pl.BlockSpec((1, n_ctx, d), lambda bi, t: (bi, 0, 0)),
                  pl.BlockSpec((1, n_ctx, d), lambda bi, t: (bi, 0, 0)),
                  pl.BlockSpec((1, rows_per_tile, GRID_W, d), main),
                  pl.BlockSpec((1, 1, GRID_W, d), up),
                  pl.BlockSpec((1, 1, GRID_W, d), down),
                  pl.BlockSpec((1, d), lambda bi, t: (0, 0)),
                  mrow(0), mrow(1),
                  pl.BlockSpec((6, d), lambda bi, t: (0, 0)),
                  whole(w_dec1), whole(w_a1), whole(w_g1)],
        out_specs=[pl.BlockSpec((1, TOKEN_TILE, d), lambda bi, t: (bi, t, 0))] * 3
                  + [pl.BlockSpec((1, TOKEN_TILE, rk), lambda bi, t: (bi, t, 0)) for rk in ranks],
        scratch_shapes=[pltpu.VMEM((TOKEN_TILE, d), BF16)] * 3,
        compiler_params=_cparams(("parallel", "arbitrary"),
                                 2 * (3 * TOKEN_TILE * d * 4 + 2 * GRID_W * d * 4 + 3 * TOKEN_TILE * d * 2
                                      + d * sum(ranks) * 2) + 3 * TOKEN_TILE * d * 2),
        name="rwkv_pre",
    )(hc, sc, x4, x4, x4, g0.reshape(1, d), mod3, mod3, mu, w_dec1, w_a1, w_g1)
    return outs


def _mm_kernel(a_ref, w_ref, *rest, has_bias):
    o_ref = rest[-1]
    acc = jnp.dot(a_ref[...], w_ref[...], preferred_element_type=F32)
    if has_bias:
        acc = acc + rest[0][...]
    o_ref[...] = acc.astype(o_ref.dtype)


def _matmul(a, w, bias=None, out_dtype=F32, tm=1024, tn=1024, name="matmul"):
    m, k = a.shape
    n = w.shape[1]
    tm = _largest_tile(m, tm, 8)
    tn = _largest_tile(n, tn, LANES)
    in_specs = [pl.BlockSpec((tm, k), lambda i, j: (i, 0)),
                pl.BlockSpec((k, tn), lambda i, j: (0, j))]
    args = [a, w]
    if bias is not None:
        in_specs.append(pl.BlockSpec((1, tn), lambda i, j: (0, j)))
        args.append(bias.reshape(1, n).astype(F32))
    est = 2 * (tm * k * 2 + k * tn * 2 + tm * tn * jnp.dtype(out_dtype).itemsize) + tm * tn * 4
    return pl.pallas_call(
        functools.partial(_mm_kernel, has_bias=bias is not None),
        out_shape=jax.ShapeDtypeStruct((m, n), out_dtype),
        grid=(m // tm, n // tn),
        in_specs=in_specs,
        out_specs=pl.BlockSpec((tm, tn), lambda i, j: (i, j)),
        compiler_params=_cparams(("parallel", "parallel"), est),
        name=name,
    )(*args)


def _seg_sum(x, ones_bd):
    r = x.shape[0]
    p = jnp.dot(jnp.concatenate(_split2(x), axis=0), ones_bd, preferred_element_type=F32)
    return p[:r] + p[r:]


def _bd(x, head0):
    return jnp.concatenate([jnp.where(head0, x, 0.0), jnp.where(head0, 0.0, x)], axis=0)


def _wkv_constants(rev):
    c = CHUNK
    shift = int(math.log2(c))
    head0 = lax.broadcasted_iota(jnp.int32, (1, LANES), 1) < HEAD_SIZE
    row2 = lax.broadcasted_iota(jnp.int32, (2 * c, 2 * c), 0)
    col2 = lax.broadcasted_iota(jnp.int32, (2 * c, 2 * c), 1)
    same = (row2 >> shift) == (col2 >> shift)
    tt = row2 & (c - 1)
    ss = col2 & (c - 1)
    if rev:
        tt, ss = ss, tt
    ones_bd = jnp.where(same, 1.0, 0.0).astype(BF16)
    eye = jnp.where(row2 == col2, 1.0, 0.0).astype(F32)
    rowc = lax.broadcasted_iota(jnp.int32, (c, c), 0)
    colc = lax.broadcasted_iota(jnp.int32, (c, c), 1)
    tri = jnp.where(colc >= rowc if rev else colc <= rowc, 1.0, 0.0).astype(BF16)
    merge = tuple(((tt >> lv) == (ss >> lv) + 1) & ((tt >> (lv + 1)) == (ss >> (lv + 1))) for lv in range(shift))
    return head0, ones_bd, tri, ss < tt, ss <= tt, eye, merge


def _each(fn, *lists):
    return [fn(*args) for args in zip(*lists)]


def _mm(a, b):
    return jnp.dot(a.astype(BF16), b.astype(BF16), preferred_element_type=F32)


def _wkv_prepare(probs, k_a, hooks=()):
    hooks = list(hooks)

    def run_hook():
        if hooks:
            hooks.pop(0)()

    revs, rs, ks, kks, bd_vs, lwxs, axs, w0s, a0s, csts = [list(t) for t in zip(*probs)]
    head0 = csts[0][0]
    eye = csts[0][5]
    c = rs[0].shape[0]
    c2 = 2 * c
    logw = _each(lambda w0, lwx: -jnp.exp(-_softplus(-(w0 + lwx)) - 0.5), w0s, lwxs)
    a = _each(lambda a0, ax: jax.nn.sigmoid(a0 + ax), a0s, axs)
    kd = _each(lambda k, a_: k * (1.0 + (a_ - 1.0) * k_a), ks, a)
    b = _each(lambda kk, a_: kk * a_, kks, a)

    def cumsum(lw, cst):
        c2_ = jnp.dot(cst[2], jnp.concatenate(_split2(lw), axis=1), preferred_element_type=F32)
        return c2_[:, :LANES] + c2_[:, LANES:]

    cum = _each(cumsum, logw, csts)
    total = _each(lambda cm, rev: cm[0:1] if rev else cm[c - 1:c], cum, revs)
    kap_t = _each(lambda kk, cm, lw: kk * jnp.exp(cm - lw), kks, cum, logw)
    r_t = _each(lambda r, cm: r * jnp.exp(cm), rs, cum)
    igam = _each(lambda cm: jnp.exp(-cm), cum)
    tail = _each(lambda t, cm: jnp.exp(t - cm), total, cum)

    bd_kap = _each(lambda x: _bd(x, head0).astype(BF16), kap_t)
    lhs = _each(lambda kap, rt: jnp.concatenate([kap, _bd(rt, head0).astype(BF16)], axis=0), bd_kap, r_t)
    rhs = _each(lambda kd_, b_, ig: jnp.concatenate([_bd(kd_ * ig, head0), _bd(b_ * ig, head0)],
                                                    axis=0).astype(BF16), kd, b, igam)
    bct = _each(lambda b_, tl: _bd(b_ * tl, head0).T.astype(BF16), b, tail)
    kct = _each(lambda kd_, tl: _bd(kd_ * tl, head0).T.astype(BF16), kd, tail)
    gmat = _each(lambda l_, r_: lax.dot_general(l_, r_, (((1,), (1,)), ((), ())),
                                                preferred_element_type=F32), lhs, rhs)
    a_kb = _each(lambda g, cst: jnp.where(cst[3], g[:c2, c2:], 0.0), gmat, csts)
    a_rb = _each(lambda g, cst: jnp.where(cst[4], g[c2:, c2:], 0.0).astype(BF16), gmat, csts)
    a_kr = _each(lambda g, cst: jnp.concatenate([jnp.where(cst[3], g[:c2, :c2], 0.0),
                                                 jnp.where(cst[4], g[c2:, :c2], 0.0)], axis=0).astype(BF16),
                 gmat, csts)
    av = _each(_mm, a_kr, bd_vs)
    x0 = _each(lambda kap, av_: jnp.concatenate([kap, av_[:c2].astype(BF16)], axis=1), bd_kap, av)
    ov0 = _each(lambda av_: av_[c2:c2 + c] + av_[c2 + c:], av)
    run_hook()

    tinv = _each(lambda akb, cst: eye - jnp.where(cst[6][0], akb, 0.0), a_kb, csts)
    for lv in range(1, len(csts[0][6])):
        y = _each(lambda akb, x, cst: _mm(jnp.where(cst[6][lv], akb, 0.0), x), a_kb, tinv, csts)
        run_hook()
        tinv = _each(lambda x, y_: x - _mm(x, y_), tinv, y)
        run_hook()

    wu = _each(lambda x, x0_: _mm(x, x0_).astype(BF16), tinv, x0)
    run_hook()
    rb = _each(_mm, a_rb, wu)
    run_hook()
    mn = _each(_mm, bct, wu)
    while hooks:
        run_hook()
    kv = _each(_mm, kct, bd_vs)
    out = []
    for i in range(len(probs)):
        rk = _bd(r_t[i], head0) - rb[i][:, :LANES]
        ov = ov0[i] - (rb[i][:c, LANES:] + rb[i][c:, LANES:])
        m_mat = jnp.where(eye > 0.0, jnp.exp(total[i]), 0.0) - mn[i][:, :LANES]
        n_mat = kv[i] - mn[i][:, LANES:]
        out.append((jnp.concatenate([rk, m_mat], axis=0).astype(BF16), ov, n_mat))
    return out


def _wkv_apply(rm_lhs, ov, n_mat, h_state):
    c = ov.shape[0]
    rm = jnp.dot(rm_lhs, h_state.astype(BF16), preferred_element_type=F32)
    return rm[:c] + rm[c:2 * c] + ov, rm[2 * c:] + n_mat


def _wkv_kernel(r_ref, k_ref, v_ref, lw1_ref, a1_ref, g1_ref, wdec_ref, wa_ref, wg_ref,
                w0_ref, a0_ref, kk_ref, ka_ref, rk_ref, lnw_ref, lnb_ref,
                z_ref, of_scr, ob_scr, rm_scr, ov_scr, n_scr, *, n_ctx_chunks, n_chunks, unroll):
    c = CHUNK
    cst_f = _wkv_constants(False)
    cst_b = _wkv_constants(True)
    head0, ones_bd = cst_f[0], cst_f[1]

    w0f, w0b = w0_ref[0:1, :], w0_ref[1:2, :]
    a0f, a0b = a0_ref[0:1, :], a0_ref[1:2, :]
    k_k = kk_ref[...]
    k_a = ka_ref[...]

    def load(ref, rows):
        return ref[0, rows, :].astype(F32)

    def lora(x_ref, w_ref, rows, rev):
        d0 = int(rev) * LANES
        return jnp.dot(x_ref[0, rows, d0:d0 + LANES], w_ref[int(rev)], preferred_element_type=F32)

    def chunk_of(i, rev):
        if not rev:
            return i
        return jnp.where(i < n_ctx_chunks, n_ctx_chunks - 1 - i, n_chunks + n_ctx_chunks - 1 - i)

    def prepare_group(grp, hooks=()):
        probs, slots = [], []
        for u in range(unroll):
            for rev, w0, a0, cst in ((False, w0f, a0f, cst_f), (True, w0b, a0b, cst_b)):
                ci = chunk_of(grp * unroll + u, rev)
                rows = pl.ds(pl.multiple_of(ci * c, c), c)
                k = load(k_ref, rows)
                kk0 = k * k_k
                kk = kk0 * lax.rsqrt(jnp.maximum(_seg_sum(kk0 * kk0, ones_bd), 1e-24))
                bd_v = _bd(load(v_ref, rows), head0).astype(BF16)
                probs.append((rev, load(r_ref, rows), k, kk, bd_v, lora(lw1_ref, wdec_ref, rows, rev),
                              lora(a1_ref, wa_ref, rows, rev), w0, a0, cst))
                slots.append(ci + n_chunks * int(rev))
        for slot, (rm_lhs, ov, n_mat) in zip(slots, _wkv_prepare(probs, k_a, hooks)):
            rm_scr[slot] = rm_lhs
            ov_scr[slot] = ov
            n_scr[slot] = n_mat

    def state_steps(grp, state):
        def step(i):
            for rev, o_scr in ((False, of_scr), (True, ob_scr)):
                ci = chunk_of(i, rev)
                slot = ci + n_chunks * int(rev)
                o, state[int(rev)] = _wkv_apply(rm_scr[slot], ov_scr[slot], n_scr[slot], state[int(rev)])
                o_scr[pl.ds(pl.multiple_of(ci * c, c), c), :] = o
        return [functools.partial(step, grp * unroll + u) for u in range(unroll)]

    n_groups = n_chunks // unroll
    prepare_group(0)

    def body(grp, carry):
        state = list(carry)
        prepare_group(grp, state_steps(grp - 1, state))
        return tuple(state)

    zero = jnp.zeros((2 * c, LANES), F32)
    state = list(lax.fori_loop(1, n_groups, body, (zero, zero)))

    r_k = rk_ref[...]
    lnw = lnw_ref[...]
    lnb = lnb_ref[...]
    blk = TOKEN_TILE
    n_ctx = n_ctx_chunks * c
    inv_n = 1.0 / HEAD_SIZE

    def read_blocks(blocks, hooks):
        hooks = list(hooks)

        def run_hook():
            if hooks:
                hooks.pop(0)()

        rows = [pl.ds(n_ctx + j * blk, blk) for j in blocks]
        seg = lambda xs: [_seg_sum(x, ones_bd) for x in xs]
        a_f = [lora(a1_ref, wa_ref, rw, False) for rw in rows]
        a_b = [lora(a1_ref, wa_ref, rw, True) for rw in rows]
        gate = [jnp.dot(g1_ref[0, rw, :], wg_ref[...], preferred_element_type=F32) for rw in rows]
        run_hook()
        o = [of_scr[rw, :] + ob_scr[rw, :] for rw in rows]
        dev = [o_ - m_ * inv_n for o_, m_ in zip(o, seg(o))]
        run_hook()
        var = seg([d_ * d_ for d_ in dev])
        run_hook()
        kd_sum = [load(k_ref, rw) * (2.0 + (jax.nn.sigmoid(a0f + af_) + jax.nn.sigmoid(a0b + ab_) - 2.0) * k_a)
                  for rw, af_, ab_ in zip(rows, a_f, a_b)]
        rkk = seg([load(r_ref, rw) * kd_ * r_k for rw, kd_ in zip(rows, kd_sum)])
        while hooks:
            run_hook()
        for j, rw, dev_, var_, rkk_, gate_ in zip(blocks, rows, dev, var, rkk, gate):
            on = dev_ * lax.rsqrt(var_ * inv_n + LNX_EPS) * lnw + lnb
            z = (on + rkk_ * load(v_ref, rw)) * gate_
            z_ref[0, pl.ds(j * blk, blk), :] = z.astype(z_ref.dtype)

    per_blk = blk // c
    n_blk = (n_chunks - n_ctx_chunks) // per_blk
    first = lambda j: n_ctx_chunks + j * per_blk
    ready = {j: max(first(j) + per_blk - 1, n_chunks + n_ctx_chunks - 1 - first(j)) for j in range(n_blk)}
    order = sorted(range(n_blk), key=lambda j: ready[j])
    pending = [((n_groups - 1) * unroll + u, fn) for u, fn in enumerate(state_steps(n_groups - 1, state))]
    hooks_per_read = 4
    for pos in range(0, n_blk, 2):
        blocks = order[pos:pos + 2]
        need = max(ready[j] for j in blocks)
        while pending and pending[0][0] <= need:
            pending.pop(0)[1]()
        ride, pending = pending[:hooks_per_read], pending[hooks_per_read:]
        read_blocks(blocks, [fn for _, fn in ride])
    for _, fn in pending:
        fn()


def _wkv(r, k, v, lw1, a1, g1, w_dec, w_a, w_g, dec_w0, a0, k_k, k_a, r_k, lnx_w, lnx_b, n_ctx):
    b, s, d = r.shape
    l = s - n_ctx
    npair = d // LANES
    seq = lambda bi, p: (bi, 0, p)
    vec = lambda bi, p: (0, p)
    sblk = pl.BlockSpec((1, s, LANES), seq)
    rank_blk = pl.BlockSpec((1, s, 2 * LANES), lambda bi, p: (bi, 0, 0))
    up_blk = pl.BlockSpec((2, LANES, LANES), lambda bi, p: (0, 0, p))
    n_chunks = s // CHUNK
    unroll = next(u for u in (9, 6, 4, 2, 1) if n_chunks % u == 0)
    scratch = [pltpu.VMEM((s, LANES), F32), pltpu.VMEM((s, LANES), F32),
               pltpu.VMEM((2 * n_chunks, 4 * CHUNK, LANES), BF16),
               pltpu.VMEM((2 * n_chunks, CHUNK, LANES), F32),
               pltpu.VMEM((2 * n_chunks, 2 * CHUNK, LANES), F32)]
    est = (2 * (3 * s * LANES * 2 + 2 * s * 2 * LANES * 2 + s * g1.shape[2] * 2 + l * LANES * 2) + 2 * s * LANES * 4
           + 2 * n_chunks * CHUNK * LANES * (4 * 2 + 4 + 2 * 4))
    return pl.pallas_call(
        functools.partial(_wkv_kernel, n_ctx_chunks=n_ctx // CHUNK, n_chunks=n_chunks, unroll=unroll),
        out_shape=jax.ShapeDtypeStruct((b, l, d), BF16),
        grid=(b, npair),
        in_specs=[sblk, sblk, sblk, rank_blk, rank_blk,
                  pl.BlockSpec((1, s, g1.shape[2]), lambda bi, p: (bi, 0, 0)), up_blk, up_blk,
                  pl.BlockSpec((g1.shape[2], LANES), lambda bi, p: (0, p)),
                  pl.BlockSpec((2, LANES), vec), pl.BlockSpec((2, LANES), vec),
                  pl.BlockSpec((1, LANES), vec), pl.BlockSpec((1, LANES), vec), pl.BlockSpec((1, LANES), vec),
                  pl.BlockSpec((1, LANES), vec), pl.BlockSpec((1, LANES), vec)],
        out_specs=pl.BlockSpec((1, l, LANES), seq),
        scratch_shapes=scratch,
        compiler_params=_cparams(("parallel", "parallel"), est),
        name="wkv_scan",
    )(r, k, v, lw1, a1, g1, w_dec, w_a, w_g, dec_w0, a0, k_k.reshape(1, d), k_a.reshape(1, d), r_k.reshape(1, d),
      lnx_w.reshape(1, d), lnx_b.reshape(1, d))


def _proj_post_kernel(a_ref, w_ref, b_ref, x_ref, g1_ref, g2_ref, gate_ref, sh_ref, sc_ref, xo_ref, h_ref):
    y = jnp.dot(a_ref[...], w_ref[...], preferred_element_type=F32) + b_ref[...]
    x = x_ref[...] + gate_ref[0] * (_rms(y) * g1_ref[...])
    xo_ref[...] = x
    h_ref[...] = _norm_mod(x, g2_ref[...], sh_ref[0], sc_ref[0]).astype(h_ref.dtype)


def _proj_post(a, w, bias, x2, mod3, layer, g1, g2, l, tm=512, name="proj_post"):
    m, k = a.shape
    d = w.shape[1]
    per_b = l // tm
    row = lambda j: pl.BlockSpec((1, 1, d), lambda i: (_mod_row(layer, i // per_b, j), 0, 0))
    tile = pl.BlockSpec((tm, d), lambda i: (i, 0))
    vec = pl.BlockSpec((1, d), lambda i: (0, 0))
    est = k * d * 2 + 2 * (tm * k * 2 + tm * d * (4 + 4 + 2)) + 2 * tm * d * 4
    return pl.pallas_call(
        _proj_post_kernel,
        out_shape=[jax.ShapeDtypeStruct((m, d), F32), jax.ShapeDtypeStruct((m, d), BF16)],
        grid=(m // tm,),
        in_specs=[pl.BlockSpec((tm, k), lambda i: (i, 0)),
                  pl.BlockSpec((k, d), lambda i: (0, 0), pipeline_mode=pl.Buffered(1)),
                  vec, tile, vec, vec, row(2), row(3), row(4)],
        out_specs=[tile, tile],
        compiler_params=_cparams(("parallel",), est),
        name=name,
    )(a, w, bias.reshape(1, d).astype(F32), x2, g1.reshape(1, d), g2.reshape(1, d), mod3, mod3, mod3)


def _mlp_kernel(h_ref, wu_ref, wd_ref, x_ref, g_ref, gate_ref, *rest, with_next):
    if with_next:
        gn_ref, shn_ref, scn_ref, o_ref, hn_ref, acc_ref = rest
    else:
        o_ref, acc_ref = rest
    kf = pl.program_id(1)

    @pl.when(kf == 0)
    def _():
        acc_ref[...] = jnp.zeros_like(acc_ref)

    u = jnp.dot(h_ref[...], wu_ref[...], preferred_element_type=F32)
    u = jnp.square(jnp.maximum(u, 0.0)).astype(BF16)
    acc_ref[...] += jnp.dot(u, wd_ref[...], preferred_element_type=F32)

    @pl.when(kf == pl.num_programs(1) - 1)
    def _():
        x = x_ref[...] + gate_ref[0] * (_rms(acc_ref[...]) * g_ref[...])
        o_ref[...] = x
        if with_next:
            hn_ref[...] = _norm_mod(x, gn_ref[...], shn_ref[0], scn_ref[0]).astype(hn_ref.dtype)


def _mlp(h2, w_up, w_down, x2, mod3, layer, g3, l, next_g0=None, tm=512, tf=1024):
    m, d = h2.shape
    dff = w_up.shape[1]
    per_b = l // tm
    with_next = next_g0 is not None
    row = lambda lay, j: pl.BlockSpec((1, 1, d), lambda i, f: (_mod_row(lay, i // per_b, j), 0, 0))
    tile = pl.BlockSpec((tm, d), lambda i, f: (i, 0))
    vec = pl.BlockSpec((1, d), lambda i, f: (0, 0))
    in_specs = [tile, pl.BlockSpec((d, tf), lambda i, f: (0, f)), pl.BlockSpec((tf, d), lambda i, f: (f, 0)),
                tile, vec, row(layer, 5)]
    args = [h2, w_up, w_down, x2, g3.reshape(1, d), mod3]
    out_shape = [jax.ShapeDtypeStruct((m, d), F32)]
    if with_next:
        in_specs += [vec, row(layer + 1, 0), row(layer + 1, 1)]
        args += [next_g0.reshape(1, d), mod3, mod3]
        out_shape.append(jax.ShapeDtypeStruct((m, d), BF16))
    est = 2 * (tm * d * 2 + 2 * d * tf * 2 + 2 * tm * d * 4 + tm * d * 2) + tm * d * 4 + tm * tf * 6
    outs = pl.pallas_call(
        functools.partial(_mlp_kernel, with_next=with_next),
        out_shape=out_shape,
        grid=(m // tm, dff // tf),
        in_specs=in_specs,
        out_specs=[tile] * len(out_shape),
        scratch_shapes=[pltpu.VMEM((tm, d), F32)],
        compiler_params=_cparams(("parallel", "arbitrary"), est),
        name="mlp",
    )(*args)
    return outs if with_next else (outs[0], None)


def _pre_norm_kernel(x_ref, g_ref, sh_ref, sc_ref, h_ref):
    h_ref[...] = _norm_mod(x_ref[...], g_ref[...], sh_ref[0], sc_ref[0]).astype(h_ref.dtype)


def _pre_norm(x2, mod3, layer, g0, l, tm=512, fixed_row=None, out_dtype=BF16):
    m, d = x2.shape
    per_b = l // tm
    if fixed_row is None:
        row = lambda j: pl.BlockSpec((1, 1, d), lambda i: (_mod_row(layer, i // per_b, j), 0, 0))
    else:
        row = lambda j: pl.BlockSpec((1, 1, d), lambda i: (_mod_row(layer, fixed_row, j), 0, 0))
    tile = pl.BlockSpec((tm, d), lambda i: (i, 0))
    return pl.pallas_call(
        _pre_norm_kernel,
        out_shape=jax.ShapeDtypeStruct((m, d), out_dtype),
        grid=(m // tm,),
        in_specs=[tile, pl.BlockSpec((1, d), lambda i: (0, 0)), row(0), row(1)],
        out_specs=tile,
        compiler_params=_cparams(("parallel",), 2 * tm * d * 6),
        name="pre_norm",
    )(x2, g0.reshape(1, d), mod3, mod3)


def _filter_kernel(z_ref, w1_ref, w2_ref, w3_ref, b_ref, fr_ref, w4_ref, t_ref, dl_ref, o_ref, hid_ref):
    hp = lax.Precision.HIGHEST

    @pl.when(pl.program_id(0) == 0)
    def _():
        b = b_ref[...]
        fr = fr_ref[...]
        z = jnp.sin(fr[0:1] * (jnp.dot(z_ref[...], w1_ref[...], precision=hp, preferred_element_type=F32) + b[0:1]))
        z = jnp.sin(fr[1:2] * (jnp.dot(z, w2_ref[...], precision=hp, preferred_element_type=F32) + b[1:2]))
        hid_ref[...] = jnp.sin(fr[2:3] * (jnp.dot(z, w3_ref[...], precision=hp, preferred_element_type=F32)
                                          + b[2:3]))

    filt = jnp.dot(hid_ref[...], w4_ref[...], precision=hp, preferred_element_type=F32)
    o_ref[...] = filt * jnp.exp(-t_ref[...] * dl_ref[...])


def _hyena_filters(l, d, f_w1, f_w23, f_w4, f_b, f_freq):
    t = jnp.linspace(0.0, 1.0, l, dtype=F32)[:, None]
    bands = (HY_EMB_DIM - 1) // 2
    freqs = jnp.linspace(1e-4, bands - 1, bands, dtype=F32)[None, :]
    ang = (2.0 * math.pi / l) * jnp.arange(l, dtype=F32)[:, None] * freqs
    z = jnp.concatenate([t, jnp.cos(ang), -jnp.sin(ang)], axis=-1)
    pad = lambda a_, r, c: jnp.pad(a_.astype(F32), ((0, r - a_.shape[0]), (0, c - a_.shape[1])))
    zp = pad(z, l, LANES)
    w1 = pad(f_w1, LANES, LANES)
    w2 = pad(f_w23[0], LANES, LANES)
    w3 = pad(f_w23[1], LANES, LANES)
    bb = pad(f_b, 8, LANES)
    fr = pad(f_freq, 8, LANES)
    n = f_w4.shape[1]
    w4 = pad(f_w4, LANES, n)
    max_decay = math.log(HY_DECAY_TARGET) / HY_FAST_DECAY
    min_decay = math.log(HY_DECAY_TARGET) / HY_SLOW_DECAY
    deltas = jnp.abs(jnp.linspace(min_decay, max_decay, d, dtype=F32))[None, :]
    tn = _largest_tile(d, 1024, LANES)
    per_d = d // tn
    sq = pl.BlockSpec((LANES, LANES), lambda j: (0, 0))
    small = pl.BlockSpec((8, LANES), lambda j: (0, 0))
    return pl.pallas_call(
        _filter_kernel,
        out_shape=jax.ShapeDtypeStruct((l, n), F32),
        grid=(n // tn,),
        in_specs=[pl.BlockSpec((l, LANES), lambda j: (0, 0)), sq, sq, sq, small, small,
                  pl.BlockSpec((LANES, tn), lambda j: (0, j)),
                  pl.BlockSpec((l, 1), lambda j: (0, 0)),
                  pl.BlockSpec((1, tn), lambda j: (0, j % per_d))],
        out_specs=pl.BlockSpec((l, tn), lambda j: (0, j)),
        scratch_shapes=[pltpu.VMEM((l, LANES), F32)],
        compiler_params=_cparams(("arbitrary",), 4 * l * tn * 4),
        name="hyena_filters",
    )(zp, w1, w2, w3, bb, fr, w4, t, deltas)


def _dft_matrices(l):
    n = 2 * l
    k = jnp.arange(l, dtype=jnp.int32)[:, None]
    t = jnp.arange(l, dtype=jnp.int32)[None, :]
    ang = ((k * t) % n).astype(F32) * (2.0 * math.pi / n)
    cos = jnp.cos(ang)
    msin = jnp.where(k == 0, jnp.where(t % 2 == 0, 1.0, -1.0), -jnp.sin(ang))
    fwd = jnp.concatenate([cos, msin], axis=0)
    return fwd.astype(BF16), fwd.T.astype(BF16)


def _short_conv(z, cw, cb):
    n = z.shape[0]
    row = lax.broadcasted_iota(jnp.int32, (n, 1), 0)
    prev = jnp.where(row != 0, pltpu.roll(z, 1, 0), 0.0)
    nxt = jnp.where(row != n - 1, pltpu.roll(z, n - 1, 0), 0.0)
    return prev * cw[0:1] + z * cw[1:2] + nxt * cw[2:3] + cb


def _spec_kernel(f_ref, hf_ref, hb_ref, o_ref):
    l, tn = hf_ref.shape
    row = lax.broadcasted_iota(jnp.int32, (l, 1), 0)
    hb = jnp.where(row == 0, 0.0, hb_ref[...])
    h2 = jnp.concatenate([hf_ref[...], hb], axis=1).astype(BF16)
    u = jnp.dot(f_ref[...], h2, preferred_element_type=F32)
    row2 = lax.broadcasted_iota(jnp.int32, (2 * l, 1), 0)
    sign = jnp.where(row2 > l, -1.0, 1.0)
    scale = jnp.where((row2 == 0) | (row2 == l), 1.0 / (2 * l), 2.0 / (2 * l))
    o_ref[0] = (u[:, :tn] + sign * u[:, tn:]) * scale


def _spectrum(fwd, filt, d, tn=256):
    l = filt.shape[0]
    n2 = fwd.shape[0]
    orders = filt.shape[1] // (2 * d)
    per = d // tn
    return pl.pallas_call(
        _spec_kernel,
        out_shape=jax.ShapeDtypeStruct((orders, n2, d), F32),
        grid=(orders, per),
        in_specs=[pl.BlockSpec((n2, l), lambda o, j: (0, 0), pipeline_mode=pl.Buffered(1)),
                  pl.BlockSpec((l, tn), lambda o, j: (0, o * per + j)),
                  pl.BlockSpec((l, tn), lambda o, j: (0, (orders + o) * per + j))],
        out_specs=pl.BlockSpec((1, n2, tn), lambda o, j: (o, 0, j)),
        compiler_params=_cparams(("parallel", "parallel"), n2 * l * 2 + 2 * (2 * l * tn * 4 + n2 * tn * 4)
                                 + 3 * n2 * tn * 4),
        name="filter_spectrum",
    )(fwd, filt, filt)


def _conv_fwd_kernel(f_ref, y_ref, k_ref, cw_ref, cb_ref, p_ref, *, short_conv, n_split):
    l = y_ref.shape[1]
    y = y_ref[0]
    if short_conv:
        y = _short_conv(y, cw_ref[...], cb_ref[...])
    yb = y.astype(BF16)
    rows = l // n_split
    for s in range(n_split):
        lo, hi = s * rows, (s + 1) * rows
        ure = jnp.dot(f_ref[lo:hi, :], yb, preferred_element_type=F32)
        uim = jnp.dot(f_ref[l + lo:l + hi, :], yb, preferred_element_type=F32)
        kre = k_ref[0, lo:hi, :]
        kim = k_ref[0, l + lo:l + hi, :]
        pre = ure * kre - uim * kim
        pim = ure * kim + uim * kre
        if s == 0:
            first = lax.broadcasted_iota(jnp.int32, (rows, 1), 0) == 0
            pre = jnp.where(first, ure * kre, pre)
            pim = jnp.where(first, uim * kim, pim)
        p_ref[0, lo:hi, :] = pre.astype(p_ref.dtype)
        p_ref[0, l + lo:l + hi, :] = pim.astype(p_ref.dtype)


def _conv_fwd(fwd, y, y_col0, kspec, order, conv_w, conv_b, short_conv, d, tn=256):
    b, l, _ = y.shape
    n2 = fwd.shape[0]
    off = y_col0 // tn
    est = n2 * l * 2 + 2 * (l * tn * 4 + n2 * tn * 4 + n2 * tn * 2) + 6 * l * tn * 4
    return pl.pallas_call(
        functools.partial(_conv_fwd_kernel, short_conv=short_conv, n_split=2),
        out_shape=jax.ShapeDtypeStruct((b, n2, d), BF16),
        grid=(d // tn, b),
        in_specs=[pl.BlockSpec((n2, l), lambda j, bi: (0, 0), pipeline_mode=pl.Buffered(1)),
                  pl.BlockSpec((1, l, tn), lambda j, bi: (bi, 0, j + off)),
                  pl.BlockSpec((1, n2, tn), lambda j, bi: (order, 0, j)),
                  pl.BlockSpec((3, tn), lambda j, bi: (0, j + off)),
                  pl.BlockSpec((1, tn), lambda j, bi: (0, j + off))],
        out_specs=pl.BlockSpec((1, n2, tn), lambda j, bi: (bi, 0, j)),
        compiler_params=_cparams(("parallel", "parallel"), est),
        name="hyena_conv_fwd",
    )(fwd, y, kspec, conv_w, conv_b)


def _conv_inv_kernel(ft_ref, p_ref, yp_ref, gt_ref, cwy_ref, cby_ref, cwg_ref, cbg_ref, sk_ref, o_ref,
                     *, short_conv_prev):
    conv = jnp.dot(ft_ref[...], p_ref[0], preferred_element_type=F32)
    yp = yp_ref[0]
    if short_conv_prev:
        yp = _short_conv(yp, cwy_ref[...], cby_ref[...])
    gate = _short_conv(gt_ref[0], cwg_ref[...], cbg_ref[...])
    o_ref[0] = (gate * (conv + sk_ref[...] * yp)).astype(o_ref.dtype)


def _conv_inv(finv, p, yprev, yprev_col0, short_conv_prev, z, gate_col0, conv_w, conv_b, skip, out_dtype, tn=256):
    b, n2, d = p.shape
    l = n2 // 2
    offy = yprev_col0 // tn
    offg = gate_col0 // tn
    est = l * n2 * 2 + 2 * (n2 * tn * 2 + 3 * l * tn * 4) + 6 * l * tn * 4
    return pl.pallas_call(
        functools.partial(_conv_inv_kernel, short_conv_prev=short_conv_prev),
        out_shape=jax.ShapeDtypeStruct((b, l, d), out_dtype),
        grid=(d // tn, b),
        in_specs=[pl.BlockSpec((l, n2), lambda j, bi: (0, 0), pipeline_mode=pl.Buffered(1)),
                  pl.BlockSpec((1, n2, tn), lambda j, bi: (bi, 0, j)),
                  pl.BlockSpec((1, l, tn), lambda j, bi: (bi, 0, j + offy)),
                  pl.BlockSpec((1, l, tn), lambda j, bi: (bi, 0, j + offg)),
                  pl.BlockSpec((3, tn), lambda j, bi: (0, j + offy)),
                  pl.BlockSpec((1, tn), lambda j, bi: (0, j + offy)),
                  pl.BlockSpec((3, tn), lambda j, bi: (0, j + offg)),
                  pl.BlockSpec((1, tn), lambda j, bi: (0, j + offg)),
                  pl.BlockSpec((1, tn), lambda j, bi: (0, j))],
        out_specs=pl.BlockSpec((1, l, tn), lambda j, bi: (bi, 0, j)),
        compiler_params=_cparams(("parallel", "parallel"), est),
        name="hyena_conv_inv",
    )(finv, p, yprev, z, conv_w, conv_b, conv_w, conv_b, skip)


def _hyena_mixer(h2, b, l, in_w, in_b, conv_w, conv_b, f_w1, f_w23, f_w4, f_b, f_freq, skip, out_w, out_b):
    d = h2.shape[1]
    z = _matmul(h2, in_w.astype(BF16), bias=in_b, name="hyena_in").reshape(b, l, 3 * d)
    filt = _hyena_filters(l, d, f_w1, f_w23, f_w4, f_b, f_freq)
    fwd, finv = _dft_matrices(l)
    kspec = _spectrum(fwd, filt, d)
    skip = skip.astype(F32)
    p0 = _conv_fwd(fwd, z, 0, kspec, 0, conv_w, conv_b.reshape(1, -1), True, d)
    y1 = _conv_inv(finv, p0, z, 0, True, z, d, conv_w, conv_b.reshape(1, -1), skip[0:1], F32)
    p1 = _conv_fwd(fwd, y1, 0, kspec, 1, conv_w, conv_b.reshape(1, -1), False, d)
    y2 = _conv_inv(finv, p1, y1, 0, False, z, 2 * d, conv_w, conv_b.reshape(1, -1), skip[1:2], BF16)
    return y2.reshape(b * l, d), out_w.astype(BF16), out_b


def _lora_in(w):
    pad = lambda m: jnp.pad(m, ((0, 0), (0, LANES - m.shape[1])))
    return jnp.concatenate([pad(w[0]), pad(w[1])], axis=1)


def _lora_out(w):
    return jnp.pad(w, ((0, 0), (0, LANES - w.shape[1]), (0, 0)))


def _rwkv_mixer(x, ctx, mod3, layer, g0, mu, w_r, w_k, w_v, w_o, dec_w0, dec_w1, dec_w2, a0, a1, a2,
                g1, g2, k_k, k_a, r_k, lnx_w, lnx_b):
    b, l, d = x.shape
    n_ctx = ctx.shape[1]
    s = n_ctx + l
    bf = lambda w: w.astype(BF16)
    assert dec_w1.shape[2] <= LANES and a1.shape[2] <= LANES
    xr, xk, xv, lw1, a1o, g1o = _pre_rwkv(x, ctx, mod3, layer, g0, mu, bf(_lora_in(dec_w1)), bf(_lora_in(a1)),
                                          bf(g1))
    r = _matmul(xr.reshape(b * s, d), bf(w_r), out_dtype=BF16, name="rwkv_r").reshape(b, s, d)
    k = _matmul(xk.reshape(b * s, d), bf(w_k), out_dtype=BF16, name="rwkv_k").reshape(b, s, d)
    v = _matmul(xv.reshape(b * s, d), bf(w_v), out_dtype=BF16, name="rwkv_v").reshape(b, s, d)
    zz = _wkv(r, k, v, lw1, a1o, g1o, bf(_lora_out(dec_w2)), bf(_lora_out(a2)), bf(g2), dec_w0, a0, k_k, k_a, r_k,
              lnx_w, lnx_b, n_ctx)
    return zz.reshape(b * l, d), bf(w_o), jnp.zeros((d,), F32)


def kernel(x, c, ctx, c_ctx, ada_w, ada_b, norm_g, mlp_up, mlp_down, rw_mu, rw_w_r, rw_w_k, rw_w_v, rw_w_o, rw_dec_w0, rw_dec_w1, rw_dec_w2, rw_a0, rw_a1, rw_a2, rw_g1, rw_g2, rw_k_k, rw_k_a, rw_r_k, rw_lnx_w, rw_lnx_b, hy_in_w, hy_in_b, hy_conv_w, hy_conv_b, hy_f_w1, hy_f_w23, hy_f_w4, hy_f_b, hy_f_freq, hy_skip, hy_out_w, hy_out_b):
    b, l, d = x.shape
    depth = ada_w.shape[0]
    assert b < MOD_ROWS
    c_rows = jnp.zeros((MOD_ROWS, d), F32).at[:b].set(c).at[b].set(c_ctx)
    mod3 = _ada_mod(c_rows, ada_w, ada_b)
    x2 = x.reshape(b * l, d)
    xc = ctx
    h_next = None
    for i in range(depth):
        kind, j = i % 2, i // 2
        ctx_live = any(q % 2 == 0 for q in range(i + 1, depth))
        assert not ctx_live, "context-stream update is not implemented for this depth"
        if kind == 0:
            mixed = _rwkv_mixer(x2.reshape(b, l, d), xc, mod3, i, norm_g[i, 0], rw_mu[j], rw_w_r[j], rw_w_k[j],
                                rw_w_v[j], rw_w_o[j], rw_dec_w0[j], rw_dec_w1[j], rw_dec_w2[j], rw_a0[j],
                                rw_a1[j], rw_a2[j], rw_g1[j], rw_g2[j], rw_k_k[j], rw_k_a[j], rw_r_k[j],
                                rw_lnx_w[j], rw_lnx_b[j])
        else:
            h = h_next if h_next is not None else _pre_norm(x2, mod3, i, norm_g[i, 0], l)
            mixed = _hyena_mixer(h, b, l, hy_in_w[j], hy_in_b[j], hy_conv_w[j], hy_conv_b[j], hy_f_w1[j],
                                 hy_f_w23[j], hy_f_w4[j], hy_f_b[j], hy_f_freq[j], hy_skip[j], hy_out_w[j],
                                 hy_out_b[j])
        x2, h2 = _proj_post(*mixed, x2, mod3, i, norm_g[i, 1], norm_g[i, 2], l,
                            name="rwkv_out_post" if kind == 0 else "hyena_out_post")
        next_is_hyena = i + 1 < depth and (i + 1) % 2 == 1
        x2, h_next = _mlp(h2, mlp_up[i].astype(BF16), mlp_down[i].astype(BF16), x2, mod3, i, norm_g[i, 3], l,
                          next_g0=norm_g[i + 1, 0] if next_is_hyena else None)
    return x2.reshape(b, l, d)
```

```python
import functools
import math

import jax
import jax.numpy as jnp
from jax import lax
from jax.experimental import pallas as pl
from jax.experimental.pallas import tpu as pltpu

F32 = jnp.float32
BF16 = jnp.bfloat16

HEAD_SIZE = 64
GRID_W = 64
N_MOD = 6
NORM_EPS = 1e-6
LNX_EPS = 64e-5
HY_FAST_DECAY = 0.3
HY_SLOW_DECAY = 1.5
HY_DECAY_TARGET = 1e-2
HY_EMB_DIM = 33
MOD_ROWS = 16
CHUNK = 64
LANES = 128
TOKEN_TILE = 256
VMEM_LIMIT_CAP = 60000 * 1024


def _cparams(sem, est_bytes):
    limit = int(min(max(2 * est_bytes, 32 * 1024 * 1024), VMEM_LIMIT_CAP))
    return pltpu.CompilerParams(dimension_semantics=sem, vmem_limit_bytes=limit)


def _largest_tile(n, cap, align):
    t = min(cap, n) // align * align
    while t > align and n % t:
        t -= align
    assert t > 0 and n % t == 0, (n, cap, align)
    return t


def _split2(x):
    hi = x.astype(BF16)
    return hi, (x - hi.astype(F32)).astype(BF16)


def _rms(x):
    return x * lax.rsqrt(jnp.mean(x * x, axis=-1, keepdims=True) + NORM_EPS)


def _norm_mod(x, g, shift, scale):
    return (_rms(x) * g) * (1.0 + scale) + shift


def _softplus(y):
    return jnp.maximum(y, 0.0) + jnp.log1p(jnp.exp(-jnp.abs(y)))


def _ada_kernel(c_ref, w_ref, b_ref, o_ref):
    c = c_ref[...]
    s_hi, s_lo = _split2(c * jax.nn.sigmoid(c))
    w_hi, w_lo = _split2(w_ref[0])
    p = jnp.dot(jnp.concatenate([s_hi, s_lo], axis=0), w_hi, preferred_element_type=F32)
    q = jnp.dot(s_hi, w_lo, preferred_element_type=F32)
    o_ref[0] = p[:MOD_ROWS] + p[MOD_ROWS:] + q + b_ref[0]


def _ada_mod(c_rows, ada_w, ada_b):
    depth, d, n = ada_w.shape
    tn = 1024
    out = pl.pallas_call(
        _ada_kernel,
        out_shape=jax.ShapeDtypeStruct((depth, MOD_ROWS, n), F32),
        grid=(depth, n // tn),
        in_specs=[pl.BlockSpec((MOD_ROWS, d), lambda l, j: (0, 0)),
                  pl.BlockSpec((1, d, tn), lambda l, j: (l, 0, j)),
                  pl.BlockSpec((1, 1, tn), lambda l, j: (l, 0, j))],
        out_specs=pl.BlockSpec((1, MOD_ROWS, tn), lambda l, j: (l, 0, j)),
        compiler_params=_cparams(("parallel", "parallel"), 2 * d * tn * 4 + 4 * d * tn),
        name="ada_mod",
    )(c_rows, ada_w, ada_b.reshape(depth, 1, n))
    return out.reshape(depth * MOD_ROWS * N_MOD, 1, d)


def _mod_row(layer, row, j):
    return (layer * MOD_ROWS + row) * N_MOD + j


def _pre_rwkv_kernel(hc_ref, sc_ref_ctx, xm_ref, xu_ref, xd_ref, g_ref, sh_ref, sc_ref, mu_ref,
                     wdec_ref, wa_ref, wg_ref, xr_ref, xk_ref, xv_ref, lw1_ref, a1_ref, g1_ref,
                     mw_scr, ma_scr, mg_scr, *, n_tiles):
    t = pl.program_id(1)
    d = xm_ref.shape[-1]
    g = g_ref[...]
    mu = mu_ref[...]
    big = {0: xr_ref, 2: xk_ref, 3: xv_ref}
    small = {1: mw_scr, 4: ma_scr, 5: mg_scr}

    def emit(h, s, lo, hi):
        xx = s - h
        for j in range(6):
            mix = (h + xx * mu[j:j + 1, lo:hi]).astype(BF16)
            if j in big:
                big[j][0, :, lo:hi] = mix
            else:
                small[j][:, lo:hi] = mix

    @pl.when(t == 0)
    def _():
        emit(hc_ref[0], sc_ref_ctx[0], 0, d)

    @pl.when(t > 0)
    def _():
        sh = sh_ref[0]
        sc = sc_ref[0]
        xm = xm_ref[0].reshape(TOKEN_TILE, d)
        hm = _norm_mod(xm, g, sh, sc)
        hu = _norm_mod(xu_ref[0, 0], g, sh, sc) * jnp.where(t > 1, 1.0, 0.0)
        hd = _norm_mod(xd_ref[0, 0], g, sh, sc) * jnp.where(t < n_tiles, 1.0, 0.0)
        col = lax.broadcasted_iota(jnp.int32, (TOKEN_TILE, 1), 0) & (GRID_W - 1)
        q = d // 4
        left = jnp.where(col != 0, pltpu.roll(hm[:, :q], 1, 0), 0.0)
        right = jnp.where(col != GRID_W - 1, pltpu.roll(hm[:, q:2 * q], TOKEN_TILE - 1, 0), 0.0)
        up = jnp.concatenate([hu[:, 2 * q:3 * q], hm[:TOKEN_TILE - GRID_W, 2 * q:3 * q]], axis=0)
        down = jnp.concatenate([hm[GRID_W:, 3 * q:], hd[:, 3 * q:]], axis=0)
        emit(hm[:, :q], left, 0, q)
        emit(hm[:, q:2 * q], right, q, 2 * q)
        emit(hm[:, 2 * q:3 * q], up, 2 * q, 3 * q)
        emit(hm[:, 3 * q:], down, 3 * q, d)

    lw1_ref[0] = jnp.tanh(jnp.dot(mw_scr[...], wdec_ref[...], preferred_element_type=F32)).astype(BF16)
    a1_ref[0] = jnp.dot(ma_scr[...], wa_ref[...], preferred_element_type=F32).astype(BF16)
    g1_ref[0] = jax.nn.sigmoid(jnp.dot(mg_scr[...], wg_ref[...], preferred_element_type=F32)).astype(BF16)


def _shift_seq(x):
    half = x.shape[-1] // 2
    p = jnp.pad(x, ((0, 0), (1, 1), (0, 0)))
    return jnp.concatenate([p[:, :-2, :half], p[:, 2:, half:]], axis=-1)


def _pre_rwkv(x, ctx, mod3, layer, g0, mu, w_dec1, w_a1, w_g1):
    b, l, d = x.shape
    ranks = (w_dec1.shape[1], w_a1.shape[1], w_g1.shape[1])
    n_ctx = ctx.shape[1]
    assert n_ctx == TOKEN_TILE and l % TOKEN_TILE == 0
    n_tiles = l // TOKEN_TILE
    rows_per_tile = TOKEN_TILE // GRID_W
    n_rows = l // GRID_W
    x4 = x.reshape(b, n_rows, GRID_W, d)
    s = n_ctx + l
    hc = _pre_norm(ctx.reshape(b * n_ctx, d), mod3, layer, g0, n_ctx, tm=n_ctx, fixed_row=b,
                   out_dtype=F32).reshape(b, n_ctx, d)
    sc = _shift_seq(hc)

    def mrow(j):
        return pl.BlockSpec((1, 1, d), lambda bi, t: (_mod_row(layer, bi, j), 0, 0))

    main = lambda bi, t: (bi, jnp.maximum(t - 1, 0), 0, 0)
    up = lambda bi, t: (bi, jnp.maximum((t - 1) * rows_per_tile - 1, 0), 0, 0)
    down = lambda bi, t: (bi, jnp.minimum(jnp.maximum(t, 1) * rows_per_tile, n_rows - 1), 0, 0)
    whole = lambda w: pl.BlockSpec(w.shape, lambda bi, t: (0, 0))
    outs = pl.pallas_call(
        functools.partial(_pre_rwkv_kernel, n_tiles=n_tiles),
        out_shape=[jax.ShapeDtypeStruct((b, s, d), BF16)] * 3
                  + [jax.ShapeDtypeStruct((b, s, rk), BF16) for rk in ranks],
        grid=(b, n_tiles + 1),
        in_specs=[pl.BlockSpec((1, n_ctx, d), lambda bi, t: (bi, 0, 0)),
                  pl.BlockSpec((1, n_ctx, d), lambda bi, t: (bi, 0, 0)),
                  pl.BlockSpec((1, rows_per_tile, GRID_W, d), main),
                  pl.BlockSpec((1, 1, GRID_W, d), up),
                  pl.BlockSpec((1, 1, GRID_W, d), down),
                  pl.BlockSpec((1, d), lambda bi, t: (0, 0)),
                  mrow(0), mrow(1),
                  pl.BlockSpec((6, d), lambda bi, t: (0, 0)),
                  whole(w_dec1), whole(w_a1), whole(w_g1)],
        out_specs=[pl.BlockSpec((1, TOKEN_TILE, d), lambda bi, t: (bi, t, 0))] * 3
                  + [pl.BlockSpec((1, TOKEN_TILE, rk), lambda bi, t: (bi, t, 0)) for rk in ranks],
        scratch_shapes=[pltpu.VMEM((TOKEN_TILE, d), BF16)] * 3,
        compiler_params=_cparams(("parallel", "arbitrary"),
                                 2 * (3 * TOKEN_TILE * d * 4 + 2 * GRID_W * d * 4 + 3 * TOKEN_TILE * d * 2
                                      + d * sum(ranks) * 2) + 3 * TOKEN_TILE * d * 2),
        name="rwkv_pre",
    )(hc, sc, x4, x4, x4, g0.reshape(1, d), mod3, mod3, mu, w_dec1, w_a1, w_g1)
    return outs


def _mm_kernel(a_ref, w_ref, *rest, has_bias):
    o_ref = rest[-1]
    acc = jnp.dot(a_ref[...], w_ref[...], preferred_element_type=F32)
    if has_bias:
        acc = acc + rest[0][...]
    o_ref[...] = acc.astype(o_ref.dtype)


def _matmul(a, w, bias=None, out_dtype=F32, tm=1024, tn=1024, name="matmul"):
    m, k = a.shape
    n = w.shape[1]
    tm = _largest_tile(m, tm, 8)
    tn = _largest_tile(n, tn, LANES)
    in_specs = [pl.BlockSpec((tm, k), lambda i, j: (i, 0)),
                pl.BlockSpec((k, tn), lambda i, j: (0, j))]
    args = [a, w]
    if bias is not None:
        in_specs.append(pl.BlockSpec((1, tn), lambda i, j: (0, j)))
        args.append(bias.reshape(1, n).astype(F32))
    est = 2 * (tm * k * 2 + k * tn * 2 + tm * tn * jnp.dtype(out_dtype).itemsize) + tm * tn * 4
    return pl.pallas_call(
        functools.partial(_mm_kernel, has_bias=bias is not None),
        out_shape=jax.ShapeDtypeStruct((m, n), out_dtype),
        grid=(m // tm, n // tn),
        in_specs=in_specs,
        out_specs=pl.BlockSpec((tm, tn), lambda i, j: (i, j)),
        compiler_params=_cparams(("parallel", "parallel"), est),
        name=name,
    )(*args)


def _seg_sum(x, ones_bd):
    r = x.shape[0]
    p = jnp.dot(jnp.concatenate(_split2(x), axis=0), ones_bd, preferred_element_type=F32)
    return p[:r] + p[r:]


def _bd(x, head0):
    return jnp.concatenate([jnp.where(head0, x, 0.0), jnp.where(head0, 0.0, x)], axis=0)


def _wkv_constants(rev):
    c = CHUNK
    shift = int(math.log2(c))
    head0 = lax.broadcasted_iota(jnp.int32, (1, LANES), 1) < HEAD_SIZE
    row2 = lax.broadcasted_iota(jnp.int32, (2 * c, 2 * c), 0)
    col2 = lax.broadcasted_iota(jnp.int32, (2 * c, 2 * c), 1)
    ones_bd = jnp.where((row2 >> shift) == (col2 >> shift), 1.0, 0.0).astype(BF16)
    eye = jnp.where(row2 == col2, 1.0, 0.0).astype(F32)
    rowc = lax.broadcasted_iota(jnp.int32, (c, c), 0)
    colc = lax.broadcasted_iota(jnp.int32, (c, c), 1)
    tri = jnp.where(colc >= rowc if rev else colc <= rowc, 1.0, 0.0).astype(BF16)
    tc = lax.broadcasted_iota(jnp.int32, (c, 2 * c), 0)
    sc = lax.broadcasted_iota(jnp.int32, (c, 2 * c), 1) & (c - 1)
    eye_c = jnp.where(tc == sc, 1.0, 0.0).astype(F32)
    if rev:
        tc, sc = sc, tc
    merge = tuple(((tc >> lv) == (sc >> lv) + 1) & ((tc >> (lv + 1)) == (sc >> (lv + 1))) for lv in range(shift))
    return head0, ones_bd, tri, sc < tc, sc <= tc, eye, merge, eye_c


def _each(fn, *lists):
    return [fn(*args) for args in zip(*lists)]


def _mm(a, b):
    return jnp.dot(a.astype(BF16), b.astype(BF16), preferred_element_type=F32)


def _wkv_prepare(probs, k_a, hooks=()):
    hooks = list(hooks)

    def run_hook():
        if hooks:
            hooks.pop(0)()

    revs, rs, ks, kks, bd_vs, lwxs, axs, w0s, a0s, csts = [list(t) for t in zip(*probs)]
    head0 = csts[0][0]
    eye = csts[0][5]
    c = rs[0].shape[0]
    c2 = 2 * c
    logw = _each(lambda w0, lwx: -jnp.exp(-_softplus(-(w0 + lwx)) - 0.5), w0s, lwxs)
    a = _each(lambda a0, ax: jax.nn.sigmoid(a0 + ax), a0s, axs)
    kd = _each(lambda k, a_: k * (1.0 + (a_ - 1.0) * k_a), ks, a)
    b = _each(lambda kk, a_: kk * a_, kks, a)

    def cumsum(lw, cst):
        c2_ = jnp.dot(cst[2], jnp.concatenate(_split2(lw), axis=1), preferred_element_type=F32)
        return c2_[:, :LANES] + c2_[:, LANES:]

    cum = _each(cumsum, logw, csts)
    total = _each(lambda cm, rev: cm[0:1] if rev else cm[c - 1:c], cum, revs)
    kap_t = _each(lambda kk, cm, lw: kk * jnp.exp(cm - lw), kks, cum, logw)
    r_t = _each(lambda r, cm: r * jnp.exp(cm), rs, cum)
    igam = _each(lambda cm: jnp.exp(-cm), cum)
    tail = _each(lambda t, cm: jnp.exp(t - cm), total, cum)

    bd_kap = _each(lambda x: _bd(x, head0).astype(BF16), kap_t)
    lhs = _each(lambda kap, rt: jnp.concatenate([kap, rt], axis=0).astype(BF16), kap_t, r_t)
    rhs = _each(lambda kd_, b_, ig: jnp.concatenate([_bd(kd_ * ig, head0), _bd(b_ * ig, head0)],
                                                    axis=0).astype(BF16), kd, b, igam)
    bct = _each(lambda b_, tl: _bd(b_ * tl, head0).T.astype(BF16), b, tail)
    kct = _each(lambda kd_, tl: _bd(kd_ * tl, head0).T.astype(BF16), kd, tail)
    gmat = _each(lambda l_, r_: lax.dot_general(l_, r_, (((1,), (1,)), ((), ())),
                                                preferred_element_type=F32), lhs, rhs)
    a_kb = _each(lambda g, cst: jnp.where(cst[3], g[:c, c2:], 0.0), gmat, csts)
    a_rb = _each(lambda g, cst: jnp.where(cst[4], g[c:, c2:], 0.0).astype(BF16), gmat, csts)
    a_kr = _each(lambda g, cst: jnp.concatenate([jnp.where(cst[3], g[:c, :c2], 0.0),
                                                 jnp.where(cst[4], g[c:, :c2], 0.0)], axis=0).astype(BF16),
                 gmat, csts)
    av = _each(_mm, a_kr, bd_vs)
    x0 = _each(lambda kap, av_: jnp.concatenate([kap, _bd(av_[:c], head0).astype(BF16)], axis=1), bd_kap, av)
    ov0 = _each(lambda av_: av_[c:], av)
    run_hook()

    tinv = _each(lambda akb, cst: cst[7] - jnp.where(cst[6][0], akb, 0.0), a_kb, csts)
    for lv in range(1, len(csts[0][6])):
        y = _each(lambda akb, x, cst: _mm(jnp.where(cst[6][lv], akb, 0.0), _bd(x, head0)), a_kb, tinv, csts)
        run_hook()
        tinv = _each(lambda x, y_: x - _mm(x, _bd(y_, head0)), tinv, y)
        run_hook()

    wu_c = _each(_mm, tinv, x0)
    wu = _each(lambda w_: jnp.concatenate([_bd(w_[:, :LANES], head0), _bd(w_[:, LANES:], head0)],
                                          axis=1).astype(BF16), wu_c)
    run_hook()
    rb = _each(_mm, a_rb, wu)
    run_hook()
    mn = _each(_mm, bct, wu)
    while hooks:
        run_hook()
    kv = _each(_mm, kct, bd_vs)
    out = []
    for i in range(len(probs)):
        rk = r_t[i] - rb[i][:, :LANES]
        ov = ov0[i] - rb[i][:, LANES:]
        m_mat = jnp.where(eye > 0.0, jnp.exp(total[i]), 0.0) - mn[i][:, :LANES]
        n_mat = kv[i] - mn[i][:, LANES:]
        out.append((jnp.concatenate([rk, m_mat], axis=0).astype(BF16), ov, n_mat))
    return out


def _wkv_apply(rm_lhs, ov, n_mat, h_state):
    c = ov.shape[0]
    rm = jnp.dot(rm_lhs, h_state.astype(BF16), preferred_element_type=F32)
    return rm[:c] + ov, rm[c:] + n_mat


def _wkv_kernel(r_ref, k_ref, v_ref, lw1_ref, a1_ref, g1_ref, wdec_ref, wa_ref, wg_ref,
                w0_ref, a0_ref, kk_ref, ka_ref, rk_ref, lnw_ref, lnb_ref,
                z_ref, of_scr, ob_scr, rm_scr, ov_scr, n_scr, *, n_ctx_chunks, n_chunks, unroll):
    c = CHUNK
    cst_f = _wkv_constants(False)
    cst_b = _wkv_constants(True)
    head0, ones_bd = cst_f[0], cst_f[1]

    w0f, w0b = w0_ref[0:1, :], w0_ref[1:2, :]
    a0f, a0b = a0_ref[0:1, :], a0_ref[1:2, :]
    k_k = kk_ref[...]
    k_a = ka_ref[...]

    def load(ref, rows):
        return ref[0, rows, :].astype(F32)

    def lora(x_ref, w_ref, rows, rev):
        d0 = int(rev) * LANES
        return jnp.dot(x_ref[0, rows, d0:d0 + LANES], w_ref[int(rev)], preferred_element_type=F32)

    def chunk_of(i, rev):
        if not rev:
            return i
        return jnp.where(i < n_ctx_chunks, n_ctx_chunks - 1 - i, n_chunks + n_ctx_chunks - 1 - i)

    def prepare_group(grp, hooks=()):
        probs, slots = [], []
        for u in range(unroll):
            for rev, w0, a0, cst in ((False, w0f, a0f, cst_f), (True, w0b, a0b, cst_b)):
                ci = chunk_of(grp * unroll + u, rev)
                rows = pl.ds(pl.multiple_of(ci * c, c), c)
                k = load(k_ref, rows)
                kk0 = k * k_k
                kk = kk0 * lax.rsqrt(jnp.maximum(_seg_sum(kk0 * kk0, ones_bd), 1e-24))
                bd_v = _bd(load(v_ref, rows), head0).astype(BF16)
                probs.append((rev, load(r_ref, rows), k, kk, bd_v, lora(lw1_ref, wdec_ref, rows, rev),
                              lora(a1_ref, wa_ref, rows, rev), w0, a0, cst))
                slots.append(ci + n_chunks * int(rev))
        for slot, (rm_lhs, ov, n_mat) in zip(slots, _wkv_prepare(probs, k_a, hooks)):
            rm_scr[slot] = rm_lhs
            ov_scr[slot] = ov
            n_scr[slot] = n_mat

    def state_steps(grp, state):
        def step(i):
            for rev, o_scr in ((False, of_scr), (True, ob_scr)):
                ci = chunk_of(i, rev)
                slot = ci + n_chunks * int(rev)
                o, state[int(rev)] = _wkv_apply(rm_scr[slot], ov_scr[slot], n_scr[slot], state[int(rev)])
                o_scr[pl.ds(pl.multiple_of(ci * c, c), c), :] = o
        return [functools.partial(step, grp * unroll + u) for u in range(unroll)]

    n_groups = n_chunks // unroll
    prepare_group(0)

    def body(grp, carry):
        state = list(carry)
        prepare_group(grp, state_steps(grp - 1, state))
        return tuple(state)

    zero = jnp.zeros((2 * c, LANES), F32)
    state = list(lax.fori_loop(1, n_groups, body, (zero, zero)))

    r_k = rk_ref[...]
    lnw = lnw_ref[...]
    lnb = lnb_ref[...]
    blk = TOKEN_TILE
    n_ctx = n_ctx_chunks * c
    inv_n = 1.0 / HEAD_SIZE

    def read_blocks(blocks, hooks):
        hooks = list(hooks)

        def run_hook():
            if hooks:
                hooks.pop(0)()

        rows = [pl.ds(n_ctx + j * blk, blk) for j in blocks]
        seg = lambda xs: [_seg_sum(x, ones_bd) for x in xs]
        a_f = [lora(a1_ref, wa_ref, rw, False) for rw in rows]
        a_b = [lora(a1_ref, wa_ref, rw, True) for rw in rows]
        gate = [jnp.dot(g1_ref[0, rw, :], wg_ref[...], preferred_element_type=F32) for rw in rows]
        run_hook()
        o = [of_scr[rw, :] + ob_scr[rw, :] for rw in rows]
        dev = [o_ - m_ * inv_n for o_, m_ in zip(o, seg(o))]
        run_hook()
        var = seg([d_ * d_ for d_ in dev])
        run_hook()
        kd_sum = [load(k_ref, rw) * (2.0 + (jax.nn.sigmoid(a0f + af_) + jax.nn.sigmoid(a0b + ab_) - 2.0) * k_a)
                  for rw, af_, ab_ in zip(rows, a_f, a_b)]
        rkk = seg([load(r_ref, rw) * kd_ * r_k for rw, kd_ in zip(rows, kd_sum)])
        while hooks:
            run_hook()
        for j, rw, dev_, var_, rkk_, gate_ in zip(blocks, rows, dev, var, rkk, gate):
            on = dev_ * lax.rsqrt(var_ * inv_n + LNX_EPS) * lnw + lnb
            z = (on + rkk_ * load(v_ref, rw)) * gate_
            z_ref[0, pl.ds(j * blk, blk), :] = z.astype(z_ref.dtype)

    per_blk = blk // c
    n_blk = (n_chunks - n_ctx_chunks) // per_blk
    first = lambda j: n_ctx_chunks + j * per_blk
    ready = {j: max(first(j) + per_blk - 1, n_chunks + n_ctx_chunks - 1 - first(j)) for j in range(n_blk)}
    order = sorted(range(n_blk), key=lambda j: ready[j])
    pending = [((n_groups - 1) * unroll + u, fn) for u, fn in enumerate(state_steps(n_groups - 1, state))]
    hooks_per_read = 4
    for pos in range(0, n_blk, 2):
        blocks = order[pos:pos + 2]
        need = max(ready[j] for j in blocks)
        while pending and pending[0][0] <= need:
            pending.pop(0)[1]()
        ride, pending = pending[:hooks_per_read], pending[hooks_per_read:]
        read_blocks(blocks, [fn for _, fn in ride])
    for _, fn in pending:
        fn()


def _wkv(r, k, v, lw1, a1, g1, w_dec, w_a, w_g, dec_w0, a0, k_k, k_a, r_k, lnx_w, lnx_b, n_ctx):
    b, s, d = r.shape
    l = s - n_ctx
    npair = d // LANES
    seq = lambda bi, p: (bi, 0, p)
    vec = lambda bi, p: (0, p)
    sblk = pl.BlockSpec((1, s, LANES), seq)
    rank_blk = pl.BlockSpec((1, s, 2 * LANES), lambda bi, p: (bi, 0, 0))
    up_blk = pl.BlockSpec((2, LANES, LANES), lambda bi, p: (0, 0, p))
    n_chunks = s // CHUNK
    unroll = next(u for u in (9, 6, 4, 2, 1) if n_chunks % u == 0)
    scratch = [pltpu.VMEM((s, LANES), F32), pltpu.VMEM((s, LANES), F32),
               pltpu.VMEM((2 * n_chunks, 3 * CHUNK, LANES), BF16),
               pltpu.VMEM((2 * n_chunks, CHUNK, LANES), F32),
               pltpu.VMEM((2 * n_chunks, 2 * CHUNK, LANES), F32)]
    est = (2 * (3 * s * LANES * 2 + 2 * s * 2 * LANES * 2 + s * g1.shape[2] * 2 + l * LANES * 2) + 2 * s * LANES * 4
           + 2 * n_chunks * CHUNK * LANES * (3 * 2 + 4 + 2 * 4))
    return pl.pallas_call(
        functools.partial(_wkv_kernel, n_ctx_chunks=n_ctx // CHUNK, n_chunks=n_chunks, unroll=unroll),
        out_shape=jax.ShapeDtypeStruct((b, l, d), BF16),
        grid=(b, npair),
        in_specs=[sblk, sblk, sblk, rank_blk, rank_blk,
                  pl.BlockSpec((1, s, g1.shape[2]), lambda bi, p: (bi, 0, 0)), up_blk, up_blk,
                  pl.BlockSpec((g1.shape[2], LANES), lambda bi, p: (0, p)),
                  pl.BlockSpec((2, LANES), vec), pl.BlockSpec((2, LANES), vec),
                  pl.BlockSpec((1, LANES), vec), pl.BlockSpec((1, LANES), vec), pl.BlockSpec((1, LANES), vec),
                  pl.BlockSpec((1, LANES), vec), pl.BlockSpec((1, LANES), vec)],
        out_specs=pl.BlockSpec((1, l, LANES), seq),
        scratch_shapes=scratch,
        compiler_params=_cparams(("parallel", "parallel"), est),
        name="wkv_scan",
    )(r, k, v, lw1, a1, g1, w_dec, w_a, w_g, dec_w0, a0, k_k.reshape(1, d), k_a.reshape(1, d), r_k.reshape(1, d),
      lnx_w.reshape(1, d), lnx_b.reshape(1, d))


def _proj_post_kernel(a_ref, w_ref, b_ref, x_ref, g1_ref, g2_ref, gate_ref, sh_ref, sc_ref, xo_ref, h_ref):
    y = jnp.dot(a_ref[...], w_ref[...], preferred_element_type=F32) + b_ref[...]
    x = x_ref[...] + gate_ref[0] * (_rms(y) * g1_ref[...])
    xo_ref[...] = x
    h_ref[...] = _norm_mod(x, g2_ref[...], sh_ref[0], sc_ref[0]).astype(h_ref.dtype)


def _proj_post(a, w, bias, x2, mod3, layer, g1, g2, l, tm=512, name="proj_post"):
    m, k = a.shape
    d = w.shape[1]
    per_b = l // tm
    row = lambda j: pl.BlockSpec((1, 1, d), lambda i: (_mod_row(layer, i // per_b, j), 0, 0))
    tile = pl.BlockSpec((tm, d), lambda i: (i, 0))
    vec = pl.BlockSpec((1, d), lambda i: (0, 0))
    est = k * d * 2 + 2 * (tm * k * 2 + tm * d * (4 + 4 + 2)) + 2 * tm * d * 4
    return pl.pallas_call(
        _proj_post_kernel,
        out_shape=[jax.ShapeDtypeStruct((m, d), F32), jax.ShapeDtypeStruct((m, d), BF16)],
        grid=(m // tm,),
        in_specs=[pl.BlockSpec((tm, k), lambda i: (i, 0)),
                  pl.BlockSpec((k, d), lambda i: (0, 0), pipeline_mode=pl.Buffered(1)),
                  vec, tile, vec, vec, row(2), row(3), row(4)],
        out_specs=[tile, tile],
        compiler_params=_cparams(("parallel",), est),
        name=name,
    )(a, w, bias.reshape(1, d).astype(F32), x2, g1.reshape(1, d), g2.reshape(1, d), mod3, mod3, mod3)


def _mlp_kernel(h_ref, wu_ref, wd_ref, x_ref, g_ref, gate_ref, *rest, with_next):
    if with_next:
        gn_ref, shn_ref, scn_ref, o_ref, hn_ref, acc_ref = rest
    else:
        o_ref, acc_ref = rest
    kf = pl.program_id(1)

    @pl.when(kf == 0)
    def _():
        acc_ref[...] = jnp.zeros_like(acc_ref)

    u = jnp.dot(h_ref[...], wu_ref[...], preferred_element_type=F32)
    u = jnp.square(jnp.maximum(u, 0.0)).astype(BF16)
    acc_ref[...] += jnp.dot(u, wd_ref[...], preferred_element_type=F32)

    @pl.when(kf == pl.num_programs(1) - 1)
    def _():
        x = x_ref[...] + gate_ref[0] * (_rms(acc_ref[...]) * g_ref[...])
        o_ref[...] = x
        if with_next:
            hn_ref[...] = _norm_mod(x, gn_ref[...], shn_ref[0], scn_ref[0]).astype(hn_ref.dtype)


def _mlp(h2, w_up, w_down, x2, mod3, layer, g3, l, next_g0=None, tm=512, tf=1024):
    m, d = h2.shape
    dff = w_up.shape[1]
    per_b = l // tm
    with_next = next_g0 is not None
    row = lambda lay, j: pl.BlockSpec((1, 1, d), lambda i, f: (_mod_row(lay, i // per_b, j), 0, 0))
    tile = pl.BlockSpec((tm, d), lambda i, f: (i, 0))
    vec = pl.BlockSpec((1, d), lambda i, f: (0, 0))
    in_specs = [tile, pl.BlockSpec((d, tf), lambda i, f: (0, f)), pl.BlockSpec((tf, d), lambda i, f: (f, 0)),
                tile, vec, row(layer, 5)]
    args = [h2, w_up, w_down, x2, g3.reshape(1, d), mod3]
    out_shape = [jax.ShapeDtypeStruct((m, d), F32)]
    if with_next:
        in_specs += [vec, row(layer + 1, 0), row(layer + 1, 1)]
        args += [next_g0.reshape(1, d), mod3, mod3]
        out_shape.append(jax.ShapeDtypeStruct((m, d), BF16))
    est = 2 * (tm * d * 2 + 2 * d * tf * 2 + 2 * tm * d * 4 + tm * d * 2) + tm * d * 4 + tm * tf * 6
    outs = pl.pallas_call(
        functools.partial(_mlp_kernel, with_next=with_next),
        out_shape=out_shape,
        grid=(m // tm, dff // tf),
        in_specs=in_specs,
        out_specs=[tile] * len(out_shape),
        scratch_shapes=[pltpu.VMEM((tm, d), F32)],
        compiler_params=_cparams(("parallel", "arbitrary"), est),
        name="mlp",
    )(*args)
    return outs if with_next else (outs[0], None)


def _pre_norm_kernel(x_ref, g_ref, sh_ref, sc_ref, h_ref):
    h_ref[...] = _norm_mod(x_ref[...], g_ref[...], sh_ref[0], sc_ref[0]).astype(h_ref.dtype)


def _pre_norm(x2, mod3, layer, g0, l, tm=512, fixed_row=None, out_dtype=BF16):
    m, d = x2.shape
    per_b = l // tm
    if fixed_row is None:
        row = lambda j: pl.BlockSpec((1, 1, d), lambda i: (_mod_row(layer, i // per_b, j), 0, 0))
    else:
        row = lambda j: pl.BlockSpec((1, 1, d), lambda i: (_mod_row(layer, fixed_row, j), 0, 0))
    tile = pl.BlockSpec((tm, d), lambda i: (i, 0))
    return pl.pallas_call(
        _pre_norm_kernel,
        out_shape=jax.ShapeDtypeStruct((m, d), out_dtype),
        grid=(m // tm,),
        in_specs=[tile, pl.BlockSpec((1, d), lambda i: (0, 0)), row(0), row(1)],
        out_specs=tile,
        compiler_params=_cparams(("parallel",), 2 * tm * d * 6),
        name="pre_norm",
    )(x2, g0.reshape(1, d), mod3, mod3)


def _filter_kernel(z_ref, w1_ref, w2_ref, w3_ref, b_ref, fr_ref, w4_ref, t_ref, dl_ref, o_ref, hid_ref):
    hp = lax.Precision.HIGHEST

    @pl.when(pl.program_id(0) == 0)
    def _():
        b = b_ref[...]
        fr = fr_ref[...]
        z = jnp.sin(fr[0:1] * (jnp.dot(z_ref[...], w1_ref[...], precision=hp, preferred_element_type=F32) + b[0:1]))
        z = jnp.sin(fr[1:2] * (jnp.dot(z, w2_ref[...], precision=hp, preferred_element_type=F32) + b[1:2]))
        hid_ref[...] = jnp.sin(fr[2:3] * (jnp.dot(z, w3_ref[...], precision=hp, preferred_element_type=F32)
                                          + b[2:3]))

    filt = jnp.dot(hid_ref[...], w4_ref[...], precision=hp, preferred_element_type=F32)
    o_ref[...] = filt * jnp.exp(-t_ref[...] * dl_ref[...])


def _hyena_filters(l, d, f_w1, f_w23, f_w4, f_b, f_freq):
    t = jnp.linspace(0.0, 1.0, l, dtype=F32)[:, None]
    bands = (HY_EMB_DIM - 1) // 2
    freqs = jnp.linspace(1e-4, bands - 1, bands, dtype=F32)[None, :]
    ang = (2.0 * math.pi / l) * jnp.arange(l, dtype=F32)[:, None] * freqs
    z = jnp.concatenate([t, jnp.cos(ang), -jnp.sin(ang)], axis=-1)
    pad = lambda a_, r, c: jnp.pad(a_.astype(F32), ((0, r - a_.shape[0]), (0, c - a_.shape[1])))
    zp = pad(z, l, LANES)
    w1 = pad(f_w1, LANES, LANES)
    w2 = pad(f_w23[0], LANES, LANES)
    w3 = pad(f_w23[1], LANES, LANES)
    bb = pad(f_b, 8, LANES)
    fr = pad(f_freq, 8, LANES)
    n = f_w4.shape[1]
    w4 = pad(f_w4, LANES, n)
    max_decay = math.log(HY_DECAY_TARGET) / HY_FAST_DECAY
    min_decay = math.log(HY_DECAY_TARGET) / HY_SLOW_DECAY
    deltas = jnp.abs(jnp.linspace(min_decay, max_decay, d, dtype=F32))[None, :]
    tn = _largest_tile(d, 1024, LANES)
    per_d = d // tn
    sq = pl.BlockSpec((LANES, LANES), lambda j: (0, 0))
    small = pl.BlockSpec((8, LANES), lambda j: (0, 0))
    return pl.pallas_call(
        _filter_kernel,
        out_shape=jax.ShapeDtypeStruct((l, n), F32),
        grid=(n // tn,),
        in_specs=[pl.BlockSpec((l, LANES), lambda j: (0, 0)), sq, sq, sq, small, small,
                  pl.BlockSpec((LANES, tn), lambda j: (0, j)),
                  pl.BlockSpec((l, 1), lambda j: (0, 0)),
                  pl.BlockSpec((1, tn), lambda j: (0, j % per_d))],
        out_specs=pl.BlockSpec((l, tn), lambda j: (0, j)),
        scratch_shapes=[pltpu.VMEM((l, LANES), F32)],
        compiler_params=_cparams(("arbitrary",), 4 * l * tn * 4),
        name="hyena_filters",
    )(zp, w1, w2, w3, bb, fr, w4, t, deltas)


def _dft_matrices(l):
    n = 2 * l
    k = jnp.arange(l, dtype=jnp.int32)[:, None]
    t = jnp.arange(l, dtype=jnp.int32)[None, :]
    ang = ((k * t) % n).astype(F32) * (2.0 * math.pi / n)
    cos = jnp.cos(ang)
    msin = jnp.where(k == 0, jnp.where(t % 2 == 0, 1.0, -1.0), -jnp.sin(ang))
    fwd = jnp.concatenate([cos, msin], axis=0)
    return fwd.astype(BF16), fwd.T.astype(BF16)


def _short_conv(z, cw, cb):
    n = z.shape[0]
    row = lax.broadcasted_iota(jnp.int32, (n, 1), 0)
    prev = jnp.where(row != 0, pltpu.roll(z, 1, 0), 0.0)
    nxt = jnp.where(row != n - 1, pltpu.roll(z, n - 1, 0), 0.0)
    return prev * cw[0:1] + z * cw[1:2] + nxt * cw[2:3] + cb


def _spec_kernel(f_ref, hf_ref, hb_ref, o_ref):
    l, tn = hf_ref.shape
    row = lax.broadcasted_iota(jnp.int32, (l, 1), 0)
    hb = jnp.where(row == 0, 0.0, hb_ref[...])
    h2 = jnp.concatenate([hf_ref[...], hb], axis=1).astype(BF16)
    u = jnp.dot(f_ref[...], h2, preferred_element_type=F32)
    row2 = lax.broadcasted_iota(jnp.int32, (2 * l, 1), 0)
    sign = jnp.where(row2 > l, -1.0, 1.0)
    scale = jnp.where((row2 == 0) | (row2 == l), 1.0 / (2 * l), 2.0 / (2 * l))
    o_ref[0] = (u[:, :tn] + sign * u[:, tn:]) * scale


def _spectrum(fwd, filt, d, tn=256):
    l = filt.shape[0]
    n2 = fwd.shape[0]
    orders = filt.shape[1] // (2 * d)
    per = d // tn
    return pl.pallas_call(
        _spec_kernel,
        out_shape=jax.ShapeDtypeStruct((orders, n2, d), F32),
        grid=(orders, per),
        in_specs=[pl.BlockSpec((n2, l), lambda o, j: (0, 0), pipeline_mode=pl.Buffered(1)),
                  pl.BlockSpec((l, tn), lambda o, j: (0, o * per + j)),
                  pl.BlockSpec((l, tn), lambda o, j: (0, (orders + o) * per + j))],
        out_specs=pl.BlockSpec((1, n2, tn), lambda o, j: (o, 0, j)),
        compiler_params=_cparams(("parallel", "parallel"), n2 * l * 2 + 2 * (2 * l * tn * 4 + n2 * tn * 4)
                                 + 3 * n2 * tn * 4),
        name="filter_spectrum",
    )(fwd, filt, filt)


def _conv_fwd_kernel(f_ref, y_ref, k_ref, cw_ref, cb_ref, p_ref, *, short_conv, n_split):
    l = y_ref.shape[1]
    y = y_ref[0]
    if short_conv:
        y = _short_conv(y, cw_ref[...], cb_ref[...])
    yb = y.astype(BF16)
    rows = l // n_split
    for s in range(n_split):
        lo, hi = s * rows, (s + 1) * rows
        ure = jnp.dot(f_ref[lo:hi, :], yb, preferred_element_type=F32)
        uim = jnp.dot(f_ref[l + lo:l + hi, :], yb, preferred_element_type=F32)
        kre = k_ref[0, lo:hi, :]
        kim = k_ref[0, l + lo:l + hi, :]
        pre = ure * kre - uim * kim
        pim = ure * kim + uim * kre
        if s == 0:
            first = lax.broadcasted_iota(jnp.int32, (rows, 1), 0) == 0
            pre = jnp.where(first, ure * kre, pre)
            pim = jnp.where(first, uim * kim, pim)
        p_ref[0, lo:hi, :] = pre.astype(p_ref.dtype)
        p_ref[0, l + lo:l + hi, :] = pim.astype(p_ref.dtype)


def _conv_fwd(fwd, y, y_col0, kspec, order, conv_w, conv_b, short_conv, d, tn=256):
    b, l, _ = y.shape
    n2 = fwd.shape[0]
    off = y_col0 // tn
    est = n2 * l * 2 + 2 * (l * tn * 4 + n2 * tn * 4 + n2 * tn * 2) + 6 * l * tn * 4
    return pl.pallas_call(
        functools.partial(_conv_fwd_kernel, short_conv=short_conv, n_split=2),
        out_shape=jax.ShapeDtypeStruct((b, n2, d), BF16),
        grid=(d // tn, b),
        in_specs=[pl.BlockSpec((n2, l), lambda j, bi: (0, 0), pipeline_mode=pl.Buffered(1)),
                  pl.BlockSpec((1, l, tn), lambda j, bi: (bi, 0, j + off)),
                  pl.BlockSpec((1, n2, tn), lambda j, bi: (order, 0, j)),
                  pl.BlockSpec((3, tn), lambda j, bi: (0, j + off)),
                  pl.BlockSpec((1, tn), lambda j, bi: (0, j + off))],
        out_specs=pl.BlockSpec((1, n2, tn), lambda j, bi: (bi, 0, j)),
        compiler_params=_cparams(("parallel", "parallel"), est),
        name="hyena_conv_fwd",
    )(fwd, y, kspec, conv_w, conv_b)


def _conv_inv_kernel(ft_ref, p_ref, yp_ref, gt_ref, cwy_ref, cby_ref, cwg_ref, cbg_ref, sk_ref, o_ref,
                     *, short_conv_prev):
    conv = jnp.dot(ft_ref[...], p_ref[0], preferred_element_type=F32)
    yp = yp_ref[0]
    if short_conv_prev:
        yp = _short_conv(yp, cwy_ref[...], cby_ref[...])
    gate = _short_conv(gt_ref[0], cwg_ref[...], cbg_ref[...])
    o_ref[0] = (gate * (conv + sk_ref[...] * yp)).astype(o_ref.dtype)


def _conv_inv(finv, p, yprev, yprev_col0, short_conv_prev, z, gate_col0, conv_w, conv_b, skip, out_dtype, tn=256):
    b, n2, d = p.shape
    l = n2 // 2
    offy = yprev_col0 // tn
    offg = gate_col0 // tn
    est = l * n2 * 2 + 2 * (n2 * tn * 2 + 3 * l * tn * 4) + 6 * l * tn * 4
    return pl.pallas_call(
        functools.partial(_conv_inv_kernel, short_conv_prev=short_conv_prev),
        out_shape=jax.ShapeDtypeStruct((b, l, d), out_dtype),
        grid=(d // tn, b),
        in_specs=[pl.BlockSpec((l, n2), lambda j, bi: (0, 0), pipeline_mode=pl.Buffered(1)),
                  pl.BlockSpec((1, n2, tn), lambda j, bi: (bi, 0, j)),
                  pl.BlockSpec((1, l, tn), lambda j, bi: (bi, 0, j + offy)),
                  pl.BlockSpec((1, l, tn), lambda j, bi: (bi, 0, j + offg)),
                  pl.BlockSpec((3, tn), lambda j, bi: (0, j + offy)),
                  pl.BlockSpec((1, tn), lambda j, bi: (0, j + offy)),
                  pl.BlockSpec((3, tn), lambda j, bi: (0, j + offg)),
                  pl.BlockSpec((1, tn), lambda j, bi: (0, j + offg)),
                  pl.BlockSpec((1, tn), lambda j, bi: (0, j))],
        out_specs=pl.BlockSpec((1, l, tn), lambda j, bi: (bi, 0, j)),
        compiler_params=_cparams(("parallel", "parallel"), est),
        name="hyena_conv_inv",
    )(finv, p, yprev, z, conv_w, conv_b, conv_w, conv_b, skip)


def _hyena_mixer(h2, b, l, in_w, in_b, conv_w, conv_b, f_w1, f_w23, f_w4, f_b, f_freq, skip, out_w, out_b):
    d = h2.shape[1]
    z = _matmul(h2, in_w.astype(BF16), bias=in_b, name="hyena_in").reshape(b, l, 3 * d)
    filt = _hyena_filters(l, d, f_w1, f_w23, f_w4, f_b, f_freq)
    fwd, finv = _dft_matrices(l)
    kspec = _spectrum(fwd, filt, d)
    skip = skip.astype(F32)
    p0 = _conv_fwd(fwd, z, 0, kspec, 0, conv_w, conv_b.reshape(1, -1), True, d)
    y1 = _conv_inv(finv, p0, z, 0, True, z, d, conv_w, conv_b.reshape(1, -1), skip[0:1], F32)
    p1 = _conv_fwd(fwd, y1, 0, kspec, 1, conv_w, conv_b.reshape(1, -1), False, d)
    y2 = _conv_inv(finv, p1, y1, 0, False, z, 2 * d, conv_w, conv_b.reshape(1, -1), skip[1:2], BF16)
    return y2.reshape(b * l, d), out_w.astype(BF16), out_b


def _lora_in(w):
    pad = lambda m: jnp.pad(m, ((0, 0), (0, LANES - m.shape[1])))
    return jnp.concatenate([pad(w[0]), pad(w[1])], axis=1)


def _lora_out(w):
    return jnp.pad(w, ((0, 0), (0, LANES - w.shape[1]), (0, 0)))


def _rwkv_mixer(x, ctx, mod3, layer, g0, mu, w_r, w_k, w_v, w_o, dec_w0, dec_w1, dec_w2, a0, a1, a2,
                g1, g2, k_k, k_a, r_k, lnx_w, lnx_b):
    b, l, d = x.shape
    n_ctx = ctx.shape[1]
    s = n_ctx + l
    bf = lambda w: w.astype(BF16)
    assert dec_w1.shape[2] <= LANES and a1.shape[2] <= LANES
    xr, xk, xv, lw1, a1o, g1o = _pre_rwkv(x, ctx, mod3, layer, g0, mu, bf(_lora_in(dec_w1)), bf(_lora_in(a1)),
                                          bf(g1))
    r = _matmul(xr.reshape(b * s, d), bf(w_r), out_dtype=BF16, name="rwkv_r").reshape(b, s, d)
    k = _matmul(xk.reshape(b * s, d), bf(w_k), out_dtype=BF16, name="rwkv_k").reshape(b, s, d)
    v = _matmul(xv.reshape(b * s, d), bf(w_v), out_dtype=BF16, name="rwkv_v").reshape(b, s, d)
    zz = _wkv(r, k, v, lw1, a1o, g1o, bf(_lora_out(dec_w2)), bf(_lora_out(a2)), bf(g2), dec_w0, a0, k_k, k_a, r_k,
              lnx_w, lnx_b, n_ctx)
    return zz.reshape(b * l, d), bf(w_o), jnp.zeros((d,), F32)


def kernel(x, c, ctx, c_ctx, ada_w, ada_b, norm_g, mlp_up, mlp_down, rw_mu, rw_w_r, rw_w_k, rw_w_v, rw_w_o, rw_dec_w0, rw_dec_w1, rw_dec_w2, rw_a0, rw_a1, rw_a2, rw_g1, rw_g2, rw_k_k, rw_k_a, rw_r_k, rw_lnx_w, rw_lnx_b, hy_in_w, hy_in_b, hy_conv_w, hy_conv_b, hy_f_w1, hy_f_w23, hy_f_w4, hy_f_b, hy_f_freq, hy_skip, hy_out_w, hy_out_b):
    b, l, d = x.shape
    depth = ada_w.shape[0]
    assert b < MOD_ROWS
    c_rows = jnp.zeros((MOD_ROWS, d), F32).at[:b].set(c).at[b].set(c_ctx)
    mod3 = _ada_mod(c_rows, ada_w, ada_b)
    x2 = x.reshape(b * l, d)
    xc = ctx
    h_next = None
    for i in range(depth):
        kind, j = i % 2, i // 2
        ctx_live = any(q % 2 == 0 for q in range(i + 1, depth))
        assert not ctx_live, "context-stream update is not implemented for this depth"
        if kind == 0:
            mixed = _rwkv_mixer(x2.reshape(b, l, d), xc, mod3, i, norm_g[i, 0], rw_mu[j], rw_w_r[j], rw_w_k[j],
                                rw_w_v[j], rw_w_o[j], rw_dec_w0[j], rw_dec_w1[j], rw_dec_w2[j], rw_a0[j],
                                rw_a1[j], rw_a2[j], rw_g1[j], rw_g2[j], rw_k_k[j], rw_k_a[j], rw_r_k[j],
                                rw_lnx_w[j], rw_lnx_b[j])
        else:
            h = h_next if h_next is not None else _pre_norm(x2, mod3, i, norm_g[i, 0], l)
            mixed = _hyena_mixer(h, b, l, hy_in_w[j], hy_in_b[j], hy_conv_w[j], hy_conv_b[j], hy_f_w1[j],
                                 hy_f_w23[j], hy_f_w4[j], hy_f_b[j], hy_f_freq[j], hy_skip[j], hy_out_w[j],
                                 hy_out_b[j])
        x2, h2 = _proj_post(*mixed, x2, mod3, i, norm_g[i, 1], norm_g[i, 2], l,
                            name="rwkv_out_post" if kind == 0 else "hyena_out_post")
        next_is_hyena = i + 1 < depth and (i + 1) % 2 == 1
        x2, h_next = _mlp(h2, mlp_up[i].astype(BF16), mlp_down[i].astype(BF16), x2, mod3, i, norm_g[i, 3], l,
                          next_g0=norm_g[i + 1, 0] if next_is_hyena else None)
    return x2.reshape(b, l, d)
```

```python
import functools
import math

import jax
import jax.numpy as jnp
from jax import lax
from jax.experimental import pallas as pl
from jax.experimental.pallas import tpu as pltpu

F32 = jnp.float32
BF16 = jnp.bfloat16

HEAD_SIZE = 64
GRID_W = 64
N_MOD = 6
NORM_EPS = 1e-6
LNX_EPS = 64e-5
HY_FAST_DECAY = 0.3
HY_SLOW_DECAY = 1.5
HY_DECAY_TARGET = 1e-2
HY_EMB_DIM = 33
MOD_ROWS = 16
CHUNK = 64
LANES = 128
TOKEN_TILE = 256
VMEM_LIMIT_CAP = 60000 * 1024


def _cparams(sem, est_bytes):
    limit = int(min(max(2 * est_bytes, 32 * 1024 * 1024), VMEM_LIMIT_CAP))
    return pltpu.CompilerParams(dimension_semantics=sem, vmem_limit_bytes=limit)


def _largest_tile(n, cap, align):
    t = min(cap, n) // align * align
    while t > align and n % t:
        t -= align
    assert t > 0 and n % t == 0, (n, cap, align)
    return t


def _split2(x):
    hi = x.astype(BF16)
    return hi, (x - hi.astype(F32)).astype(BF16)


def _rms(x):
    return x * lax.rsqrt(jnp.mean(x * x, axis=-1, keepdims=True) + NORM_EPS)


def _norm_mod(x, g, shift, scale):
    return (_rms(x) * g) * (1.0 + scale) + shift


def _softplus(y):
    return jnp.maximum(y, 0.0) + jnp.log1p(jnp.exp(-jnp.abs(y)))


def _ada_kernel(c_ref, w_ref, b_ref, o_ref):
    c = c_ref[...]
    s_hi, s_lo = _split2(c * jax.nn.sigmoid(c))
    w_hi, w_lo = _split2(w_ref[0])
    p = jnp.dot(jnp.concatenate([s_hi, s_lo], axis=0), w_hi, preferred_element_type=F32)
    q = jnp.dot(s_hi, w_lo, preferred_element_type=F32)
    o_ref[0] = p[:MOD_ROWS] + p[MOD_ROWS:] + q + b_ref[0]


def _ada_mod(c_rows, ada_w, ada_b):
    depth, d, n = ada_w.shape
    tn = 1024
    out = pl.pallas_call(
        _ada_kernel,
        out_shape=jax.ShapeDtypeStruct((depth, MOD_ROWS, n), F32),
        grid=(depth, n // tn),
        in_specs=[pl.BlockSpec((MOD_ROWS, d), lambda l, j: (0, 0)),
                  pl.BlockSpec((1, d, tn), lambda l, j: (l, 0, j)),
                  pl.BlockSpec((1, 1, tn), lambda l, j: (l, 0, j))],
        out_specs=pl.BlockSpec((1, MOD_ROWS, tn), lambda l, j: (l, 0, j)),
        compiler_params=_cparams(("parallel", "parallel"), 2 * d * tn * 4 + 4 * d * tn),
        name="ada_mod",
    )(c_rows, ada_w, ada_b.reshape(depth, 1, n))
    return out.reshape(depth * MOD_ROWS * N_MOD, 1, d)


def _mod_row(layer, row, j):
    return (layer * MOD_ROWS + row) * N_MOD + j


def _pre_rwkv_kernel(hc_ref, sc_ref_ctx, xm_ref, xu_ref, xd_ref, g_ref, sh_ref, sc_ref, mu_ref,
                     wdec_ref, wa_ref, wg_ref, xr_ref, xk_ref, xv_ref, lw1_ref, a1_ref, g1_ref,
                     mw_scr, ma_scr, mg_scr, *, n_tiles):
    t = pl.program_id(1)
    d = xm_ref.shape[-1]
    g = g_ref[...]
    mu = mu_ref[...]
    big = {0: xr_ref, 2: xk_ref, 3: xv_ref}
    small = {1: mw_scr, 4: ma_scr, 5: mg_scr}

    def emit(h, s, lo, hi):
        xx = s - h
        for j in range(6):
            mix = (h + xx * mu[j:j + 1, lo:hi]).astype(BF16)
            if j in big:
                big[j][0, :, lo:hi] = mix
            else:
                small[j][:, lo:hi] = mix

    @pl.when(t == 0)
    def _():
        emit(hc_ref[0], sc_ref_ctx[0], 0, d)

    @pl.when(t > 0)
    def _():
        sh = sh_ref[0]
        sc = sc_ref[0]
        xm = xm_ref[0].reshape(TOKEN_TILE, d)
        hm = _norm_mod(xm, g, sh, sc)
        hu = _norm_mod(xu_ref[0, 0], g, sh, sc) * jnp.where(t > 1, 1.0, 0.0)
        hd = _norm_mod(xd_ref[0, 0], g, sh, sc) * jnp.where(t < n_tiles, 1.0, 0.0)
        col = lax.broadcasted_iota(jnp.int32, (TOKEN_TILE, 1), 0) & (GRID_W - 1)
        q = d // 4
        left = jnp.where(col != 0, pltpu.roll(hm[:, :q], 1, 0), 0.0)
        right = jnp.where(col != GRID_W - 1, pltpu.roll(hm[:, q:2 * q], TOKEN_TILE - 1, 0), 0.0)
        up = jnp.concatenate([hu[:, 2 * q:3 * q], hm[:TOKEN_TILE - GRID_W, 2 * q:3 * q]], axis=0)
        down = jnp.concatenate([hm[GRID_W:, 3 * q:], hd[:, 3 * q:]], axis=0)
        emit(hm[:, :q], left, 0, q)
        emit(hm[:, q:2 * q], right, q, 2 * q)
        emit(hm[:, 2 * q:3 * q], up, 2 * q, 3 * q)
        emit(hm[:, 3 * q:], down, 3 * q, d)

    lw1_ref[0] = jnp.tanh(jnp.dot(mw_scr[...], wdec_ref[...], preferred_element_type=F32)).astype(BF16)
    a1_ref[0] = jnp.dot(ma_scr[...], wa_ref[...], preferred_element_type=F32).astype(BF16)
    g1_ref[0] = jax.nn.sigmoid(jnp.dot(mg_scr[...], wg_ref[...], preferred_element_type=F32)).astype(BF16)


def _shift_seq(x):
    half = x.shape[-1] // 2
    p = jnp.pad(x, ((0, 0), (1, 1), (0, 0)))
    return jnp.concatenate([p[:, :-2, :half], p[:, 2:, half:]], axis=-1)


def _pre_rwkv(x, ctx, mod3, layer, g0, mu, w_dec1, w_a1, w_g1):
    b, l, d = x.shape
    ranks = (w_dec1.shape[1], w_a1.shape[1], w_g1.shape[1])
    n_ctx = ctx.shape[1]
    assert n_ctx == TOKEN_TILE and l % TOKEN_TILE == 0
    n_tiles = l // TOKEN_TILE
    rows_per_tile = TOKEN_TILE // GRID_W
    n_rows = l // GRID_W
    x4 = x.reshape(b, n_rows, GRID_W, d)
    s = n_ctx + l
    hc = _pre_norm(ctx.reshape(b * n_ctx, d), mod3, layer, g0, n_ctx, tm=n_ctx, fixed_row=b,
                   out_dtype=F32).reshape(b, n_ctx, d)
    sc = _shift_seq(hc)

    def mrow(j):
        return pl.BlockSpec((1, 1, d), lambda bi, t: (_mod_row(layer, bi, j), 0, 0))

    main = lambda bi, t: (bi, jnp.maximum(t - 1, 0), 0, 0)
    up = lambda bi, t: (bi, jnp.maximum((t - 1) * rows_per_tile - 1, 0), 0, 0)
    down = lambda bi, t: (bi, jnp.minimum(jnp.maximum(t, 1) * rows_per_tile, n_rows - 1), 0, 0)
    whole = lambda w: pl.BlockSpec(w.shape, lambda bi, t: (0, 0))
    outs = pl.pallas_call(
        functools.partial(_pre_rwkv_kernel, n_tiles=n_tiles),
        out_shape=[jax.ShapeDtypeStruct((b, s, d), BF16)] * 3
                  + [jax.ShapeDtypeStruct((b, s, rk), BF16) for rk in ranks],
        grid=(b, n_tiles + 1),
        in_specs=[pl.BlockSpec((1, n_ctx, d), lambda bi, t: (bi, 0, 0)),
                  pl.BlockSpec((1, n_ctx, d), lambda bi, t: (bi, 0, 0)),
                  pl.BlockSpec((1, rows_per_tile, GRID_W, d), main),
                  pl.BlockSpec((1, 1, GRID_W, d), up),
                  pl.BlockSpec((1, 1, GRID_W, d), down),
                  pl.BlockSpec((1, d), lambda bi, t: (0, 0)),
                  mrow(0), mrow(1),
                  pl.BlockSpec((6, d), lambda bi, t: (0, 0)),
                  whole(w_dec1), whole(w_a1), whole(w_g1)],
        out_specs=[pl.BlockSpec((1, TOKEN_TILE, d), lambda bi, t: (bi, t, 0))] * 3
                  + [pl.BlockSpec((1, TOKEN_TILE, rk), lambda bi, t: (bi, t, 0)) for rk in ranks],
        scratch_shapes=[pltpu.VMEM((TOKEN_TILE, d), BF16)] * 3,
        compiler_params=_cparams(("parallel", "arbitrary"),
                                 2 * (3 * TOKEN_TILE * d * 4 + 2 * GRID_W * d * 4 + 3 * TOKEN_TILE * d * 2
                                      + d * sum(ranks) * 2) + 3 * TOKEN_TILE * d * 2),
        name="rwkv_pre",
    )(hc, sc, x4, x4, x4, g0.reshape(1, d), mod3, mod3, mu, w_dec1, w_a1, w_g1)
    return outs


def _mm_kernel(a_ref, w_ref, *rest, has_bias):
    o_ref = rest[-1]
    acc = jnp.dot(a_ref[...], w_ref[...], preferred_element_type=F32)
    if has_bias:
        acc = acc + rest[0][...]
    o_ref[...] = acc.astype(o_ref.dtype)


def _matmul(a, w, bias=None, out_dtype=F32, tm=1024, tn=1024, name="matmul"):
    m, k = a.shape
    n = w.shape[1]
    tm = _largest_tile(m, tm, 8)
    tn = _largest_tile(n, tn, LANES)
    in_specs = [pl.BlockSpec((tm, k), lambda i, j: (i, 0)),
                pl.BlockSpec((k, tn), lambda i, j: (0, j))]
    args = [a, w]
    if bias is not None:
        in_specs.append(pl.BlockSpec((1, tn), lambda i, j: (0, j)))
        args.append(bias.reshape(1, n).astype(F32))
    est = 2 * (tm * k * 2 + k * tn * 2 + tm * tn * jnp.dtype(out_dtype).itemsize) + tm * tn * 4
    return pl.pallas_call(
        functools.partial(_mm_kernel, has_bias=bias is not None),
        out_shape=jax.ShapeDtypeStruct((m, n), out_dtype),
        grid=(m // tm, n // tn),
        in_specs=in_specs,
        out_specs=pl.BlockSpec((tm, tn), lambda i, j: (i, j)),
        compiler_params=_cparams(("parallel", "parallel"), est),
        name=name,
    )(*args)


def _seg_sum(x, ones_bd):
    r = x.shape[0]
    p = jnp.dot(jnp.concatenate(_split2(x), axis=0), ones_bd, preferred_element_type=F32)
    return p[:r] + p[r:]


def _bd(x, head0):
    return jnp.concatenate([jnp.where(head0, x, 0.0), jnp.where(head0, 0.0, x)], axis=0)


def _wkv_constants(rev):
    c = CHUNK
    shift = int(math.log2(c))
    head0 = lax.broadcasted_iota(jnp.int32, (1, LANES), 1) < HEAD_SIZE
    row2 = lax.broadcasted_iota(jnp.int32, (2 * c, 2 * c), 0)
    col2 = lax.broadcasted_iota(jnp.int32, (2 * c, 2 * c), 1)
    ones_bd = jnp.where((row2 >> shift) == (col2 >> shift), 1.0, 0.0).astype(BF16)
    eye = jnp.where(row2 == col2, 1.0, 0.0).astype(F32)
    rowc = lax.broadcasted_iota(jnp.int32, (c, c), 0)
    colc = lax.broadcasted_iota(jnp.int32, (c, c), 1)
    tri = jnp.where(colc >= rowc if rev else colc <= rowc, 1.0, 0.0).astype(BF16)
    tc = lax.broadcasted_iota(jnp.int32, (c, 2 * c), 0)
    sc = lax.broadcasted_iota(jnp.int32, (c, 2 * c), 1) & (c - 1)
    eye_c = jnp.where(tc == sc, 1.0, 0.0).astype(F32)
    if rev:
        tc, sc = sc, tc
    merge = tuple(((tc >> lv) == (sc >> lv) + 1) & ((tc >> (lv + 1)) == (sc >> (lv + 1))) for lv in range(shift))
    return head0, ones_bd, tri, sc < tc, sc <= tc, eye, merge, eye_c


def _each(fn, *lists):
    return [fn(*args) for args in zip(*lists)]


def _mm(a, b):
    return jnp.dot(a.astype(BF16), b.astype(BF16), preferred_element_type=F32)


def _wkv_prepare(probs, k_a, hooks=()):
    hooks = list(hooks)

    def run_hook():
        if hooks:
            hooks.pop(0)()

    revs, rs, ks, kks, bd_vs, lwxs, axs, w0s, a0s, csts = [list(t) for t in zip(*probs)]
    head0 = csts[0][0]
    eye = csts[0][5]
    c = rs[0].shape[0]
    c2 = 2 * c
    logw = _each(lambda w0, lwx: -jnp.exp(-_softplus(-(w0 + lwx)) - 0.5), w0s, lwxs)
    a = _each(lambda a0, ax: jax.nn.sigmoid(a0 + ax), a0s, axs)
    kd = _each(lambda k, a_: k * (1.0 + (a_ - 1.0) * k_a), ks, a)
    b = _each(lambda kk, a_: kk * a_, kks, a)

    def cumsum(lw, cst):
        c2_ = jnp.dot(cst[2], jnp.concatenate(_split2(lw), axis=1), preferred_element_type=F32)
        return c2_[:, :LANES] + c2_[:, LANES:]

    cum = _each(cumsum, logw, csts)
    total = _each(lambda cm, rev: cm[0:1] if rev else cm[c - 1:c], cum, revs)
    kap_t = _each(lambda kk, cm, lw: kk * jnp.exp(cm - lw), kks, cum, logw)
    r_t = _each(lambda r, cm: r * jnp.exp(cm), rs, cum)
    igam = _each(lambda cm: jnp.exp(-cm), cum)
    tail = _each(lambda t, cm: jnp.exp(t - cm), total, cum)

    bd_kap = _each(lambda x: _bd(x, head0).astype(BF16), kap_t)
    lhs = _each(lambda kap, rt: jnp.concatenate([kap, rt], axis=0).astype(BF16), kap_t, r_t)
    rhs = _each(lambda kd_, b_, ig: jnp.concatenate([_bd(kd_ * ig, head0), _bd(b_ * ig, head0)],
                                                    axis=0).astype(BF16), kd, b, igam)
    bct = _each(lambda b_, tl: _bd(b_ * tl, head0).T.astype(BF16), b, tail)
    kct = _each(lambda kd_, tl: _bd(kd_ * tl, head0).T.astype(BF16), kd, tail)
    gmat = _each(lambda l_, r_: lax.dot_general(l_, r_, (((1,), (1,)), ((), ())),
                                                preferred_element_type=F32), lhs, rhs)
    a_kb = _each(lambda g, cst: jnp.where(cst[3], g[:c, c2:], 0.0), gmat, csts)
    a_rb = _each(lambda g, cst: jnp.where(cst[4], g[c:, c2:], 0.0).astype(BF16), gmat, csts)
    a_kr = _each(lambda g, cst: jnp.concatenate([jnp.where(cst[3], g[:c, :c2], 0.0),
                                                 jnp.where(cst[4], g[c:, :c2], 0.0)], axis=0).astype(BF16),
                 gmat, csts)
    av = _each(_mm, a_kr, bd_vs)
    x0 = _each(lambda kap, av_: jnp.concatenate([kap, _bd(av_[:c], head0).astype(BF16)], axis=1), bd_kap, av)
    ov0 = _each(lambda av_: av_[c:], av)
    run_hook()

    tinv = _each(lambda akb, cst: cst[7] - jnp.where(cst[6][0], akb, 0.0), a_kb, csts)
    for lv in range(1, len(csts[0][6])):
        y = _each(lambda akb, x, cst: _mm(jnp.where(cst[6][lv], akb, 0.0), _bd(x, head0)), a_kb, tinv, csts)
        run_hook()
        tinv = _each(lambda x, y_: x - _mm(x, _bd(y_, head0)), tinv, y)
        run_hook()

    wu_c = _each(_mm, tinv, x0)
    wu = _each(lambda w_: jnp.concatenate([_bd(w_[:, :LANES], head0), _bd(w_[:, LANES:], head0)],
                                          axis=1).astype(BF16), wu_c)
    run_hook()
    rb = _each(_mm, a_rb, wu)
    run_hook()
    mn = _each(_mm, bct, wu)
    while hooks:
        run_hook()
    kv = _each(_mm, kct, bd_vs)
    out = []
    for i in range(len(probs)):
        rk = r_t[i] - rb[i][:, :LANES]
        ov = ov0[i] - rb[i][:, LANES:]
        m_mat = jnp.where(eye > 0.0, jnp.exp(total[i]), 0.0) - mn[i][:, :LANES]
        n_mat = kv[i] - mn[i][:, LANES:]
        out.append((jnp.concatenate([rk, m_mat], axis=0).astype(BF16), ov, n_mat))
    return out


def _wkv_apply(rm_lhs, ov, n_mat, h_state):
    c = ov.shape[0]
    rm = jnp.dot(rm_lhs, h_state.astype(BF16), preferred_element_type=F32)
    return rm[:c] + ov, rm[c:] + n_mat


def _wkv_kernel(r_ref, k_ref, v_ref, lw1_ref, a1_ref, g1_ref, wdec_ref, wa_ref, wg_ref,
                w0_ref, a0_ref, kk_ref, ka_ref, rk_ref, lnw_ref, lnb_ref,
                z_ref, of_scr, ob_scr, rm_scr, ov_scr, n_scr, *, n_ctx_chunks, n_chunks, unroll):
    c = CHUNK
    cst_f = _wkv_constants(False)
    cst_b = _wkv_constants(True)
    head0, ones_bd = cst_f[0], cst_f[1]

    w0f, w0b = w0_ref[0:1, :], w0_ref[1:2, :]
    a0f, a0b = a0_ref[0:1, :], a0_ref[1:2, :]
    k_k = kk_ref[...]
    k_a = ka_ref[...]

    def load(ref, rows):
        return ref[0, rows, :].astype(F32)

    def lora(x_ref, w_ref, rows, rev):
        d0 = int(rev) * LANES
        return jnp.dot(x_ref[0, rows, d0:d0 + LANES], w_ref[int(rev)], preferred_element_type=F32)

    def chunk_of(i, rev):
        if not rev:
            return i
        return jnp.where(i < n_ctx_chunks, n_ctx_chunks - 1 - i, n_chunks + n_ctx_chunks - 1 - i)

    def prepare_group(grp, hooks=()):
        probs, slots = [], []
        for u in range(unroll):
            for rev, w0, a0, cst in ((False, w0f, a0f, cst_f), (True, w0b, a0b, cst_b)):
                ci = chunk_of(grp * unroll + u, rev)
                rows = pl.ds(pl.multiple_of(ci * c, c), c)
                k = load(k_ref, rows)
                kk0 = k * k_k
                kk = kk0 * lax.rsqrt(jnp.maximum(_seg_sum(kk0 * kk0, ones_bd), 1e-24))
                bd_v = _bd(load(v_ref, rows), head0).astype(BF16)
                probs.append((rev, load(r_ref, rows), k, kk, bd_v, lora(lw1_ref, wdec_ref, rows, rev),
                              lora(a1_ref, wa_ref, rows, rev), w0, a0, cst))
                slots.append(ci + n_chunks * int(rev))
        for slot, (rm_lhs, ov, n_mat) in zip(slots, _wkv_prepare(probs, k_a, hooks)):
            rm_scr[slot] = rm_lhs
            ov_scr[slot] = ov
            n_scr[slot] = n_mat

    def state_steps(grp, state):
        def step(i):
            for rev, o_scr in ((False, of_scr), (True, ob_scr)):
                ci = chunk_of(i, rev)
                slot = ci + n_chunks * int(rev)
                o, state[int(rev)] = _wkv_apply(rm_scr[slot], ov_scr[slot], n_scr[slot], state[int(rev)])
                o_scr[pl.ds(pl.multiple_of(ci * c, c), c), :] = o
        return [functools.partial(step, grp * unroll + u) for u in range(unroll)]

    n_groups = n_chunks // unroll
    prepare_group(0)

    def body(grp, carry):
        state = list(carry)
        prepare_group(grp, state_steps(grp - 1, state))
        return tuple(state)

    zero = jnp.zeros((2 * c, LANES), F32)
    state = list(lax.fori_loop(1, n_groups, body, (zero, zero)))

    r_k = rk_ref[...]
    lnw = lnw_ref[...]
    lnb = lnb_ref[...]
    blk = TOKEN_TILE
    n_ctx = n_ctx_chunks * c
    inv_n = 1.0 / HEAD_SIZE

    def read_blocks(blocks, hooks):
        hooks = list(hooks)

        def run_hook():
            if hooks:
                hooks.pop(0)()

        rows = [pl.ds(n_ctx + j * blk, blk) for j in blocks]
        seg = lambda xs: [_seg_sum(x, ones_bd) for x in xs]
        a_f = [lora(a1_ref, wa_ref, rw, False) for rw in rows]
        a_b = [lora(a1_ref, wa_ref, rw, True) for rw in rows]
        gate = [jnp.dot(g1_ref[0, rw, :], wg_ref[...], preferred_element_type=F32) for rw in rows]
        run_hook()
        o = [of_scr[rw, :] + ob_scr[rw, :] for rw in rows]
        dev = [o_ - m_ * inv_n for o_, m_ in zip(o, seg(o))]
        run_hook()
        var = seg([d_ * d_ for d_ in dev])
        run_hook()
        kd_sum = [load(k_ref, rw) * (2.0 + (jax.nn.sigmoid(a0f + af_) + jax.nn.sigmoid(a0b + ab_) - 2.0) * k_a)
                  for rw, af_, ab_ in zip(rows, a_f, a_b)]
        rkk = seg([load(r_ref, rw) * kd_ * r_k for rw, kd_ in zip(rows, kd_sum)])
        while hooks:
            run_hook()
        for j, rw, dev_, var_, rkk_, gate_ in zip(blocks, rows, dev, var, rkk, gate):
            on = dev_ * lax.rsqrt(var_ * inv_n + LNX_EPS) * lnw + lnb
            z = (on + rkk_ * load(v_ref, rw)) * gate_
            z_ref[0, pl.ds(j * blk, blk), :] = z.astype(z_ref.dtype)

    per_blk = blk // c
    n_blk = (n_chunks - n_ctx_chunks) // per_blk
    first = lambda j: n_ctx_chunks + j * per_blk
    ready = {j: max(first(j) + per_blk - 1, n_chunks + n_ctx_chunks - 1 - first(j)) for j in range(n_blk)}
    order = sorted(range(n_blk), key=lambda j: ready[j])
    pending = [((n_groups - 1) * unroll + u, fn) for u, fn in enumerate(state_steps(n_groups - 1, state))]
    hooks_per_read = 4
    for pos in range(0, n_blk, 2):
        blocks = order[pos:pos + 2]
        need = max(ready[j] for j in blocks)
        while pending and pending[0][0] <= need:
            pending.pop(0)[1]()
        ride, pending = pending[:hooks_per_read], pending[hooks_per_read:]
        read_blocks(blocks, [fn for _, fn in ride])
    for _, fn in pending:
        fn()


def _wkv(r, k, v, lw1, a1, g1, w_dec, w_a, w_g, dec_w0, a0, k_k, k_a, r_k, lnx_w, lnx_b, n_ctx):
    b, s, d = r.shape
    l = s - n_ctx
    npair = d // LANES
    seq = lambda bi, p: (bi, 0, p)
    vec = lambda bi, p: (0, p)
    sblk = pl.BlockSpec((1, s, LANES), seq)
    rank_blk = pl.BlockSpec((1, s, 2 * LANES), lambda bi, p: (bi, 0, 0))
    up_blk = pl.BlockSpec((2, LANES, LANES), lambda bi, p: (0, 0, p))
    n_chunks = s // CHUNK
    unroll = next(u for u in (9, 6, 4, 2, 1) if n_chunks % u == 0)
    scratch = [pltpu.VMEM((s, LANES), F32), pltpu.VMEM((s, LANES), F32),
               pltpu.VMEM((2 * n_chunks, 3 * CHUNK, LANES), BF16),
               pltpu.VMEM((2 * n_chunks, CHUNK, LANES), F32),
               pltpu.VMEM((2 * n_chunks, 2 * CHUNK, LANES), F32)]
    est = (2 * (3 * s * LANES * 2 + 2 * s * 2 * LANES * 2 + s * g1.shape[2] * 2 + l * LANES * 2) + 2 * s * LANES * 4
           + 2 * n_chunks * CHUNK * LANES * (3 * 2 + 4 + 2 * 4))
    return pl.pallas_call(
        functools.partial(_wkv_kernel, n_ctx_chunks=n_ctx // CHUNK, n_chunks=n_chunks, unroll=unroll),
        out_shape=jax.ShapeDtypeStruct((b, l, d), BF16),
        grid=(b, npair),
        in_specs=[sblk, sblk, sblk, rank_blk, rank_blk,
                  pl.BlockSpec((1, s, g1.shape[2]), lambda bi, p: (bi, 0, 0)), up_blk, up_blk,
                  pl.BlockSpec((g1.shape[2], LANES), lambda bi, p: (0, p)),
                  pl.BlockSpec((2, LANES), vec), pl.BlockSpec((2, LANES), vec),
                  pl.BlockSpec((1, LANES), vec), pl.BlockSpec((1, LANES), vec), pl.BlockSpec((1, LANES), vec),
                  pl.BlockSpec((1, LANES), vec), pl.BlockSpec((1, LANES), vec)],
        out_specs=pl.BlockSpec((1, l, LANES), seq),
        scratch_shapes=scratch,
        compiler_params=_cparams(("parallel", "parallel"), est),
        name="wkv_scan",
    )(r, k, v, lw1, a1, g1, w_dec, w_a, w_g, dec_w0, a0, k_k.reshape(1, d), k_a.reshape(1, d), r_k.reshape(1, d),
      lnx_w.reshape(1, d), lnx_b.reshape(1, d))


def _proj_post_kernel(a_ref, w_ref, b_ref, x_ref, g1_ref, g2_ref, gate_ref, sh_ref, sc_ref, xo_ref, h_ref):
    y = jnp.dot(a_ref[...], w_ref[...], preferred_element_type=F32) + b_ref[...]
    x = x_ref[...] + gate_ref[0] * (_rms(y) * g1_ref[...])
    xo_ref[...] = x
    h_ref[...] = _norm_mod(x, g2_ref[...], sh_ref[0], sc_ref[0]).astype(h_ref.dtype)


def _proj_post(a, w, bias, x2, mod3, layer, g1, g2, l, tm=512, name="proj_post"):
    m, k = a.shape
    d = w.shape[1]
    per_b = l // tm
    row = lambda j: pl.BlockSpec((1, 1, d), lambda i: (_mod_row(layer, i // per_b, j), 0, 0))
    tile = pl.BlockSpec((tm, d), lambda i: (i, 0))
    vec = pl.BlockSpec((1, d), lambda i: (0, 0))
    est = k * d * 2 + 2 * (tm * k * 2 + tm * d * (4 + 4 + 2)) + 2 * tm * d * 4
    return pl.pallas_call(
        _proj_post_kernel,
        out_shape=[jax.ShapeDtypeStruct((m, d), F32), jax.ShapeDtypeStruct((m, d), BF16)],
        grid=(m // tm,),
        in_specs=[pl.BlockSpec((tm, k), lambda i: (i, 0)),
                  pl.BlockSpec((k, d), lambda i: (0, 0), pipeline_mode=pl.Buffered(1)),
                  vec, tile, vec, vec, row(2), row(3), row(4)],
        out_specs=[tile, tile],
        compiler_params=_cparams(("parallel",), est),
        name=name,
    )(a, w, bias.reshape(1, d).astype(F32), x2, g1.reshape(1, d), g2.reshape(1, d), mod3, mod3, mod3)


def _mlp_kernel(h_ref, wu_ref, wd_ref, x_ref, g_ref, gate_ref, *rest, with_next):
    if with_next:
        gn_ref, shn_ref, scn_ref, o_ref, hn_ref, acc_ref = rest
    else:
        o_ref, acc_ref = rest
    kf = pl.program_id(1)

    @pl.when(kf == 0)
    def _():
        acc_ref[...] = jnp.zeros_like(acc_ref)

    u = jnp.dot(h_ref[...], wu_ref[...], preferred_element_type=F32)
    u = jnp.square(jnp.maximum(u, 0.0)).astype(BF16)
    acc_ref[...] += jnp.dot(u, wd_ref[...], preferred_element_type=F32)

    @pl.when(kf == pl.num_programs(1) - 1)
    def _():
        x = x_ref[...] + gate_ref[0] * (_rms(acc_ref[...]) * g_ref[...])
        o_ref[...] = x
        if with_next:
            hn_ref[...] = _norm_mod(x, gn_ref[...], shn_ref[0], scn_ref[0]).astype(hn_ref.dtype)


def _mlp(h2, w_up, w_down, x2, mod3, layer, g3, l, next_g0=None, tm=512, tf=1024):
    m, d = h2.shape
    dff = w_up.shape[1]
    per_b = l // tm
    with_next = next_g0 is not None
    row = lambda lay, j: pl.BlockSpec((1, 1, d), lambda i, f: (_mod_row(lay, i // per_b, j), 0, 0))
    tile = pl.BlockSpec((tm, d), lambda i, f: (i, 0))
    vec = pl.BlockSpec((1, d), lambda i, f: (0, 0))
    in_specs = [tile, pl.BlockSpec((d, tf), lambda i, f: (0, f)), pl.BlockSpec((tf, d), lambda i, f: (f, 0)),
                tile, vec, row(layer, 5)]
    args = [h2, w_up, w_down, x2, g3.reshape(1, d), mod3]
    out_shape = [jax.ShapeDtypeStruct((m, d), F32)]
    if with_next:
        in_specs += [vec, row(layer + 1, 0), row(layer + 1, 1)]
        args += [next_g0.reshape(1, d), mod3, mod3]
        out_shape.append(jax.ShapeDtypeStruct((m, d), BF16))
    est = 2 * (tm * d * 2 + 2 * d * tf * 2 + 2 * tm * d * 4 + tm * d * 2) + tm * d * 4 + tm * tf * 6
    outs = pl.pallas_call(
        functools.partial(_mlp_kernel, with_next=with_next),
        out_shape=out_shape,
        grid=(m // tm, dff // tf),
        in_specs=in_specs,
        out_specs=[tile] * len(out_shape),
        scratch_shapes=[pltpu.VMEM((tm, d), F32)],
        compiler_params=_cparams(("parallel", "arbitrary"), est),
        name="mlp",
    )(*args)
    return outs if with_next else (outs[0], None)


def _pre_norm_kernel(x_ref, g_ref, sh_ref, sc_ref, h_ref):
    h_ref[...] = _norm_mod(x_ref[...], g_ref[...], sh_ref[0], sc_ref[0]).astype(h_ref.dtype)


def _pre_norm(x2, mod3, layer, g0, l, tm=512, fixed_row=None, out_dtype=BF16):
    m, d = x2.shape
    per_b = l // tm
    if fixed_row is None:
        row = lambda j: pl.BlockSpec((1, 1, d), lambda i: (_mod_row(layer, i // per_b, j), 0, 0))
    else:
        row = lambda j: pl.BlockSpec((1, 1, d), lambda i: (_mod_row(layer, fixed_row, j), 0, 0))
    tile = pl.BlockSpec((tm, d), lambda i: (i, 0))
    return pl.pallas_call(
        _pre_norm_kernel,
        out_shape=jax.ShapeDtypeStruct((m, d), out_dtype),
        grid=(m // tm,),
        in_specs=[tile, pl.BlockSpec((1, d), lambda i: (0, 0)), row(0), row(1)],
        out_specs=tile,
        compiler_params=_cparams(("parallel",), 2 * tm * d * 6),
        name="pre_norm",
    )(x2, g0.reshape(1, d), mod3, mod3)


def _filter_kernel(z_ref, w1_ref, w2_ref, w3_ref, b_ref, fr_ref, w4_ref, t_ref, dl_ref, o_ref, hid_ref):
    hp = lax.Precision.HIGHEST

    @pl.when(pl.program_id(0) == 0)
    def _():
        b = b_ref[...]
        fr = fr_ref[...]
        z = jnp.sin(fr[0:1] * (jnp.dot(z_ref[...], w1_ref[...], precision=hp, preferred_element_type=F32) + b[0:1]))
        z = jnp.sin(fr[1:2] * (jnp.dot(z, w2_ref[...], precision=hp, preferred_element_type=F32) + b[1:2]))
        hid_ref[...] = jnp.sin(fr[2:3] * (jnp.dot(z, w3_ref[...], precision=hp, preferred_element_type=F32)
                                          + b[2:3]))

    filt = jnp.dot(hid_ref[...], w4_ref[...], precision=hp, preferred_element_type=F32)
    o_ref[...] = filt * jnp.exp(-t_ref[...] * dl_ref[...])


def _hyena_filters(l, d, f_w1, f_w23, f_w4, f_b, f_freq):
    t = jnp.linspace(0.0, 1.0, l, dtype=F32)[:, None]
    bands = (HY_EMB_DIM - 1) // 2
    freqs = jnp.linspace(1e-4, bands - 1, bands, dtype=F32)[None, :]
    ang = (2.0 * math.pi / l) * jnp.arange(l, dtype=F32)[:, None] * freqs
    z = jnp.concatenate([t, jnp.cos(ang), -jnp.sin(ang)], axis=-1)
    pad = lambda a_, r, c: jnp.pad(a_.astype(F32), ((0, r - a_.shape[0]), (0, c - a_.shape[1])))
    zp = pad(z, l, LANES)
    w1 = pad(f_w1, LANES, LANES)
    w2 = pad(f_w23[0], LANES, LANES)
    w3 = pad(f_w23[1], LANES, LANES)
    bb = pad(f_b, 8, LANES)
    fr = pad(f_freq, 8, LANES)
    n = f_w4.shape[1]
    w4 = pad(f_w4, LANES, n)
    max_decay = math.log(HY_DECAY_TARGET) / HY_FAST_DECAY
    min_decay = math.log(HY_DECAY_TARGET) / HY_SLOW_DECAY
    deltas = jnp.abs(jnp.linspace(min_decay, max_decay, d, dtype=F32))[None, :]
    tn = _largest_tile(d, 1024, LANES)
    per_d = d // tn
    sq = pl.BlockSpec((LANES, LANES), lambda j: (0, 0))
    small = pl.BlockSpec((8, LANES), lambda j: (0, 0))
    return pl.pallas_call(
        _filter_kernel,
        out_shape=jax.ShapeDtypeStruct((l, n), F32),
        grid=(n // tn,),
        in_specs=[pl.BlockSpec((l, LANES), lambda j: (0, 0)), sq, sq, sq, small, small,
                  pl.BlockSpec((LANES, tn), lambda j: (0, j)),
                  pl.BlockSpec((l, 1), lambda j: (0, 0)),
                  pl.BlockSpec((1, tn), lambda j: (0, j % per_d))],
        out_specs=pl.BlockSpec((l, tn), lambda j: (0, j)),
        scratch_shapes=[pltpu.VMEM((l, LANES), F32)],
        compiler_params=_cparams(("arbitrary",), 4 * l * tn * 4),
        name="hyena_filters",
    )(zp, w1, w2, w3, bb, fr, w4, t, deltas)


def _dft_matrix(l):
    n = 2 * l
    k = jnp.arange(l, dtype=jnp.int32)[:, None]
    t = jnp.arange(l, dtype=jnp.int32)[None, :]
    ang = ((k * t) % n).astype(F32) * (2.0 * math.pi / n)
    cos = jnp.cos(ang)
    msin = jnp.where(k == 0, jnp.where(t % 2 == 0, 1.0, -1.0), -jnp.sin(ang))
    return jnp.concatenate([cos, msin], axis=0).astype(BF16)


def _short_conv(z, cw, cb):
    n = z.shape[0]
    row = lax.broadcasted_iota(jnp.int32, (n, 1), 0)
    prev = jnp.where(row != 0, pltpu.roll(z, 1, 0), 0.0)
    nxt = jnp.where(row != n - 1, pltpu.roll(z, n - 1, 0), 0.0)
    return prev * cw[0:1] + z * cw[1:2] + nxt * cw[2:3] + cb


def _spec_kernel(f_ref, hf_ref, hb_ref, o_ref):
    l, tn = hf_ref.shape
    row = lax.broadcasted_iota(jnp.int32, (l, 1), 0)
    hb = jnp.where(row == 0, 0.0, hb_ref[...])
    h2 = jnp.concatenate([hf_ref[...], hb], axis=1).astype(BF16)
    u = jnp.dot(f_ref[...], h2, preferred_element_type=F32)
    row2 = lax.broadcasted_iota(jnp.int32, (2 * l, 1), 0)
    sign = jnp.where(row2 > l, -1.0, 1.0)
    scale = jnp.where((row2 == 0) | (row2 == l), 1.0 / (2 * l), 2.0 / (2 * l))
    o_ref[0] = (u[:, :tn] + sign * u[:, tn:]) * scale


def _spectrum(fwd, filt, d, tn=256):
    l = filt.shape[0]
    n2 = fwd.shape[0]
    orders = filt.shape[1] // (2 * d)
    per = d // tn
    return pl.pallas_call(
        _spec_kernel,
        out_shape=jax.ShapeDtypeStruct((orders, n2, d), F32),
        grid=(orders, per),
        in_specs=[pl.BlockSpec((n2, l), lambda o, j: (0, 0), pipeline_mode=pl.Buffered(1)),
                  pl.BlockSpec((l, tn), lambda o, j: (0, o * per + j)),
                  pl.BlockSpec((l, tn), lambda o, j: (0, (orders + o) * per + j))],
        out_specs=pl.BlockSpec((1, n2, tn), lambda o, j: (o, 0, j)),
        compiler_params=_cparams(("parallel", "parallel"), n2 * l * 2 + 2 * (2 * l * tn * 4 + n2 * tn * 4)
                                 + 3 * n2 * tn * 4),
        name="filter_spectrum",
    )(fwd, filt, filt)


def _conv_fwd_kernel(f_ref, y_ref, k_ref, cw_ref, cb_ref, p_ref, *, short_conv, n_split):
    l = y_ref.shape[1]
    y = y_ref[0]
    if short_conv:
        y = _short_conv(y, cw_ref[...], cb_ref[...])
    yb = y.astype(BF16)
    rows = l // n_split
    for s in range(n_split):
        lo, hi = s * rows, (s + 1) * rows
        ure = jnp.dot(f_ref[lo:hi, :], yb, preferred_element_type=F32)
        uim = jnp.dot(f_ref[l + lo:l + hi, :], yb, preferred_element_type=F32)
        kre = k_ref[0, lo:hi, :]
        kim = k_ref[0, l + lo:l + hi, :]
        pre = ure * kre - uim * kim
        pim = ure * kim + uim * kre
        if s == 0:
            first = lax.broadcasted_iota(jnp.int32, (rows, 1), 0) == 0
            pre = jnp.where(first, ure * kre, pre)
            pim = jnp.where(first, uim * kim, pim)
        p_ref[0, lo:hi, :] = pre.astype(p_ref.dtype)
        p_ref[0, l + lo:l + hi, :] = pim.astype(p_ref.dtype)


def _conv_fwd(fwd, y, y_col0, kspec, order, conv_w, conv_b, short_conv, d, tn=256):
    b, l, _ = y.shape
    n2 = fwd.shape[0]
    off = y_col0 // tn
    est = n2 * l * 2 + 2 * (l * tn * 4 + n2 * tn * 4 + n2 * tn * 2) + 6 * l * tn * 4
    return pl.pallas_call(
        functools.partial(_conv_fwd_kernel, short_conv=short_conv, n_split=2),
        out_shape=jax.ShapeDtypeStruct((b, n2, d), BF16),
        grid=(d // tn, b),
        in_specs=[pl.BlockSpec((n2, l), lambda j, bi: (0, 0), pipeline_mode=pl.Buffered(1)),
                  pl.BlockSpec((1, l, tn), lambda j, bi: (bi, 0, j + off)),
                  pl.BlockSpec((1, n2, tn), lambda j, bi: (order, 0, j)),
                  pl.BlockSpec((3, tn), lambda j, bi: (0, j + off)),
                  pl.BlockSpec((1, tn), lambda j, bi: (0, j + off))],
        out_specs=pl.BlockSpec((1, n2, tn), lambda j, bi: (bi, 0, j)),
        compiler_params=_cparams(("parallel", "parallel"), est),
        name="hyena_conv_fwd",
    )(fwd, y, kspec, conv_w, conv_b)


def _conv_inv_kernel(f_ref, p_ref, yp_ref, gt_ref, cwy_ref, cby_ref, cwg_ref, cbg_ref, sk_ref, o_ref,
                     *, short_conv_prev):
    l = yp_ref.shape[1]
    row = lax.broadcasted_iota(jnp.int32, (l, 1), 0)
    p_im = p_ref[0, l:, :]
    sine = jnp.dot(f_ref[l:, :], p_im, preferred_element_type=F32)
    nyquist = jnp.where((row & 1) == 0, 1.0, -1.0) * p_im[0:1, :].astype(F32)
    conv = (jnp.dot(f_ref[:l, :], p_ref[0, :l, :], preferred_element_type=F32)
            + jnp.where(row == 0, 0.0, sine) + nyquist)
    yp = yp_ref[0]
    if short_conv_prev:
        yp = _short_conv(yp, cwy_ref[...], cby_ref[...])
    gate = _short_conv(gt_ref[0], cwg_ref[...], cbg_ref[...])
    o_ref[0] = (gate * (conv + sk_ref[...] * yp)).astype(o_ref.dtype)


def _conv_inv(fwd, p, yprev, yprev_col0, short_conv_prev, z, gate_col0, conv_w, conv_b, skip, out_dtype, tn=256):
    b, n2, d = p.shape
    l = n2 // 2
    offy = yprev_col0 // tn
    offg = gate_col0 // tn
    est = l * n2 * 2 + 2 * (n2 * tn * 2 + 3 * l * tn * 4) + 6 * l * tn * 4
    return pl.pallas_call(
        functools.partial(_conv_inv_kernel, short_conv_prev=short_conv_prev),
        out_shape=jax.ShapeDtypeStruct((b, l, d), out_dtype),
        grid=(d // tn, b),
        in_specs=[pl.BlockSpec((n2, l), lambda j, bi: (0, 0), pipeline_mode=pl.Buffered(1)),
                  pl.BlockSpec((1, n2, tn), lambda j, bi: (bi, 0, j)),
                  pl.BlockSpec((1, l, tn), lambda j, bi: (bi, 0, j + offy)),
                  pl.BlockSpec((1, l, tn), lambda j, bi: (bi, 0, j + offg)),
                  pl.BlockSpec((3, tn), lambda j, bi: (0, j + offy)),
                  pl.BlockSpec((1, tn), lambda j, bi: (0, j + offy)),
                  pl.BlockSpec((3, tn), lambda j, bi: (0, j + offg)),
                  pl.BlockSpec((1, tn), lambda j, bi: (0, j + offg)),
                  pl.BlockSpec((1, tn), lambda j, bi: (0, j))],
        out_specs=pl.BlockSpec((1, l, tn), lambda j, bi: (bi, 0, j)),
        compiler_params=_cparams(("parallel", "parallel"), est),
        name="hyena_conv_inv",
    )(fwd, p, yprev, z, conv_w, conv_b, conv_w, conv_b, skip)


def _hyena_mixer(h2, b, l, in_w, in_b, conv_w, conv_b, f_w1, f_w23, f_w4, f_b, f_freq, skip, out_w, out_b):
    d = h2.shape[1]
    z = _matmul(h2, in_w.astype(BF16), bias=in_b, name="hyena_in").reshape(b, l, 3 * d)
    filt = _hyena_filters(l, d, f_w1, f_w23, f_w4, f_b, f_freq)
    fwd = _dft_matrix(l)
    kspec = _spectrum(fwd, filt, d)
    skip = skip.astype(F32)
    p0 = _conv_fwd(fwd, z, 0, kspec, 0, conv_w, conv_b.reshape(1, -1), True, d)
    y1 = _conv_inv(fwd, p0, z, 0, True, z, d, conv_w, conv_b.reshape(1, -1), skip[0:1], F32)
    p1 = _conv_fwd(fwd, y1, 0, kspec, 1, conv_w, conv_b.reshape(1, -1), False, d)
    y2 = _conv_inv(fwd, p1, y1, 0, False, z, 2 * d, conv_w, conv_b.reshape(1, -1), skip[1:2], BF16)
    return y2.reshape(b * l, d), out_w.astype(BF16), out_b


def _lora_in(w):
    pad = lambda m: jnp.pad(m, ((0, 0), (0, LANES - m.shape[1])))
    return jnp.concatenate([pad(w[0]), pad(w[1])], axis=1)


def _lora_out(w):
    return jnp.pad(w, ((0, 0), (0, LANES - w.shape[1]), (0, 0)))


def _rwkv_mixer(x, ctx, mod3, layer, g0, mu, w_r, w_k, w_v, w_o, dec_w0, dec_w1, dec_w2, a0, a1, a2,
                g1, g2, k_k, k_a, r_k, lnx_w, lnx_b):
    b, l, d = x.shape
    n_ctx = ctx.shape[1]
    s = n_ctx + l
    bf = lambda w: w.astype(BF16)
    assert dec_w1.shape[2] <= LANES and a1.shape[2] <= LANES
    xr, xk, xv, lw1, a1o, g1o = _pre_rwkv(x, ctx, mod3, layer, g0, mu, bf(_lora_in(dec_w1)), bf(_lora_in(a1)),
                                          bf(g1))
    r = _matmul(xr.reshape(b * s, d), bf(w_r), out_dtype=BF16, name="rwkv_r").reshape(b, s, d)
    k = _matmul(xk.reshape(b * s, d), bf(w_k), out_dtype=BF16, name="rwkv_k").reshape(b, s, d)
    v = _matmul(xv.reshape(b * s, d), bf(w_v), out_dtype=BF16, name="rwkv_v").reshape(b, s, d)
    zz = _wkv(r, k, v, lw1, a1o, g1o, bf(_lora_out(dec_w2)), bf(_lora_out(a2)), bf(g2), dec_w0, a0, k_k, k_a, r_k,
              lnx_w, lnx_b, n_ctx)
    return zz.reshape(b * l, d), bf(w_o), jnp.zeros((d,), F32)


def kernel(x, c, ctx, c_ctx, ada_w, ada_b, norm_g, mlp_up, mlp_down, rw_mu, rw_w_r, rw_w_k, rw_w_v, rw_w_o, rw_dec_w0, rw_dec_w1, rw_dec_w2, rw_a0, rw_a1, rw_a2, rw_g1, rw_g2, rw_k_k, rw_k_a, rw_r_k, rw_lnx_w, rw_lnx_b, hy_in_w, hy_in_b, hy_conv_w, hy_conv_b, hy_f_w1, hy_f_w23, hy_f_w4, hy_f_b, hy_f_freq, hy_skip, hy_out_w, hy_out_b):
    b, l, d = x.shape
    depth = ada_w.shape[0]
    assert b < MOD_ROWS
    c_rows = jnp.zeros((MOD_ROWS, d), F32).at[:b].set(c).at[b].set(c_ctx)
    mod3 = _ada_mod(c_rows, ada_w, ada_b)
    x2 = x.reshape(b * l, d)
    xc = ctx
    h_next = None
    for i in range(depth):
        kind, j = i % 2, i // 2
        ctx_live = any(q % 2 == 0 for q in range(i + 1, depth))
        assert not ctx_live, "context-stream update is not implemented for this depth"
        if kind == 0:
            mixed = _rwkv_mixer(x2.reshape(b, l, d), xc, mod3, i, norm_g[i, 0], rw_mu[j], rw_w_r[j], rw_w_k[j],
                                rw_w_v[j], rw_w_o[j], rw_dec_w0[j], rw_dec_w1[j], rw_dec_w2[j], rw_a0[j],
                                rw_a1[j], rw_a2[j], rw_g1[j], rw_g2[j], rw_k_k[j], rw_k_a[j], rw_r_k[j],
                                rw_lnx_w[j], rw_lnx_b[j])
        else:
            h = h_next if h_next is not None else _pre_norm(x2, mod3, i, norm_g[i, 0], l)
            mixed = _hyena_mixer(h, b, l, hy_in_w[j], hy_in_b[j], hy_conv_w[j], hy_conv_b[j], hy_f_w1[j],
                                 hy_f_w23[j], hy_f_w4[j], hy_f_b[j], hy_f_freq[j], hy_skip[j], hy_out_w[j],
                                 hy_out_b[j])
        x2, h2 = _proj_post(*mixed, x2, mod3, i, norm_g[i, 1], norm_g[i, 2], l,
                            name="rwkv_out_post" if kind == 0 else "hyena_out_post")
        next_is_hyena = i + 1 < depth and (i + 1) % 2 == 1
        x2, h_next = _mlp(h2, mlp_up[i].astype(BF16), mlp_down[i].astype(BF16), x2, mod3, i, norm_g[i, 3], l,
                          next_g0=norm_g[i + 1, 0] if next_is_hyena else None)
    return x2.reshape(b, l, d)
```

```python
import functools
import math

import jax
import jax.numpy as jnp
from jax import lax
from jax.experimental import pallas as pl
from jax.experimental.pallas import tpu as pltpu

F32 = jnp.float32
BF16 = jnp.bfloat16

HEAD_SIZE = 64
GRID_W = 64
N_MOD = 6
NORM_EPS = 1e-6
LNX_EPS = 64e-5
HY_FAST_DECAY = 0.3
HY_SLOW_DECAY = 1.5
HY_DECAY_TARGET = 1e-2
HY_EMB_DIM = 33
MOD_ROWS = 16
CHUNK = 64
LANES = 128
TOKEN_TILE = 256
VMEM_LIMIT_CAP = 60000 * 1024


def _cparams(sem, est_bytes):
    limit = int(min(max(2 * est_bytes, 32 * 1024 * 1024), VMEM_LIMIT_CAP))
    return pltpu.CompilerParams(dimension_semantics=sem, vmem_limit_bytes=limit)


def _largest_tile(n, cap, align):
    t = min(cap, n) // align * align
    while t > align and n % t:
        t -= align
    assert t > 0 and n % t == 0, (n, cap, align)
    return t


def _split2(x):
    hi = x.astype(BF16)
    return hi, (x - hi.astype(F32)).astype(BF16)


def _rms(x):
    return x * lax.rsqrt(jnp.mean(x * x, axis=-1, keepdims=True) + NORM_EPS)


def _norm_mod(x, g, shift, scale):
    return (_rms(x) * g) * (1.0 + scale) + shift


def _softplus(y):
    return jnp.maximum(y, 0.0) + jnp.log1p(jnp.exp(-jnp.abs(y)))


def _ada_kernel(c_ref, w_ref, b_ref, o_ref):
    c = c_ref[...]
    s_hi, s_lo = _split2(c * jax.nn.sigmoid(c))
    w_hi, w_lo = _split2(w_ref[0])
    p = jnp.dot(jnp.concatenate([s_hi, s_lo], axis=0), w_hi, preferred_element_type=F32)
    q = jnp.dot(s_hi, w_lo, preferred_element_type=F32)
    o_ref[0] = p[:MOD_ROWS] + p[MOD_ROWS:] + q + b_ref[0]


def _ada_mod(c_rows, ada_w, ada_b):
    depth, d, n = ada_w.shape
    tn = 1024
    out = pl.pallas_call(
        _ada_kernel,
        out_shape=jax.ShapeDtypeStruct((depth, MOD_ROWS, n), F32),
        grid=(depth, n // tn),
        in_specs=[pl.BlockSpec((MOD_ROWS, d), lambda l, j: (0, 0)),
                  pl.BlockSpec((1, d, tn), lambda l, j: (l, 0, j)),
                  pl.BlockSpec((1, 1, tn), lambda l, j: (l, 0, j))],
        out_specs=pl.BlockSpec((1, MOD_ROWS, tn), lambda l, j: (l, 0, j)),
        compiler_params=_cparams(("parallel", "parallel"), 2 * d * tn * 4 + 4 * d * tn),
        name="ada_mod",
    )(c_rows, ada_w, ada_b.reshape(depth, 1, n))
    return out.reshape(depth * MOD_ROWS * N_MOD, 1, d)


def _mod_row(layer, row, j):
    return (layer * MOD_ROWS + row) * N_MOD + j


def _pre_rwkv_kernel(hc_ref, sc_ref_ctx, xm_ref, xu_ref, xd_ref, g_ref, sh_ref, sc_ref, mu_ref,
                     wdec_ref, wa_ref, wg_ref, xr_ref, xk_ref, xv_ref, lw1_ref, a1_ref, g1_ref,
                     mw_scr, ma_scr, mg_scr, *, n_tiles):
    t = pl.program_id(1)
    d = xm_ref.shape[-1]
    g = g_ref[...]
    mu = mu_ref[...]
    big = {0: xr_ref, 2: xk_ref, 3: xv_ref}
    small = {1: mw_scr, 4: ma_scr, 5: mg_scr}

    def emit(h, s, lo, hi):
        xx = s - h
        for j in range(6):
            mix = (h + xx * mu[j:j + 1, lo:hi]).astype(BF16)
            if j in big:
                big[j][0, :, lo:hi] = mix
            else:
                small[j][:, lo:hi] = mix

    @pl.when(t == 0)
    def _():
        emit(hc_ref[0], sc_ref_ctx[0], 0, d)

    @pl.when(t > 0)
    def _():
        sh = sh_ref[0]
        sc = sc_ref[0]
        xm = xm_ref[0].reshape(TOKEN_TILE, d)
        hm = _norm_mod(xm, g, sh, sc)
        hu = _norm_mod(xu_ref[0, 0], g, sh, sc) * jnp.where(t > 1, 1.0, 0.0)
        hd = _norm_mod(xd_ref[0, 0], g, sh, sc) * jnp.where(t < n_tiles, 1.0, 0.0)
        col = lax.broadcasted_iota(jnp.int32, (TOKEN_TILE, 1), 0) & (GRID_W - 1)
        q = d // 4
        left = jnp.where(col != 0, pltpu.roll(hm[:, :q], 1, 0), 0.0)
        right = jnp.where(col != GRID_W - 1, pltpu.roll(hm[:, q:2 * q], TOKEN_TILE - 1, 0), 0.0)
        up = jnp.concatenate([hu[:, 2 * q:3 * q], hm[:TOKEN_TILE - GRID_W, 2 * q:3 * q]], axis=0)
        down = jnp.concatenate([hm[GRID_W:, 3 * q:], hd[:, 3 * q:]], axis=0)
        emit(hm[:, :q], left, 0, q)
        emit(hm[:, q:2 * q], right, q, 2 * q)
        emit(hm[:, 2 * q:3 * q], up, 2 * q, 3 * q)
        emit(hm[:, 3 * q:], down, 3 * q, d)

    lw1_ref[0] = jnp.tanh(jnp.dot(mw_scr[...], wdec_ref[...], preferred_element_type=F32)).astype(BF16)
    a1_ref[0] = jnp.dot(ma_scr[...], wa_ref[...], preferred_element_type=F32).astype(BF16)
    g1_ref[0] = jax.nn.sigmoid(jnp.dot(mg_scr[...], wg_ref[...], preferred_element_type=F32)).astype(BF16)


def _shift_seq(x):
    half = x.shape[-1] // 2
    p = jnp.pad(x, ((0, 0), (1, 1), (0, 0)))
    return jnp.concatenate([p[:, :-2, :half], p[:, 2:, half:]], axis=-1)


def _pre_rwkv(x, ctx, mod3, layer, g0, mu, w_dec1, w_a1, w_g1):
    b, l, d = x.shape
    ranks = (w_dec1.shape[1], w_a1.shape[1], w_g1.shape[1])
    n_ctx = ctx.shape[1]
    assert n_ctx == TOKEN_TILE and l % TOKEN_TILE == 0
    n_tiles = l // TOKEN_TILE
    rows_per_tile = TOKEN_TILE // GRID_W
    n_rows = l // GRID_W
    x4 = x.reshape(b, n_rows, GRID_W, d)
    s = n_ctx + l
    hc = _pre_norm(ctx.reshape(b * n_ctx, d), mod3, layer, g0, n_ctx, tm=n_ctx, fixed_row=b,
                   out_dtype=F32).reshape(b, n_ctx, d)
    sc = _shift_seq(hc)

    def mrow(j):
        return pl.BlockSpec((1, 1, d), lambda bi, t: (_mod_row(layer, bi, j), 0, 0))

    main = lambda bi, t: (bi, jnp.maximum(t - 1, 0), 0, 0)
    up = lambda bi, t: (bi, jnp.maximum((t - 1) * rows_per_tile - 1, 0), 0, 0)
    down = lambda bi, t: (bi, jnp.minimum(jnp.maximum(t, 1) * rows_per_tile, n_rows - 1), 0, 0)
    whole = lambda w: pl.BlockSpec(w.shape, lambda bi, t: (0, 0))
    outs = pl.pallas_call(
        functools.partial(_pre_rwkv_kernel, n_tiles=n_tiles),
        out_shape=[jax.ShapeDtypeStruct((b, s, d), BF16)] * 3
                  + [jax.ShapeDtypeStruct((b, s, rk), BF16) for rk in ranks],
        grid=(b, n_tiles + 1),
        in_specs=[pl.BlockSpec((1, n_ctx, d), lambda bi, t: (bi, 0, 0)),
                  pl.BlockSpec((1, n_ctx, d), lambda bi, t: (bi, 0, 0)),
                  pl.BlockSpec((1, rows_per_tile, GRID_W, d), main),
                  pl.BlockSpec((1, 1, GRID_W, d), up),
                  pl.BlockSpec((1, 1, GRID_W, d), down),
                  pl.BlockSpec((1, d), lambda bi, t: (0, 0)),
                  mrow(0), mrow(1),
                  pl.BlockSpec((6, d), lambda bi, t: (0, 0)),
                  whole(w_dec1), whole(w_a1), whole(w_g1)],
        out_specs=[pl.BlockSpec((1, TOKEN_TILE, d), lambda bi, t: (bi, t, 0))] * 3
                  + [pl.BlockSpec((1, TOKEN_TILE, rk), lambda bi, t: (bi, t, 0)) for rk in ranks],
        scratch_shapes=[pltpu.VMEM((TOKEN_TILE, d), BF16)] * 3,
        compiler_params=_cparams(("parallel", "arbitrary"),
                                 2 * (3 * TOKEN_TILE * d * 4 + 2 * GRID_W * d * 4 + 3 * TOKEN_TILE * d * 2
                                      + d * sum(ranks) * 2) + 3 * TOKEN_TILE * d * 2),
        name="rwkv_pre",
    )(hc, sc, x4, x4, x4, g0.reshape(1, d), mod3, mod3, mu, w_dec1, w_a1, w_g1)
    return outs


def _mm_kernel(a_ref, w_ref, *rest, has_bias):
    o_ref = rest[-1]
    acc = jnp.dot(a_ref[...], w_ref[...], preferred_element_type=F32)
    if has_bias:
        acc = acc + rest[0][...]
    o_ref[...] = acc.astype(o_ref.dtype)


def _matmul(a, w, bias=None, out_dtype=F32, tm=1024, tn=1024, name="matmul"):
    m, k = a.shape
    n = w.shape[1]
    tm = _largest_tile(m, tm, 8)
    tn = _largest_tile(n, tn, LANES)
    in_specs = [pl.BlockSpec((tm, k), lambda i, j: (i, 0)),
                pl.BlockSpec((k, tn), lambda i, j: (0, j))]
    args = [a, w]
    if bias is not None:
        in_specs.append(pl.BlockSpec((1, tn), lambda i, j: (0, j)))
        args.append(bias.reshape(1, n).astype(F32))
    est = 2 * (tm * k * 2 + k * tn * 2 + tm * tn * jnp.dtype(out_dtype).itemsize) + tm * tn * 4
    return pl.pallas_call(
        functools.partial(_mm_kernel, has_bias=bias is not None),
        out_shape=jax.ShapeDtypeStruct((m, n), out_dtype),
        grid=(m // tm, n // tn),
        in_specs=in_specs,
        out_specs=pl.BlockSpec((tm, tn), lambda i, j: (i, j)),
        compiler_params=_cparams(("parallel", "parallel"), est),
        name=name,
    )(*args)


def _seg_sum(x, ones_bd):
    r = x.shape[0]
    p = jnp.dot(jnp.concatenate(_split2(x), axis=0), ones_bd, preferred_element_type=F32)
    return p[:r] + p[r:]


def _bd(x, head0):
    return jnp.concatenate([jnp.where(head0, x, 0.0), jnp.where(head0, 0.0, x)], axis=0)


def _wkv_constants(rev):
    c = CHUNK
    shift = int(math.log2(c))
    head0 = lax.broadcasted_iota(jnp.int32, (1, LANES), 1) < HEAD_SIZE
    row2 = lax.broadcasted_iota(jnp.int32, (2 * c, 2 * c), 0)
    col2 = lax.broadcasted_iota(jnp.int32, (2 * c, 2 * c), 1)
    ones_bd = jnp.where((row2 >> shift) == (col2 >> shift), 1.0, 0.0).astype(BF16)
    eye = jnp.where(row2 == col2, 1.0, 0.0).astype(F32)
    rowc = lax.broadcasted_iota(jnp.int32, (c, c), 0)
    colc = lax.broadcasted_iota(jnp.int32, (c, c), 1)
    tri = jnp.where(colc >= rowc if rev else colc <= rowc, 1.0, 0.0).astype(BF16)
    tc = lax.broadcasted_iota(jnp.int32, (c, 2 * c), 0)
    sc = lax.broadcasted_iota(jnp.int32, (c, 2 * c), 1) & (c - 1)
    eye_c = jnp.where(tc == sc, 1.0, 0.0).astype(F32)
    if rev:
        tc, sc = sc, tc
    merge = tuple(((tc >> lv) == (sc >> lv) + 1) & ((tc >> (lv + 1)) == (sc >> (lv + 1))) for lv in range(shift))
    return head0, ones_bd, tri, sc < tc, sc <= tc, eye, merge, eye_c


def _each(fn, *lists):
    return [fn(*args) for args in zip(*lists)]


def _mm(a, b):
    return jnp.dot(a.astype(BF16), b.astype(BF16), preferred_element_type=F32)


def _wkv_prepare(probs, k_a, hooks=()):
    hooks = list(hooks)

    def run_hook():
        if hooks:
            hooks.pop(0)()

    revs, rs, ks, kks, bd_vs, lwxs, axs, w0s, a0s, csts = [list(t) for t in zip(*probs)]
    head0 = csts[0][0]
    eye = csts[0][5]
    c = rs[0].shape[0]
    c2 = 2 * c
    logw = _each(lambda w0, lwx: -jnp.exp(-_softplus(-(w0 + lwx)) - 0.5), w0s, lwxs)
    a = _each(lambda a0, ax: jax.nn.sigmoid(a0 + ax), a0s, axs)
    kd = _each(lambda k, a_: k * (1.0 + (a_ - 1.0) * k_a), ks, a)
    b = _each(lambda kk, a_: kk * a_, kks, a)

    def cumsum(lw, cst):
        c2_ = jnp.dot(cst[2], jnp.concatenate(_split2(lw), axis=1), preferred_element_type=F32)
        return c2_[:, :LANES] + c2_[:, LANES:]

    cum = _each(cumsum, logw, csts)
    total = _each(lambda cm, rev: cm[0:1] if rev else cm[c - 1:c], cum, revs)
    kap_t = _each(lambda kk, cm, lw: kk * jnp.exp(cm - lw), kks, cum, logw)
    r_t = _each(lambda r, cm: r * jnp.exp(cm), rs, cum)
    igam = _each(lambda cm: jnp.exp(-cm), cum)
    tail = _each(lambda t, cm: jnp.exp(t - cm), total, cum)

    bd_kap = _each(lambda x: _bd(x, head0).astype(BF16), kap_t)
    lhs = _each(lambda kap, rt: jnp.concatenate([kap, rt], axis=0).astype(BF16), kap_t, r_t)
    rhs = _each(lambda kd_, b_, ig: jnp.concatenate([_bd(kd_ * ig, head0), _bd(b_ * ig, head0)],
                                                    axis=0).astype(BF16), kd, b, igam)
    bct = _each(lambda b_, tl: _bd(b_ * tl, head0).T.astype(BF16), b, tail)
    kct = _each(lambda kd_, tl: _bd(kd_ * tl, head0).T.astype(BF16), kd, tail)
    gmat = _each(lambda l_, r_: lax.dot_general(l_, r_, (((1,), (1,)), ((), ())),
                                                preferred_element_type=F32), lhs, rhs)
    a_kb = _each(lambda g, cst: jnp.where(cst[3], g[:c, c2:], 0.0), gmat, csts)
    a_rb = _each(lambda g, cst: jnp.where(cst[4], g[c:, c2:], 0.0).astype(BF16), gmat, csts)
    a_kr = _each(lambda g, cst: jnp.concatenate([jnp.where(cst[3], g[:c, :c2], 0.0),
                                                 jnp.where(cst[4], g[c:, :c2], 0.0)], axis=0).astype(BF16),
                 gmat, csts)
    av = _each(_mm, a_kr, bd_vs)
    x0 = _each(lambda kap, av_: jnp.concatenate([kap, _bd(av_[:c], head0).astype(BF16)], axis=1), bd_kap, av)
    ov0 = _each(lambda av_: av_[c:], av)
    run_hook()

    tinv = _each(lambda akb, cst: cst[7] - jnp.where(cst[6][0], akb, 0.0), a_kb, csts)
    for lv in range(1, len(csts[0][6])):
        y = _each(lambda akb, x, cst: _mm(jnp.where(cst[6][lv], akb, 0.0), _bd(x, head0)), a_kb, tinv, csts)
        run_hook()
        tinv = _each(lambda x, y_: x - _mm(x, _bd(y_, head0)), tinv, y)
        run_hook()

    wu_c = _each(_mm, tinv, x0)
    wu = _each(lambda w_: jnp.concatenate([_bd(w_[:, :LANES], head0), _bd(w_[:, LANES:], head0)],
                                          axis=1).astype(BF16), wu_c)
    run_hook()
    rb = _each(_mm, a_rb, wu)
    run_hook()
    mn = _each(_mm, bct, wu)
    while hooks:
        run_hook()
    kv = _each(_mm, kct, bd_vs)
    out = []
    for i in range(len(probs)):
        rk = r_t[i] - rb[i][:, :LANES]
        ov = ov0[i] - rb[i][:, LANES:]
        m_mat = jnp.where(eye > 0.0, jnp.exp(total[i]), 0.0) - mn[i][:, :LANES]
        n_mat = kv[i] - mn[i][:, LANES:]
        out.append((jnp.concatenate([rk, m_mat], axis=0).astype(BF16), ov, n_mat))
    return out


def _wkv_apply(rm_lhs, ov, n_mat, h_state):
    c = ov.shape[0]
    rm = jnp.dot(rm_lhs, h_state.astype(BF16), preferred_element_type=F32)
    return rm[:c] + ov, rm[c:] + n_mat


def _wkv_kernel(r_ref, k_ref, v_ref, lw1_ref, a1_ref, g1_ref, wdec_ref, wa_ref, wg_ref,
                w0_ref, a0_ref, kk_ref, ka_ref, rk_ref, lnw_ref, lnb_ref,
                z_ref, of_scr, ob_scr, rm_scr, ov_scr, n_scr, *, n_ctx_chunks, n_chunks, unroll):
    c = CHUNK
    cst_f = _wkv_constants(False)
    cst_b = _wkv_constants(True)
    head0, ones_bd = cst_f[0], cst_f[1]

    w0f, w0b = w0_ref[0:1, :], w0_ref[1:2, :]
    a0f, a0b = a0_ref[0:1, :], a0_ref[1:2, :]
    k_k = kk_ref[...]
    k_a = ka_ref[...]

    def load(ref, rows):
        return ref[0, rows, :].astype(F32)

    def lora(x_ref, w_ref, rows, rev):
        d0 = int(rev) * LANES
        return jnp.dot(x_ref[0, rows, d0:d0 + LANES], w_ref[int(rev)], preferred_element_type=F32)

    def chunk_of(i, rev):
        if not rev:
            return i
        return jnp.where(i < n_ctx_chunks, n_ctx_chunks - 1 - i, n_chunks + n_ctx_chunks - 1 - i)

    def prepare_group(grp, hooks=()):
        probs, slots = [], []
        for u in range(unroll):
            for rev, w0, a0, cst in ((False, w0f, a0f, cst_f), (True, w0b, a0b, cst_b)):
                ci = chunk_of(grp * unroll + u, rev)
                rows = pl.ds(pl.multiple_of(ci * c, c), c)
                k = load(k_ref, rows)
                kk0 = k * k_k
                kk = kk0 * lax.rsqrt(jnp.maximum(_seg_sum(kk0 * kk0, ones_bd), 1e-24))
                bd_v = _bd(load(v_ref, rows), head0).astype(BF16)
                probs.append((rev, load(r_ref, rows), k, kk, bd_v, lora(lw1_ref, wdec_ref, rows, rev),
                              lora(a1_ref, wa_ref, rows, rev), w0, a0, cst))
                slots.append(ci + n_chunks * int(rev))
        for slot, (rm_lhs, ov, n_mat) in zip(slots, _wkv_prepare(probs, k_a, hooks)):
            rm_scr[slot] = rm_lhs
            ov_scr[slot] = ov
            n_scr[slot] = n_mat

    def state_steps(grp, state):
        def step(i):
            for rev, o_scr in ((False, of_scr), (True, ob_scr)):
                ci = chunk_of(i, rev)
                slot = ci + n_chunks * int(rev)
                o, state[int(rev)] = _wkv_apply(rm_scr[slot], ov_scr[slot], n_scr[slot], state[int(rev)])
                o_scr[pl.ds(pl.multiple_of(ci * c, c), c), :] = o
        return [functools.partial(step, grp * unroll + u) for u in range(unroll)]

    n_groups = n_chunks // unroll
    prepare_group(0)

    def body(grp, carry):
        state = list(carry)
        prepare_group(grp, state_steps(grp - 1, state))
        return tuple(state)

    zero = jnp.zeros((2 * c, LANES), F32)
    state = list(lax.fori_loop(1, n_groups, body, (zero, zero)))

    r_k = rk_ref[...]
    lnw = lnw_ref[...]
    lnb = lnb_ref[...]
    blk = TOKEN_TILE
    n_ctx = n_ctx_chunks * c
    inv_n = 1.0 / HEAD_SIZE

    def read_blocks(blocks, hooks):
        hooks = list(hooks)

        def run_hook():
            if hooks:
                hooks.pop(0)()

        rows = [pl.ds(n_ctx + j * blk, blk) for j in blocks]
        seg = lambda xs: [_seg_sum(x, ones_bd) for x in xs]
        a_f = [lora(a1_ref, wa_ref, rw, False) for rw in rows]
        a_b = [lora(a1_ref, wa_ref, rw, True) for rw in rows]
        gate = [jnp.dot(g1_ref[0, rw, :], wg_ref[...], preferred_element_type=F32) for rw in rows]
        run_hook()
        o = [of_scr[rw, :] + ob_scr[rw, :] for rw in rows]
        dev = [o_ - m_ * inv_n for o_, m_ in zip(o, seg(o))]
        run_hook()
        var = seg([d_ * d_ for d_ in dev])
        run_hook()
        kd_sum = [load(k_ref, rw) * (2.0 + (jax.nn.sigmoid(a0f + af_) + jax.nn.sigmoid(a0b + ab_) - 2.0) * k_a)
                  for rw, af_, ab_ in zip(rows, a_f, a_b)]
        rkk = seg([load(r_ref, rw) * kd_ * r_k for rw, kd_ in zip(rows, kd_sum)])
        while hooks:
            run_hook()
        for j, rw, dev_, var_, rkk_, gate_ in zip(blocks, rows, dev, var, rkk, gate):
            on = dev_ * lax.rsqrt(var_ * inv_n + LNX_EPS) * lnw + lnb
            z = (on + rkk_ * load(v_ref, rw)) * gate_
            z_ref[0, pl.ds(j * blk, blk), :] = z.astype(z_ref.dtype)

    per_blk = blk // c
    n_blk = (n_chunks - n_ctx_chunks) // per_blk
    first = lambda j: n_ctx_chunks + j * per_blk
    ready = {j: max(first(j) + per_blk - 1, n_chunks + n_ctx_chunks - 1 - first(j)) for j in range(n_blk)}
    order = sorted(range(n_blk), key=lambda j: ready[j])
    pending = [((n_groups - 1) * unroll + u, fn) for u, fn in enumerate(state_steps(n_groups - 1, state))]
    hooks_per_read = 4
    for pos in range(0, n_blk, 2):
        blocks = order[pos:pos + 2]
        need = max(ready[j] for j in blocks)
        while pending and pending[0][0] <= need:
            pending.pop(0)[1]()
        ride, pending = pending[:hooks_per_read], pending[hooks_per_read:]
        read_blocks(blocks, [fn for _, fn in ride])
    for _, fn in pending:
        fn()


def _wkv(r, k, v, lw1, a1, g1, w_dec, w_a, w_g, dec_w0, a0, k_k, k_a, r_k, lnx_w, lnx_b, n_ctx):
    b, s, d = r.shape
    l = s - n_ctx
    npair = d // LANES
    seq = lambda bi, p: (bi, 0, p)
    vec = lambda bi, p: (0, p)
    sblk = pl.BlockSpec((1, s, LANES), seq)
    rank_blk = pl.BlockSpec((1, s, 2 * LANES), lambda bi, p: (bi, 0, 0))
    up_blk = pl.BlockSpec((2, LANES, LANES), lambda bi, p: (0, 0, p))
    n_chunks = s // CHUNK
    unroll = next(u for u in (12, 9, 6, 4, 2, 1) if n_chunks % u == 0)
    scratch = [pltpu.VMEM((s, LANES), F32), pltpu.VMEM((s, LANES), F32),
               pltpu.VMEM((2 * n_chunks, 3 * CHUNK, LANES), BF16),
               pltpu.VMEM((2 * n_chunks, CHUNK, LANES), F32),
               pltpu.VMEM((2 * n_chunks, 2 * CHUNK, LANES), F32)]
    est = (2 * (3 * s * LANES * 2 + 2 * s * 2 * LANES * 2 + s * g1.shape[2] * 2 + l * LANES * 2) + 2 * s * LANES * 4
           + 2 * n_chunks * CHUNK * LANES * (3 * 2 + 4 + 2 * 4))
    return pl.pallas_call(
        functools.partial(_wkv_kernel, n_ctx_chunks=n_ctx // CHUNK, n_chunks=n_chunks, unroll=unroll),
        out_shape=jax.ShapeDtypeStruct((b, l, d), BF16),
        grid=(b, npair),
        in_specs=[sblk, sblk, sblk, rank_blk, rank_blk,
                  pl.BlockSpec((1, s, g1.shape[2]), lambda bi, p: (bi, 0, 0)), up_blk, up_blk,
                  pl.BlockSpec((g1.shape[2], LANES), lambda bi, p: (0, p)),
                  pl.BlockSpec((2, LANES), vec), pl.BlockSpec((2, LANES), vec),
                  pl.BlockSpec((1, LANES), vec), pl.BlockSpec((1, LANES), vec), pl.BlockSpec((1, LANES), vec),
                  pl.BlockSpec((1, LANES), vec), pl.BlockSpec((1, LANES), vec)],
        out_specs=pl.BlockSpec((1, l, LANES), seq),
        scratch_shapes=scratch,
        compiler_params=_cparams(("parallel", "parallel"), est),
        name="wkv_scan",
    )(r, k, v, lw1, a1, g1, w_dec, w_a, w_g, dec_w0, a0, k_k.reshape(1, d), k_a.reshape(1, d), r_k.reshape(1, d),
      lnx_w.reshape(1, d), lnx_b.reshape(1, d))


def _proj_post_kernel(a_ref, w_ref, b_ref, x_ref, g1_ref, g2_ref, gate_ref, sh_ref, sc_ref, xo_ref, h_ref):
    y = jnp.dot(a_ref[...], w_ref[...], preferred_element_type=F32) + b_ref[...]
    x = x_ref[...] + gate_ref[0] * (_rms(y) * g1_ref[...])
    xo_ref[...] = x
    h_ref[...] = _norm_mod(x, g2_ref[...], sh_ref[0], sc_ref[0]).astype(h_ref.dtype)


def _proj_post(a, w, bias, x2, mod3, layer, g1, g2, l, tm=512, name="proj_post"):
    m, k = a.shape
    d = w.shape[1]
    per_b = l // tm
    row = lambda j: pl.BlockSpec((1, 1, d), lambda i: (_mod_row(layer, i // per_b, j), 0, 0))
    tile = pl.BlockSpec((tm, d), lambda i: (i, 0))
    vec = pl.BlockSpec((1, d), lambda i: (0, 0))
    est = k * d * 2 + 2 * (tm * k * 2 + tm * d * (4 + 4 + 2)) + 2 * tm * d * 4
    return pl.pallas_call(
        _proj_post_kernel,
        out_shape=[jax.ShapeDtypeStruct((m, d), F32), jax.ShapeDtypeStruct((m, d), BF16)],
        grid=(m // tm,),
        in_specs=[pl.BlockSpec((tm, k), lambda i: (i, 0)),
                  pl.BlockSpec((k, d), lambda i: (0, 0), pipeline_mode=pl.Buffered(1)),
                  vec, tile, vec, vec, row(2), row(3), row(4)],
        out_specs=[tile, tile],
        compiler_params=_cparams(("parallel",), est),
        name=name,
    )(a, w, bias.reshape(1, d).astype(F32), x2, g1.reshape(1, d), g2.reshape(1, d), mod3, mod3, mod3)


def _mlp_kernel(h_ref, wu_ref, wd_ref, x_ref, g_ref, gate_ref, *rest, with_next):
    if with_next:
        gn_ref, shn_ref, scn_ref, o_ref, hn_ref, acc_ref = rest
    else:
        o_ref, acc_ref = rest
    kf = pl.program_id(1)

    @pl.when(kf == 0)
    def _():
        acc_ref[...] = jnp.zeros_like(acc_ref)

    u = jnp.dot(h_ref[...], wu_ref[0], preferred_element_type=F32)
    u = jnp.square(jnp.maximum(u, 0.0)).astype(BF16)
    acc_ref[...] += jnp.dot(u, wd_ref[0], preferred_element_type=F32)

    @pl.when(kf == pl.num_programs(1) - 1)
    def _():
        x = x_ref[...] + gate_ref[0] * (_rms(acc_ref[...]) * g_ref[...])
        o_ref[...] = x
        if with_next:
            hn_ref[...] = _norm_mod(x, gn_ref[...], shn_ref[0], scn_ref[0]).astype(hn_ref.dtype)


def _mlp(h2, w_up, w_down, x2, mod3, layer, g3, l, next_g0=None, tm=512, tf=1024):
    m, d = h2.shape
    dff = w_up.shape[2]
    per_b = l // tm
    with_next = next_g0 is not None
    row = lambda lay, j: pl.BlockSpec((1, 1, d), lambda i, f: (_mod_row(lay, i // per_b, j), 0, 0))
    tile = pl.BlockSpec((tm, d), lambda i, f: (i, 0))
    vec = pl.BlockSpec((1, d), lambda i, f: (0, 0))
    in_specs = [tile, pl.BlockSpec((1, d, tf), lambda i, f: (layer, 0, f)),
                pl.BlockSpec((1, tf, d), lambda i, f: (layer, f, 0)), tile, vec, row(layer, 5)]
    args = [h2, w_up, w_down, x2, g3.reshape(1, d), mod3]
    out_shape = [jax.ShapeDtypeStruct((m, d), F32)]
    if with_next:
        in_specs += [vec, row(layer + 1, 0), row(layer + 1, 1)]
        args += [next_g0.reshape(1, d), mod3, mod3]
        out_shape.append(jax.ShapeDtypeStruct((m, d), BF16))
    est = 2 * (tm * d * 2 + 2 * d * tf * 2 + 2 * tm * d * 4 + tm * d * 2) + tm * d * 4 + tm * tf * 6
    outs = pl.pallas_call(
        functools.partial(_mlp_kernel, with_next=with_next),
        out_shape=out_shape,
        grid=(m // tm, dff // tf),
        in_specs=in_specs,
        out_specs=[tile] * len(out_shape),
        scratch_shapes=[pltpu.VMEM((tm, d), F32)],
        compiler_params=_cparams(("parallel", "arbitrary"), est),
        name="mlp",
    )(*args)
    return outs if with_next else (outs[0], None)


def _pre_norm_kernel(x_ref, g_ref, sh_ref, sc_ref, h_ref):
    h_ref[...] = _norm_mod(x_ref[...], g_ref[...], sh_ref[0], sc_ref[0]).astype(h_ref.dtype)


def _pre_norm(x2, mod3, layer, g0, l, tm=512, fixed_row=None, out_dtype=BF16):
    m, d = x2.shape
    per_b = l // tm
    if fixed_row is None:
        row = lambda j: pl.BlockSpec((1, 1, d), lambda i: (_mod_row(layer, i // per_b, j), 0, 0))
    else:
        row = lambda j: pl.BlockSpec((1, 1, d), lambda i: (_mod_row(layer, fixed_row, j), 0, 0))
    tile = pl.BlockSpec((tm, d), lambda i: (i, 0))
    return pl.pallas_call(
        _pre_norm_kernel,
        out_shape=jax.ShapeDtypeStruct((m, d), out_dtype),
        grid=(m // tm,),
        in_specs=[tile, pl.BlockSpec((1, d), lambda i: (0, 0)), row(0), row(1)],
        out_specs=tile,
        compiler_params=_cparams(("parallel",), 2 * tm * d * 6),
        name="pre_norm",
    )(x2, g0.reshape(1, d), mod3, mod3)


def _filter_kernel(z_ref, w1_ref, w2_ref, w3_ref, b_ref, fr_ref, w4_ref, t_ref, dl_ref, o_ref, hid_ref):
    hp = lax.Precision.HIGHEST

    @pl.when(pl.program_id(0) == 0)
    def _():
        b = b_ref[...]
        fr = fr_ref[...]
        z = jnp.sin(fr[0:1] * (jnp.dot(z_ref[...], w1_ref[...], precision=hp, preferred_element_type=F32) + b[0:1]))
        z = jnp.sin(fr[1:2] * (jnp.dot(z, w2_ref[...], precision=hp, preferred_element_type=F32) + b[1:2]))
        hid_ref[...] = jnp.sin(fr[2:3] * (jnp.dot(z, w3_ref[...], precision=hp, preferred_element_type=F32)
                                          + b[2:3]))

    filt = jnp.dot(hid_ref[...], w4_ref[...], precision=hp, preferred_element_type=F32)
    o_ref[...] = filt * jnp.exp(-t_ref[...] * dl_ref[...])


def _hyena_filters(l, d, f_w1, f_w23, f_w4, f_b, f_freq):
    t = jnp.linspace(0.0, 1.0, l, dtype=F32)[:, None]
    bands = (HY_EMB_DIM - 1) // 2
    freqs = jnp.linspace(1e-4, bands - 1, bands, dtype=F32)[None, :]
    ang = (2.0 * math.pi / l) * jnp.arange(l, dtype=F32)[:, None] * freqs
    z = jnp.concatenate([t, jnp.cos(ang), -jnp.sin(ang)], axis=-1)
    pad = lambda a_, r, c: jnp.pad(a_.astype(F32), ((0, r - a_.shape[0]), (0, c - a_.shape[1])))
    zp = pad(z, l, LANES)
    w1 = pad(f_w1, LANES, LANES)
    w2 = pad(f_w23[0], LANES, LANES)
    w3 = pad(f_w23[1], LANES, LANES)
    bb = pad(f_b, 8, LANES)
    fr = pad(f_freq, 8, LANES)
    n = f_w4.shape[1]
    w4 = pad(f_w4, LANES, n)
    max_decay = math.log(HY_DECAY_TARGET) / HY_FAST_DECAY
    min_decay = math.log(HY_DECAY_TARGET) / HY_SLOW_DECAY
    deltas = jnp.abs(jnp.linspace(min_decay, max_decay, d, dtype=F32))[None, :]
    tn = _largest_tile(d, 1024, LANES)
    per_d = d // tn
    sq = pl.BlockSpec((LANES, LANES), lambda j: (0, 0))
    small = pl.BlockSpec((8, LANES), lambda j: (0, 0))
    return pl.pallas_call(
        _filter_kernel,
        out_shape=jax.ShapeDtypeStruct((l, n), F32),
        grid=(n // tn,),
        in_specs=[pl.BlockSpec((l, LANES), lambda j: (0, 0)), sq, sq, sq, small, small,
                  pl.BlockSpec((LANES, tn), lambda j: (0, j)),
                  pl.BlockSpec((l, 1), lambda j: (0, 0)),
                  pl.BlockSpec((1, tn), lambda j: (0, j % per_d))],
        out_specs=pl.BlockSpec((l, tn), lambda j: (0, j)),
        scratch_shapes=[pltpu.VMEM((l, LANES), F32)],
        compiler_params=_cparams(("arbitrary",), 4 * l * tn * 4),
        name="hyena_filters",
    )(zp, w1, w2, w3, bb, fr, w4, t, deltas)


def _dft_matrices(l):
    n = 2 * l
    k = jnp.arange(l, dtype=jnp.int32)[:, None]
    t = jnp.arange(l, dtype=jnp.int32)[None, :]
    ang = ((k * t) % n).astype(F32) * (2.0 * math.pi / n)
    cos = jnp.cos(ang)
    msin = jnp.where(k == 0, jnp.where(t % 2 == 0, 1.0, -1.0), -jnp.sin(ang))
    fwd = jnp.concatenate([cos, msin], axis=0)
    return fwd.astype(BF16), fwd.T.astype(BF16)


def _short_conv(z, cw, cb):
    n = z.shape[0]
    row = lax.broadcasted_iota(jnp.int32, (n, 1), 0)
    prev = jnp.where(row != 0, pltpu.roll(z, 1, 0), 0.0)
    nxt = jnp.where(row != n - 1, pltpu.roll(z, n - 1, 0), 0.0)
    return prev * cw[0:1] + z * cw[1:2] + nxt * cw[2:3] + cb


def _spec_kernel(f_ref, hf_ref, hb_ref, o_ref):
    l, tn = hf_ref.shape
    row = lax.broadcasted_iota(jnp.int32, (l, 1), 0)
    hb = jnp.where(row == 0, 0.0, hb_ref[...])
    h2 = jnp.concatenate([hf_ref[...], hb], axis=1).astype(BF16)
    u = jnp.dot(f_ref[...], h2, preferred_element_type=F32)
    row2 = lax.broadcasted_iota(jnp.int32, (2 * l, 1), 0)
    sign = jnp.where(row2 > l, -1.0, 1.0)
    scale = jnp.where((row2 == 0) | (row2 == l), 1.0 / (2 * l), 2.0 / (2 * l))
    o_ref[0] = (u[:, :tn] + sign * u[:, tn:]) * scale


def _spectrum(fwd, filt, d, tn=256):
    l = filt.shape[0]
    n2 = fwd.shape[0]
    orders = filt.shape[1] // (2 * d)
    per = d // tn
    return pl.pallas_call(
        _spec_kernel,
        out_shape=jax.ShapeDtypeStruct((orders, n2, d), F32),
        grid=(orders, per),
        in_specs=[pl.BlockSpec((n2, l), lambda o, j: (0, 0), pipeline_mode=pl.Buffered(1)),
                  pl.BlockSpec((l, tn), lambda o, j: (0, o * per + j)),
                  pl.BlockSpec((l, tn), lambda o, j: (0, (orders + o) * per + j))],
        out_specs=pl.BlockSpec((1, n2, tn), lambda o, j: (o, 0, j)),
        compiler_params=_cparams(("parallel", "parallel"), n2 * l * 2 + 2 * (2 * l * tn * 4 + n2 * tn * 4)
                                 + 3 * n2 * tn * 4),
        name="filter_spectrum",
    )(fwd, filt, filt)


def _conv_fwd_kernel(f_ref, y_ref, k_ref, cw_ref, cb_ref, p_ref, *, short_conv, n_split):
    l = y_ref.shape[1]
    y = y_ref[0]
    if short_conv:
        y = _short_conv(y, cw_ref[...], cb_ref[...])
    yb = y.astype(BF16)
    rows = l // n_split
    for s in range(n_split):
        lo, hi = s * rows, (s + 1) * rows
        ure = jnp.dot(f_ref[lo:hi, :], yb, preferred_element_type=F32)
        uim = jnp.dot(f_ref[l + lo:l + hi, :], yb, preferred_element_type=F32)
        kre = k_ref[0, lo:hi, :]
        kim = k_ref[0, l + lo:l + hi, :]
        pre = ure * kre - uim * kim
        pim = ure * kim + uim * kre
        if s == 0:
            first = lax.broadcasted_iota(jnp.int32, (rows, 1), 0) == 0
            pre = jnp.where(first, ure * kre, pre)
            pim = jnp.where(first, uim * kim, pim)
        p_ref[0, lo:hi, :] = pre.astype(p_ref.dtype)
        p_ref[0, l + lo:l + hi, :] = pim.astype(p_ref.dtype)


def _conv_fwd(fwd, y, y_col0, kspec, order, conv_w, conv_b, short_conv, d, tn=256):
    b, l, _ = y.shape
    n2 = fwd.shape[0]
    off = y_col0 // tn
    est = n2 * l * 2 + 2 * (l * tn * 4 + n2 * tn * 4 + n2 * tn * 2) + 6 * l * tn * 4
    return pl.pallas_call(
        functools.partial(_conv_fwd_kernel, short_conv=short_conv, n_split=2),
        out_shape=jax.ShapeDtypeStruct((b, n2, d), BF16),
        grid=(d // tn, b),
        in_specs=[pl.BlockSpec((n2, l), lambda j, bi: (0, 0), pipeline_mode=pl.Buffered(1)),
                  pl.BlockSpec((1, l, tn), lambda j, bi: (bi, 0, j + off)),
                  pl.BlockSpec((1, n2, tn), lambda j, bi: (order, 0, j)),
                  pl.BlockSpec((3, tn), lambda j, bi: (0, j + off)),
                  pl.BlockSpec((1, tn), lambda j, bi: (0, j + off))],
        out_specs=pl.BlockSpec((1, n2, tn), lambda j, bi: (bi, 0, j)),
        compiler_params=_cparams(("parallel", "parallel"), est),
        name="hyena_conv_fwd",
    )(fwd, y, kspec, conv_w, conv_b)


def _conv_inv_kernel(ft_ref, p_ref, yp_ref, gt_ref, cwy_ref, cby_ref, cwg_ref, cbg_ref, sk_ref, o_ref,
                     *, short_conv_prev):
    conv = jnp.dot(ft_ref[...], p_ref[0], preferred_element_type=F32)
    yp = yp_ref[0]
    if short_conv_prev:
        yp = _short_conv(yp, cwy_ref[...], cby_ref[...])
    gate = _short_conv(gt_ref[0], cwg_ref[...], cbg_ref[...])
    o_ref[0] = (gate * (conv + sk_ref[...] * yp)).astype(o_ref.dtype)


def _conv_inv(finv, p, yprev, yprev_col0, short_conv_prev, z, gate_col0, conv_w, conv_b, skip, out_dtype, tn=256):
    b, n2, d = p.shape
    l = n2 // 2
    offy = yprev_col0 // tn
    offg = gate_col0 // tn
    est = l * n2 * 2 + 2 * (n2 * tn * 2 + 3 * l * tn * 4) + 6 * l * tn * 4
    return pl.pallas_call(
        functools.partial(_conv_inv_kernel, short_conv_prev=short_conv_prev),
        out_shape=jax.ShapeDtypeStruct((b, l, d), out_dtype),
        grid=(d // tn, b),
        in_specs=[pl.BlockSpec((l, n2), lambda j, bi: (0, 0), pipeline_mode=pl.Buffered(1)),
                  pl.BlockSpec((1, n2, tn), lambda j, bi: (bi, 0, j)),
                  pl.BlockSpec((1, l, tn), lambda j, bi: (bi, 0, j + offy)),
                  pl.BlockSpec((1, l, tn), lambda j, bi: (bi, 0, j + offg)),
                  pl.BlockSpec((3, tn), lambda j, bi: (0, j + offy)),
                  pl.BlockSpec((1, tn), lambda j, bi: (0, j + offy)),
                  pl.BlockSpec((3, tn), lambda j, bi: (0, j + offg)),
                  pl.BlockSpec((1, tn), lambda j, bi: (0, j + offg)),
                  pl.BlockSpec((1, tn), lambda j, bi: (0, j))],
        out_specs=pl.BlockSpec((1, l, tn), lambda j, bi: (bi, 0, j)),
        compiler_params=_cparams(("parallel", "parallel"), est),
        name="hyena_conv_inv",
    )(finv, p, yprev, z, conv_w, conv_b, conv_w, conv_b, skip)


def _hyena_mixer(h2, b, l, in_w, in_b, conv_w, conv_b, f_w1, f_w23, f_w4, f_b, f_freq, skip, out_w, out_b):
    d = h2.shape[1]
    z = _matmul(h2, in_w.astype(BF16), bias=in_b, name="hyena_in").reshape(b, l, 3 * d)
    filt = _hyena_filters(l, d, f_w1, f_w23, f_w4, f_b, f_freq)
    fwd, finv = _dft_matrices(l)
    kspec = _spectrum(fwd, filt, d)
    skip = skip.astype(F32)
    p0 = _conv_fwd(fwd, z, 0, kspec, 0, conv_w, conv_b.reshape(1, -1), True, d)
    y1 = _conv_inv(finv, p0, z, 0, True, z, d, conv_w, conv_b.reshape(1, -1), skip[0:1], F32)
    p1 = _conv_fwd(fwd, y1, 0, kspec, 1, conv_w, conv_b.reshape(1, -1), False, d)
    y2 = _conv_inv(finv, p1, y1, 0, False, z, 2 * d, conv_w, conv_b.reshape(1, -1), skip[1:2], BF16)
    return y2.reshape(b * l, d), out_w.astype(BF16), out_b


def _lora_in(w):
    pad = lambda m: jnp.pad(m, ((0, 0), (0, LANES - m.shape[1])))
    return jnp.concatenate([pad(w[0]), pad(w[1])], axis=1)


def _lora_out(w):
    return jnp.pad(w, ((0, 0), (0, LANES - w.shape[1]), (0, 0)))


def _rwkv_mixer(x, ctx, mod3, layer, g0, mu, w_r, w_k, w_v, w_o, dec_w0, dec_w1, dec_w2, a0, a1, a2,
                g1, g2, k_k, k_a, r_k, lnx_w, lnx_b):
    b, l, d = x.shape
    n_ctx = ctx.shape[1]
    s = n_ctx + l
    bf = lambda w: w.astype(BF16)
    assert dec_w1.shape[2] <= LANES and a1.shape[2] <= LANES
    xr, xk, xv, lw1, a1o, g1o = _pre_rwkv(x, ctx, mod3, layer, g0, mu, bf(_lora_in(dec_w1)), bf(_lora_in(a1)),
                                          bf(g1))
    r = _matmul(xr.reshape(b * s, d), bf(w_r), out_dtype=BF16, name="rwkv_r").reshape(b, s, d)
    k = _matmul(xk.reshape(b * s, d), bf(w_k), out_dtype=BF16, name="rwkv_k").reshape(b, s, d)
    v = _matmul(xv.reshape(b * s, d), bf(w_v), out_dtype=BF16, name="rwkv_v").reshape(b, s, d)
    zz = _wkv(r, k, v, lw1, a1o, g1o, bf(_lora_out(dec_w2)), bf(_lora_out(a2)), bf(g2), dec_w0, a0, k_k, k_a, r_k,
              lnx_w, lnx_b, n_ctx)
    return zz.reshape(b * l, d), bf(w_o), jnp.zeros((d,), F32)


def kernel(x, c, ctx, c_ctx, ada_w, ada_b, norm_g, mlp_up, mlp_down, rw_mu, rw_w_r, rw_w_k, rw_w_v, rw_w_o, rw_dec_w0, rw_dec_w1, rw_dec_w2, rw_a0, rw_a1, rw_a2, rw_g1, rw_g2, rw_k_k, rw_k_a, rw_r_k, rw_lnx_w, rw_lnx_b, hy_in_w, hy_in_b, hy_conv_w, hy_conv_b, hy_f_w1, hy_f_w23, hy_f_w4, hy_f_b, hy_f_freq, hy_skip, hy_out_w, hy_out_b):
    b, l, d = x.shape
    depth = ada_w.shape[0]
    assert b < MOD_ROWS
    c_rows = jnp.zeros((MOD_ROWS, d), F32).at[:b].set(c).at[b].set(c_ctx)
    mod3 = _ada_mod(c_rows, ada_w, ada_b)
    x2 = x.reshape(b * l, d)
    xc = ctx
    h_next = None
    mlp_up_b = mlp_up.astype(BF16)
    mlp_down_b = mlp_down.astype(BF16)
    for i in range(depth):
        kind, j = i % 2, i // 2
        ctx_live = any(q % 2 == 0 for q in range(i + 1, depth))
        assert not ctx_live, "context-stream update is not implemented for this depth"
        if kind == 0:
            mixed = _rwkv_mixer(x2.reshape(b, l, d), xc, mod3, i, norm_g[i, 0], rw_mu[j], rw_w_r[j], rw_w_k[j],
                                rw_w_v[j], rw_w_o[j], rw_dec_w0[j], rw_dec_w1[j], rw_dec_w2[j], rw_a0[j],
                                rw_a1[j], rw_a2[j], rw_g1[j], rw_g2[j], rw_k_k[j], rw_k_a[j], rw_r_k[j],
                                rw_lnx_w[j], rw_lnx_b[j])
        else:
            h = h_next if h_next is not None else _pre_norm(x2, mod3, i, norm_g[i, 0], l)
            mixed = _hyena_mixer(h, b, l, hy_in_w[j], hy_in_b[j], hy_conv_w[j], hy_conv_b[j], hy_f_w1[j],
                                 hy_f_w23[j], hy_f_w4[j], hy_f_b[j], hy_f_freq[j], hy_skip[j], hy_out_w[j],
                                 hy_out_b[j])
        x2, h2 = _proj_post(*mixed, x2, mod3, i, norm_g[i, 1], norm_g[i, 2], l,
                            name="rwkv_out_post" if kind == 0 else "hyena_out_post")
        next_is_hyena = i + 1 < depth and (i + 1) % 2 == 1
        x2, h_next = _mlp(h2, mlp_up_b, mlp_down_b, x2, mod3, i, norm_g[i, 3], l,
                          next_g0=norm_g[i + 1, 0] if next_is_hyena else None)
    return x2.reshape(b, l, d)
```

```python
import functools
import math

import jax
import jax.numpy as jnp
from jax import lax
from jax.experimental import pallas as pl
from jax.experimental.pallas import tpu as pltpu

F32 = jnp.float32
BF16 = jnp.bfloat16

HEAD_SIZE = 64
GRID_W = 64
N_MOD = 6
NORM_EPS = 1e-6
LNX_EPS = 64e-5
HY_FAST_DECAY = 0.3
HY_SLOW_DECAY = 1.5
HY_DECAY_TARGET = 1e-2
HY_EMB_DIM = 33
MOD_ROWS = 16
CHUNK = 64
LANES = 128
TOKEN_TILE = 256
VMEM_LIMIT_CAP = 60000 * 1024


def _cparams(sem, est_bytes):
    limit = int(min(max(2 * est_bytes, 32 * 1024 * 1024), VMEM_LIMIT_CAP))
    return pltpu.CompilerParams(dimension_semantics=sem, vmem_limit_bytes=limit)


def _largest_tile(n, cap, align):
    t = min(cap, n) // align * align
    while t > align and n % t:
        t -= align
    assert t > 0 and n % t == 0, (n, cap, align)
    return t


def _split2(x):
    hi = x.astype(BF16)
    return hi, (x - hi.astype(F32)).astype(BF16)


def _rms(x):
    return x * lax.rsqrt(jnp.mean(x * x, axis=-1, keepdims=True) + NORM_EPS)


def _norm_mod(x, g, shift, scale):
    return (_rms(x) * g) * (1.0 + scale) + shift


def _softplus(y):
    return jnp.maximum(y, 0.0) + jnp.log1p(jnp.exp(-jnp.abs(y)))


def _ada_kernel(c_ref, w_ref, b_ref, o_ref):
    c = c_ref[...]
    s_hi, s_lo = _split2(c * jax.nn.sigmoid(c))
    w_hi, w_lo = _split2(w_ref[0])
    p = jnp.dot(jnp.concatenate([s_hi, s_lo], axis=0), w_hi, preferred_element_type=F32)
    q = jnp.dot(s_hi, w_lo, preferred_element_type=F32)
    o_ref[0] = p[:MOD_ROWS] + p[MOD_ROWS:] + q + b_ref[0]


def _ada_mod(c_rows, ada_w, ada_b):
    depth, d, n = ada_w.shape
    tn = 1024
    out = pl.pallas_call(
        _ada_kernel,
        out_shape=jax.ShapeDtypeStruct((depth, MOD_ROWS, n), F32),
        grid=(depth, n // tn),
        in_specs=[pl.BlockSpec((MOD_ROWS, d), lambda l, j: (0, 0)),
                  pl.BlockSpec((1, d, tn), lambda l, j: (l, 0, j)),
                  pl.BlockSpec((1, 1, tn), lambda l, j: (l, 0, j))],
        out_specs=pl.BlockSpec((1, MOD_ROWS, tn), lambda l, j: (l, 0, j)),
        compiler_params=_cparams(("parallel", "parallel"), 2 * d * tn * 4 + 4 * d * tn),
        name="ada_mod",
    )(c_rows, ada_w, ada_b.reshape(depth, 1, n))
    return out.reshape(depth * MOD_ROWS * N_MOD, 1, d)


def _mod_row(layer, row, j):
    return (layer * MOD_ROWS + row) * N_MOD + j


def _pre_rwkv_kernel(hc_ref, sc_ref_ctx, xm_ref, xu_ref, xd_ref, g_ref, sh_ref, sc_ref, mu_ref,
                     wdec_ref, wa_ref, wg_ref, xr_ref, xk_ref, xv_ref, lw1_ref, a1_ref, g1_ref,
                     mw_scr, ma_scr, mg_scr, *, n_tiles):
    t = pl.program_id(1)
    d = xm_ref.shape[-1]
    g = g_ref[...]
    mu = mu_ref[...]
    big = {0: xr_ref, 2: xk_ref, 3: xv_ref}
    small = {1: mw_scr, 4: ma_scr, 5: mg_scr}

    def emit(h, s, lo, hi):
        xx = s - h
        for j in range(6):
            mix = (h + xx * mu[j:j + 1, lo:hi]).astype(BF16)
            if j in big:
                big[j][0, :, lo:hi] = mix
            else:
                small[j][:, lo:hi] = mix

    @pl.when(t == 0)
    def _():
        emit(hc_ref[0], sc_ref_ctx[0], 0, d)

    @pl.when(t > 0)
    def _():
        sh = sh_ref[0]
        sc = sc_ref[0]
        xm = xm_ref[0].reshape(TOKEN_TILE, d)
        hm = _norm_mod(xm, g, sh, sc)
        hu = _norm_mod(xu_ref[0, 0], g, sh, sc) * jnp.where(t > 1, 1.0, 0.0)
        hd = _norm_mod(xd_ref[0, 0], g, sh, sc) * jnp.where(t < n_tiles, 1.0, 0.0)
        col = lax.broadcasted_iota(jnp.int32, (TOKEN_TILE, 1), 0) & (GRID_W - 1)
        q = d // 4
        left = jnp.where(col != 0, pltpu.roll(hm[:, :q], 1, 0), 0.0)
        right = jnp.where(col != GRID_W - 1, pltpu.roll(hm[:, q:2 * q], TOKEN_TILE - 1, 0), 0.0)
        up = jnp.concatenate([hu[:, 2 * q:3 * q], hm[:TOKEN_TILE - GRID_W, 2 * q:3 * q]], axis=0)
        down = jnp.concatenate([hm[GRID_W:, 3 * q:], hd[:, 3 * q:]], axis=0)
        emit(hm[:, :q], left, 0, q)
        emit(hm[:, q:2 * q], right, q, 2 * q)
        emit(hm[:, 2 * q:3 * q], up, 2 * q, 3 * q)
        emit(hm[:, 3 * q:], down, 3 * q, d)

    lw1_ref[0] = jnp.tanh(jnp.dot(mw_scr[...], wdec_ref[...], preferred_element_type=F32)).astype(BF16)
    a1_ref[0] = jnp.dot(ma_scr[...], wa_ref[...], preferred_element_type=F32).astype(BF16)
    g1_ref[0] = jax.nn.sigmoid(jnp.dot(mg_scr[...], wg_ref[...], preferred_element_type=F32)).astype(BF16)


def _shift_seq(x):
    half = x.shape[-1] // 2
    p = jnp.pad(x, ((0, 0), (1, 1), (0, 0)))
    return jnp.concatenate([p[:, :-2, :half], p[:, 2:, half:]], axis=-1)


def _pre_rwkv(x, ctx, mod3, layer, g0, mu, w_dec1, w_a1, w_g1):
    b, l, d = x.shape
    ranks = (w_dec1.shape[1], w_a1.shape[1], w_g1.shape[1])
    n_ctx = ctx.shape[1]
    assert n_ctx == TOKEN_TILE and l % TOKEN_TILE == 0
    n_tiles = l // TOKEN_TILE
    rows_per_tile = TOKEN_TILE // GRID_W
    n_rows = l // GRID_W
    x4 = x.reshape(b, n_rows, GRID_W, d)
    s = n_ctx + l
    hc = _pre_norm(ctx.reshape(b * n_ctx, d), mod3, layer, g0, n_ctx, tm=n_ctx, fixed_row=b,
                   out_dtype=F32).reshape(b, n_ctx, d)
    sc = _shift_seq(hc)

    def mrow(j):
        return pl.BlockSpec((1, 1, d), lambda bi, t: (_mod_row(layer, bi, j), 0, 0))

    main = lambda bi, t: (bi, jnp.maximum(t - 1, 0), 0, 0)
    up = lambda bi, t: (bi, jnp.maximum((t - 1) * rows_per_tile - 1, 0), 0, 0)
    down = lambda bi, t: (bi, jnp.minimum(jnp.maximum(t, 1) * rows_per_tile, n_rows - 1), 0, 0)
    whole = lambda w: pl.BlockSpec(w.shape, lambda bi, t: (0, 0))
    outs = pl.pallas_call(
        functools.partial(_pre_rwkv_kernel, n_tiles=n_tiles),
        out_shape=[jax.ShapeDtypeStruct((b, s, d), BF16)] * 3
                  + [jax.ShapeDtypeStruct((b, s, rk), BF16) for rk in ranks],
        grid=(b, n_tiles + 1),
        in_specs=[pl.BlockSpec((1, n_ctx, d), lambda bi, t: (bi, 0, 0)),
                  pl.BlockSpec((1, n_ctx, d), lambda bi, t: (bi, 0, 0)),
                  pl.BlockSpec((1, rows_per_tile, GRID_W, d), main),
                  pl.BlockSpec((1, 1, GRID_W, d), up),
                  pl.BlockSpec((1, 1, GRID_W, d), down),
                  pl.BlockSpec((1, d), lambda bi, t: (0, 0)),
                  mrow(0), mrow(1),
                  pl.BlockSpec((6, d), lambda bi, t: (0, 0)),
                  whole(w_dec1), whole(w_a1), whole(w_g1)],
        out_specs=[pl.BlockSpec((1, TOKEN_TILE, d), lambda bi, t: (bi, t, 0))] * 3
                  + [pl.BlockSpec((1, TOKEN_TILE, rk), lambda bi, t: (bi, t, 0)) for rk in ranks],
        scratch_shapes=[pltpu.VMEM((TOKEN_TILE, d), BF16)] * 3,
        compiler_params=_cparams(("parallel", "arbitrary"),
                                 2 * (3 * TOKEN_TILE * d * 4 + 2 * GRID_W * d * 4 + 3 * TOKEN_TILE * d * 2
                                      + d * sum(ranks) * 2) + 3 * TOKEN_TILE * d * 2),
        name="rwkv_pre",
    )(hc, sc, x4, x4, x4, g0.reshape(1, d), mod3, mod3, mu, w_dec1, w_a1, w_g1)
    return outs


def _mm_kernel(a_ref, w_ref, *rest, has_bias):
    o_ref = rest[-1]
    acc = jnp.dot(a_ref[...], w_ref[...], preferred_element_type=F32)
    if has_bias:
        acc = acc + rest[0][...]
    o_ref[...] = acc.astype(o_ref.dtype)


def _matmul(a, w, bias=None, out_dtype=F32, tm=1024, tn=1024, name="matmul"):
    m, k = a.shape
    n = w.shape[1]
    tm = _largest_tile(m, tm, 8)
    tn = _largest_tile(n, tn, LANES)
    in_specs = [pl.BlockSpec((tm, k), lambda i, j: (i, 0)),
                pl.BlockSpec((k, tn), lambda i, j: (0, j))]
    args = [a, w]
    if bias is not None:
        in_specs.append(pl.BlockSpec((1, tn), lambda i, j: (0, j)))
        args.append(bias.reshape(1, n).astype(F32))
    est = 2 * (tm * k * 2 + k * tn * 2 + tm * tn * jnp.dtype(out_dtype).itemsize) + tm * tn * 4
    return pl.pallas_call(
        functools.partial(_mm_kernel, has_bias=bias is not None),
        out_shape=jax.ShapeDtypeStruct((m, n), out_dtype),
        grid=(m // tm, n // tn),
        in_specs=in_specs,
        out_specs=pl.BlockSpec((tm, tn), lambda i, j: (i, j)),
        compiler_params=_cparams(("parallel", "parallel"), est),
        name=name,
    )(*args)


def _seg_sum(x, ones_bd):
    r = x.shape[0]
    p = jnp.dot(jnp.concatenate(_split2(x), axis=0), ones_bd, preferred_element_type=F32)
    return p[:r] + p[r:]


def _bd(x, head0):
    return jnp.concatenate([jnp.where(head0, x, 0.0), jnp.where(head0, 0.0, x)], axis=0)


def _wkv_constants(rev):
    c = CHUNK
    shift = int(math.log2(c))
    head0 = lax.broadcasted_iota(jnp.int32, (1, LANES), 1) < HEAD_SIZE
    row2 = lax.broadcasted_iota(jnp.int32, (2 * c, 2 * c), 0)
    col2 = lax.broadcasted_iota(jnp.int32, (2 * c, 2 * c), 1)
    ones_bd = jnp.where((row2 >> shift) == (col2 >> shift), 1.0, 0.0).astype(BF16)
    eye = jnp.where(row2 == col2, 1.0, 0.0).astype(F32)
    rowc = lax.broadcasted_iota(jnp.int32, (c, c), 0)
    colc = lax.broadcasted_iota(jnp.int32, (c, c), 1)
    tri = jnp.where(colc >= rowc if rev else colc <= rowc, 1.0, 0.0).astype(BF16)
    tc = lax.broadcasted_iota(jnp.int32, (c, 2 * c), 0)
    sc = lax.broadcasted_iota(jnp.int32, (c, 2 * c), 1) & (c - 1)
    eye_c = jnp.where(tc == sc, 1.0, 0.0).astype(F32)
    if rev:
        tc, sc = sc, tc
    merge = tuple(((tc >> lv) == (sc >> lv) + 1) & ((tc >> (lv + 1)) == (sc >> (lv + 1))) for lv in range(shift))
    return head0, ones_bd, tri, sc < tc, sc <= tc, eye, merge, eye_c


def _each(fn, *lists):
    return [fn(*args) for args in zip(*lists)]


def _mm(a, b):
    return jnp.dot(a.astype(BF16), b.astype(BF16), preferred_element_type=F32)


def _wkv_prepare(probs, k_a, hooks=()):
    hooks = list(hooks)

    def run_hook():
        if hooks:
            hooks.pop(0)()

    revs, rs, ks, kks, bd_vs, lwxs, axs, w0s, a0s, csts = [list(t) for t in zip(*probs)]
    head0 = csts[0][0]
    eye = csts[0][5]
    c = rs[0].shape[0]
    c2 = 2 * c
    logw = _each(lambda w0, lwx: -jnp.exp(-_softplus(-(w0 + lwx)) - 0.5), w0s, lwxs)
    a = _each(lambda a0, ax: jax.nn.sigmoid(a0 + ax), a0s, axs)
    kd = _each(lambda k, a_: k * (1.0 + (a_ - 1.0) * k_a), ks, a)
    b = _each(lambda kk, a_: kk * a_, kks, a)

    def cumsum(lw, cst):
        c2_ = jnp.dot(cst[2], jnp.concatenate(_split2(lw), axis=1), preferred_element_type=F32)
        return c2_[:, :LANES] + c2_[:, LANES:]

    cum = _each(cumsum, logw, csts)
    total = _each(lambda cm, rev: cm[0:1] if rev else cm[c - 1:c], cum, revs)
    kap_t = _each(lambda kk, cm, lw: kk * jnp.exp(cm - lw), kks, cum, logw)
    r_t = _each(lambda r, cm: r * jnp.exp(cm), rs, cum)
    igam = _each(lambda cm: jnp.exp(-cm), cum)
    tail = _each(lambda t, cm: jnp.exp(t - cm), total, cum)

    bd_kap = _each(lambda x: _bd(x, head0).astype(BF16), kap_t)
    lhs = _each(lambda kap, rt: jnp.concatenate([kap, rt], axis=0).astype(BF16), kap_t, r_t)
    rhs = _each(lambda kd_, b_, ig: jnp.concatenate([_bd(kd_ * ig, head0), _bd(b_ * ig, head0)],
                                                    axis=0).astype(BF16), kd, b, igam)
    bct = _each(lambda b_, tl: _bd(b_ * tl, head0).T.astype(BF16), b, tail)
    kct = _each(lambda kd_, tl: _bd(kd_ * tl, head0).T.astype(BF16), kd, tail)
    gmat = _each(lambda l_, r_: lax.dot_general(l_, r_, (((1,), (1,)), ((), ())),
                                                preferred_element_type=F32), lhs, rhs)
    a_kb = _each(lambda g, cst: jnp.where(cst[3], g[:c, c2:], 0.0), gmat, csts)
    a_rb = _each(lambda g, cst: jnp.where(cst[4], g[c:, c2:], 0.0).astype(BF16), gmat, csts)
    a_kr = _each(lambda g, cst: jnp.concatenate([jnp.where(cst[3], g[:c, :c2], 0.0),
                                                 jnp.where(cst[4], g[c:, :c2], 0.0)], axis=0).astype(BF16),
                 gmat, csts)
    av = _each(_mm, a_kr, bd_vs)
    x0 = _each(lambda kap, av_: jnp.concatenate([kap, _bd(av_[:c], head0).astype(BF16)], axis=1), bd_kap, av)
    ov0 = _each(lambda av_: av_[c:], av)
    run_hook()

    tinv = _each(lambda akb, cst: cst[7] - jnp.where(cst[6][0], akb, 0.0), a_kb, csts)
    for lv in range(1, len(csts[0][6])):
        y = _each(lambda akb, x, cst: _mm(jnp.where(cst[6][lv], akb, 0.0), _bd(x, head0)), a_kb, tinv, csts)
        run_hook()
        tinv = _each(lambda x, y_: x - _mm(x, _bd(y_, head0)), tinv, y)
        run_hook()

    wu_c = _each(_mm, tinv, x0)
    wu = _each(lambda w_: jnp.concatenate([_bd(w_[:, :LANES], head0), _bd(w_[:, LANES:], head0)],
                                          axis=1).astype(BF16), wu_c)
    run_hook()
    rb = _each(_mm, a_rb, wu)
    run_hook()
    mn = _each(_mm, bct, wu)
    while hooks:
        run_hook()
    kv = _each(_mm, kct, bd_vs)
    out = []
    for i in range(len(probs)):
        rk = r_t[i] - rb[i][:, :LANES]
        ov = ov0[i] - rb[i][:, LANES:]
        m_mat = jnp.where(eye > 0.0, jnp.exp(total[i]), 0.0) - mn[i][:, :LANES]
        n_mat = kv[i] - mn[i][:, LANES:]
        out.append((jnp.concatenate([rk, m_mat], axis=0).astype(BF16), ov, n_mat))
    return out


def _wkv_apply(rm_lhs, ov, n_mat, h_state):
    c = ov.shape[0]
    rm = jnp.dot(rm_lhs, h_state.astype(BF16), preferred_element_type=F32)
    return rm[:c] + ov, rm[c:] + n_mat


def _wkv_kernel(r_ref, k_ref, v_ref, lw1_ref, a1_ref, g1_ref, wdec_ref, wa_ref, wg_ref,
                w0_ref, a0_ref, kk_ref, ka_ref, rk_ref, lnw_ref, lnb_ref,
                z_ref, of_scr, ob_scr, rm_scr, ov_scr, n_scr, *, n_ctx_chunks, n_chunks, unroll):
    c = CHUNK
    cst_f = _wkv_constants(False)
    cst_b = _wkv_constants(True)
    head0, ones_bd = cst_f[0], cst_f[1]

    w0f, w0b = w0_ref[0:1, :], w0_ref[1:2, :]
    a0f, a0b = a0_ref[0:1, :], a0_ref[1:2, :]
    k_k = kk_ref[...]
    k_a = ka_ref[...]

    def load(ref, rows):
        return ref[0, rows, :].astype(F32)

    def lora(x_ref, w_ref, rows, rev):
        d0 = int(rev) * LANES
        return jnp.dot(x_ref[0, rows, d0:d0 + LANES], w_ref[int(rev)], preferred_element_type=F32)

    def chunk_of(i, rev):
        if not rev:
            return i
        return jnp.where(i < n_ctx_chunks, n_ctx_chunks - 1 - i, n_chunks + n_ctx_chunks - 1 - i)

    def prepare_group(grp, hooks=()):
        probs, slots = [], []
        for u in range(unroll):
            for rev, w0, a0, cst in ((False, w0f, a0f, cst_f), (True, w0b, a0b, cst_b)):
                ci = chunk_of(grp * unroll + u, rev)
                rows = pl.ds(pl.multiple_of(ci * c, c), c)
                k = load(k_ref, rows)
                kk0 = k * k_k
                kk = kk0 * lax.rsqrt(jnp.maximum(_seg_sum(kk0 * kk0, ones_bd), 1e-24))
                bd_v = _bd(load(v_ref, rows), head0).astype(BF16)
                probs.append((rev, load(r_ref, rows), k, kk, bd_v, lora(lw1_ref, wdec_ref, rows, rev),
                              lora(a1_ref, wa_ref, rows, rev), w0, a0, cst))
                slots.append(ci + n_chunks * int(rev))
        for slot, (rm_lhs, ov, n_mat) in zip(slots, _wkv_prepare(probs, k_a, hooks)):
            rm_scr[slot] = rm_lhs
            ov_scr[slot] = ov
            n_scr[slot] = n_mat

    def state_steps(grp, state):
        def step(i):
            for rev, o_scr in ((False, of_scr), (True, ob_scr)):
                ci = chunk_of(i, rev)
                slot = ci + n_chunks * int(rev)
                o, state[int(rev)] = _wkv_apply(rm_scr[slot], ov_scr[slot], n_scr[slot], state[int(rev)])
                o_scr[pl.ds(pl.multiple_of(ci * c, c), c), :] = o
        return [functools.partial(step, grp * unroll + u) for u in range(unroll)]

    n_groups = n_chunks // unroll
    prepare_group(0)

    def body(grp, carry):
        state = list(carry)
        prepare_group(grp, state_steps(grp - 1, state))
        return tuple(state)

    zero = jnp.zeros((2 * c, LANES), F32)
    state = list(lax.fori_loop(1, n_groups, body, (zero, zero)))

    r_k = rk_ref[...]
    lnw = lnw_ref[...]
    lnb = lnb_ref[...]
    blk = TOKEN_TILE
    n_ctx = n_ctx_chunks * c
    inv_n = 1.0 / HEAD_SIZE

    def read_blocks(blocks, hooks):
        hooks = list(hooks)

        def run_hook():
            if hooks:
                hooks.pop(0)()

        rows = [pl.ds(n_ctx + j * blk, blk) for j in blocks]
        seg = lambda xs: [_seg_sum(x, ones_bd) for x in xs]
        a_f = [lora(a1_ref, wa_ref, rw, False) for rw in rows]
        a_b = [lora(a1_ref, wa_ref, rw, True) for rw in rows]
        gate = [jnp.dot(g1_ref[0, rw, :], wg_ref[...], preferred_element_type=F32) for rw in rows]
        run_hook()
        o = [of_scr[rw, :] + ob_scr[rw, :] for rw in rows]
        dev = [o_ - m_ * inv_n for o_, m_ in zip(o, seg(o))]
        run_hook()
        var = seg([d_ * d_ for d_ in dev])
        run_hook()
        kd_sum = [load(k_ref, rw) * (2.0 + (jax.nn.sigmoid(a0f + af_) + jax.nn.sigmoid(a0b + ab_) - 2.0) * k_a)
                  for rw, af_, ab_ in zip(rows, a_f, a_b)]
        rkk = seg([load(r_ref, rw) * kd_ * r_k for rw, kd_ in zip(rows, kd_sum)])
        while hooks:
            run_hook()
        for j, rw, dev_, var_, rkk_, gate_ in zip(blocks, rows, dev, var, rkk, gate):
            on = dev_ * lax.rsqrt(var_ * inv_n + LNX_EPS) * lnw + lnb
            z = (on + rkk_ * load(v_ref, rw)) * gate_
            z_ref[0, pl.ds(j * blk, blk), :] = z.astype(z_ref.dtype)

    per_blk = blk // c
    n_blk = (n_chunks - n_ctx_chunks) // per_blk
    first = lambda j: n_ctx_chunks + j * per_blk
    ready = {j: max(first(j) + per_blk - 1, n_chunks + n_ctx_chunks - 1 - first(j)) for j in range(n_blk)}
    order = sorted(range(n_blk), key=lambda j: ready[j])
    pending = [((n_groups - 1) * unroll + u, fn) for u, fn in enumerate(state_steps(n_groups - 1, state))]
    hooks_per_read = 4
    for pos in range(0, n_blk, 2):
        blocks = order[pos:pos + 2]
        need = max(ready[j] for j in blocks)
        while pending and pending[0][0] <= need:
            pending.pop(0)[1]()
        ride, pending = pending[:hooks_per_read], pending[hooks_per_read:]
        read_blocks(blocks, [fn for _, fn in ride])
    for _, fn in pending:
        fn()


def _wkv(r, k, v, lw1, a1, g1, w_dec, w_a, w_g, dec_w0, a0, k_k, k_a, r_k, lnx_w, lnx_b, n_ctx):
    b, s, d = r.shape
    l = s - n_ctx
    npair = d // LANES
    seq = lambda bi, p: (bi, 0, p)
    vec = lambda bi, p: (0, p)
    sblk = pl.BlockSpec((1, s, LANES), seq)
    rank_blk = pl.BlockSpec((1, s, 2 * LANES), lambda bi, p: (bi, 0, 0))
    up_blk = pl.BlockSpec((2, LANES, LANES), lambda bi, p: (0, 0, p))
    n_chunks = s // CHUNK
    unroll = next(u for u in (12, 9, 6, 4, 2, 1) if n_chunks % u == 0)
    scratch = [pltpu.VMEM((s, LANES), F32), pltpu.VMEM((s, LANES), F32),
               pltpu.VMEM((2 * n_chunks, 3 * CHUNK, LANES), BF16),
               pltpu.VMEM((2 * n_chunks, CHUNK, LANES), F32),
               pltpu.VMEM((2 * n_chunks, 2 * CHUNK, LANES), F32)]
    est = (2 * (3 * s * LANES * 2 + 2 * s * 2 * LANES * 2 + s * g1.shape[2] * 2 + l * LANES * 2) + 2 * s * LANES * 4
           + 2 * n_chunks * CHUNK * LANES * (3 * 2 + 4 + 2 * 4))
    return pl.pallas_call(
        functools.partial(_wkv_kernel, n_ctx_chunks=n_ctx // CHUNK, n_chunks=n_chunks, unroll=unroll),
        out_shape=jax.ShapeDtypeStruct((b, l, d), BF16),
        grid=(b, npair),
        in_specs=[sblk, sblk, sblk, rank_blk, rank_blk,
                  pl.BlockSpec((1, s, g1.shape[2]), lambda bi, p: (bi, 0, 0)), up_blk, up_blk,
                  pl.BlockSpec((g1.shape[2], LANES), lambda bi, p: (0, p)),
                  pl.BlockSpec((2, LANES), vec), pl.BlockSpec((2, LANES), vec),
                  pl.BlockSpec((1, LANES), vec), pl.BlockSpec((1, LANES), vec), pl.BlockSpec((1, LANES), vec),
                  pl.BlockSpec((1, LANES), vec), pl.BlockSpec((1, LANES), vec)],
        out_specs=pl.BlockSpec((1, l, LANES), seq),
        scratch_shapes=scratch,
        compiler_params=_cparams(("parallel", "parallel"), est),
        name="wkv_scan",
    )(r, k, v, lw1, a1, g1, w_dec, w_a, w_g, dec_w0, a0, k_k.reshape(1, d), k_a.reshape(1, d), r_k.reshape(1, d),
      lnx_w.reshape(1, d), lnx_b.reshape(1, d))


def _proj_post_kernel(a_ref, w_ref, b_ref, x_ref, g1_ref, g2_ref, gate_ref, sh_ref, sc_ref, xo_ref, h_ref):
    y = jnp.dot(a_ref[...], w_ref[...], preferred_element_type=F32) + b_ref[...]
    x = x_ref[...] + gate_ref[0] * (_rms(y) * g1_ref[...])
    xo_ref[...] = x
    h_ref[...] = _norm_mod(x, g2_ref[...], sh_ref[0], sc_ref[0]).astype(h_ref.dtype)


def _proj_post(a, w, bias, x2, mod3, layer, g1, g2, l, tm=512, name="proj_post"):
    m, k = a.shape
    d = w.shape[1]
    per_b = l // tm
    row = lambda j: pl.BlockSpec((1, 1, d), lambda i: (_mod_row(layer, i // per_b, j), 0, 0))
    tile = pl.BlockSpec((tm, d), lambda i: (i, 0))
    vec = pl.BlockSpec((1, d), lambda i: (0, 0))
    est = k * d * 2 + 2 * (tm * k * 2 + tm * d * (4 + 4 + 2)) + 2 * tm * d * 4
    return pl.pallas_call(
        _proj_post_kernel,
        out_shape=[jax.ShapeDtypeStruct((m, d), F32), jax.ShapeDtypeStruct((m, d), BF16)],
        grid=(m // tm,),
        in_specs=[pl.BlockSpec((tm, k), lambda i: (i, 0)),
                  pl.BlockSpec((k, d), lambda i: (0, 0), pipeline_mode=pl.Buffered(1)),
                  vec, tile, vec, vec, row(2), row(3), row(4)],
        out_specs=[tile, tile],
        compiler_params=_cparams(("parallel",), est),
        name=name,
    )(a, w, bias.reshape(1, d).astype(F32), x2, g1.reshape(1, d), g2.reshape(1, d), mod3, mod3, mod3)


def _mlp_kernel(h_ref, wu_ref, wd_ref, x_ref, g_ref, gate_ref, *rest, with_next):
    if with_next:
        gn_ref, shn_ref, scn_ref, o_ref, hn_ref, acc_ref = rest
    else:
        o_ref, acc_ref = rest
    kf = pl.program_id(1)

    @pl.when(kf == 0)
    def _():
        acc_ref[...] = jnp.zeros_like(acc_ref)

    u = jnp.dot(h_ref[...], wu_ref[0], preferred_element_type=F32)
    u = jnp.square(jnp.maximum(u, 0.0)).astype(BF16)
    acc_ref[...] += jnp.dot(u, wd_ref[0], preferred_element_type=F32)

    @pl.when(kf == pl.num_programs(1) - 1)
    def _():
        x = x_ref[...] + gate_ref[0] * (_rms(acc_ref[...]) * g_ref[...])
        o_ref[...] = x
        if with_next:
            hn_ref[...] = _norm_mod(x, gn_ref[...], shn_ref[0], scn_ref[0]).astype(hn_ref.dtype)


def _mlp(h2, w_up, w_down, x2, mod3, layer, g3, l, next_g0=None, tm=512, tf=1024):
    m, d = h2.shape
    dff = w_up.shape[2]
    per_b = l // tm
    with_next = next_g0 is not None
    row = lambda lay, j: pl.BlockSpec((1, 1, d), lambda i, f: (_mod_row(lay, i // per_b, j), 0, 0))
    tile = pl.BlockSpec((tm, d), lambda i, f: (i, 0))
    vec = pl.BlockSpec((1, d), lambda i, f: (0, 0))
    in_specs = [tile, pl.BlockSpec((1, d, tf), lambda i, f: (layer, 0, f)),
                pl.BlockSpec((1, tf, d), lambda i, f: (layer, f, 0)), tile, vec, row(layer, 5)]
    args = [h2, w_up, w_down, x2, g3.reshape(1, d), mod3]
    out_shape = [jax.ShapeDtypeStruct((m, d), F32)]
    if with_next:
        in_specs += [vec, row(layer + 1, 0), row(layer + 1, 1)]
        args += [next_g0.reshape(1, d), mod3, mod3]
        out_shape.append(jax.ShapeDtypeStruct((m, d), BF16))
    est = 2 * (tm * d * 2 + 2 * d * tf * 2 + 2 * tm * d * 4 + tm * d * 2) + tm * d * 4 + tm * tf * 6
    outs = pl.pallas_call(
        functools.partial(_mlp_kernel, with_next=with_next),
        out_shape=out_shape,
        grid=(m // tm, dff // tf),
        in_specs=in_specs,
        out_specs=[tile] * len(out_shape),
        scratch_shapes=[pltpu.VMEM((tm, d), F32)],
        compiler_params=_cparams(("parallel", "arbitrary"), est),
        name="mlp",
    )(*args)
    return outs if with_next else (outs[0], None)


def _pre_norm_kernel(x_ref, g_ref, sh_ref, sc_ref, h_ref):
    h_ref[...] = _norm_mod(x_ref[...], g_ref[...], sh_ref[0], sc_ref[0]).astype(h_ref.dtype)


def _pre_norm(x2, mod3, layer, g0, l, tm=512, fixed_row=None, out_dtype=BF16):
    m, d = x2.shape
    per_b = l // tm
    if fixed_row is None:
        row = lambda j: pl.BlockSpec((1, 1, d), lambda i: (_mod_row(layer, i // per_b, j), 0, 0))
    else:
        row = lambda j: pl.BlockSpec((1, 1, d), lambda i: (_mod_row(layer, fixed_row, j), 0, 0))
    tile = pl.BlockSpec((tm, d), lambda i: (i, 0))
    return pl.pallas_call(
        _pre_norm_kernel,
        out_shape=jax.ShapeDtypeStruct((m, d), out_dtype),
        grid=(m // tm,),
        in_specs=[tile, pl.BlockSpec((1, d), lambda i: (0, 0)), row(0), row(1)],
        out_specs=tile,
        compiler_params=_cparams(("parallel",), 2 * tm * d * 6),
        name="pre_norm",
    )(x2, g0.reshape(1, d), mod3, mod3)


def _filter_kernel(z_ref, w1_ref, w2_ref, w3_ref, b_ref, fr_ref, w4_ref, t_ref, dl_ref, o_ref, hid_ref):
    hp = lax.Precision.HIGHEST

    @pl.when(pl.program_id(0) == 0)
    def _():
        b = b_ref[...]
        fr = fr_ref[...]
        z = jnp.sin(fr[0:1] * (jnp.dot(z_ref[...], w1_ref[...], precision=hp, preferred_element_type=F32) + b[0:1]))
        z = jnp.sin(fr[1:2] * (jnp.dot(z, w2_ref[...], precision=hp, preferred_element_type=F32) + b[1:2]))
        hid_ref[...] = jnp.sin(fr[2:3] * (jnp.dot(z, w3_ref[...], precision=hp, preferred_element_type=F32)
                                          + b[2:3]))

    filt = jnp.dot(hid_ref[...], w4_ref[...], precision=hp, preferred_element_type=F32)
    o_ref[...] = filt * jnp.exp(-t_ref[...] * dl_ref[...])


def _hyena_filters(l, d, f_w1, f_w23, f_w4, f_b, f_freq):
    t = jnp.linspace(0.0, 1.0, l, dtype=F32)[:, None]
    bands = (HY_EMB_DIM - 1) // 2
    freqs = jnp.linspace(1e-4, bands - 1, bands, dtype=F32)[None, :]
    ang = (2.0 * math.pi / l) * jnp.arange(l, dtype=F32)[:, None] * freqs
    z = jnp.concatenate([t, jnp.cos(ang), -jnp.sin(ang)], axis=-1)
    pad = lambda a_, r, c: jnp.pad(a_.astype(F32), ((0, r - a_.shape[0]), (0, c - a_.shape[1])))
    zp = pad(z, l, LANES)
    w1 = pad(f_w1, LANES, LANES)
    w2 = pad(f_w23[0], LANES, LANES)
    w3 = pad(f_w23[1], LANES, LANES)
    bb = pad(f_b, 8, LANES)
    fr = pad(f_freq, 8, LANES)
    n = f_w4.shape[1]
    w4 = pad(f_w4, LANES, n)
    max_decay = math.log(HY_DECAY_TARGET) / HY_FAST_DECAY
    min_decay = math.log(HY_DECAY_TARGET) / HY_SLOW_DECAY
    deltas = jnp.abs(jnp.linspace(min_decay, max_decay, d, dtype=F32))[None, :]
    tn = _largest_tile(d, 1024, LANES)
    per_d = d // tn
    sq = pl.BlockSpec((LANES, LANES), lambda j: (0, 0))
    small = pl.BlockSpec((8, LANES), lambda j: (0, 0))
    return pl.pallas_call(
        _filter_kernel,
        out_shape=jax.ShapeDtypeStruct((l, n), F32),
        grid=(n // tn,),
        in_specs=[pl.BlockSpec((l, LANES), lambda j: (0, 0)), sq, sq, sq, small, small,
                  pl.BlockSpec((LANES, tn), lambda j: (0, j)),
                  pl.BlockSpec((l, 1), lambda j: (0, 0)),
                  pl.BlockSpec((1, tn), lambda j: (0, j % per_d))],
        out_specs=pl.BlockSpec((l, tn), lambda j: (0, j)),
        scratch_shapes=[pltpu.VMEM((l, LANES), F32)],
        compiler_params=_cparams(("arbitrary",), 4 * l * tn * 4),
        name="hyena_filters",
    )(zp, w1, w2, w3, bb, fr, w4, t, deltas)


def _dft_matrices(l):
    n = 2 * l
    step = GRID_W if l % GRID_W == 0 else 1
    k = jnp.arange(l, dtype=jnp.int32)[:, None]
    ang_a = ((k * (step * jnp.arange(l // step, dtype=jnp.int32))[None, :]) % n).astype(F32) * (2.0 * math.pi / n)
    ang_b = ((k * jnp.arange(step, dtype=jnp.int32)[None, :]) % n).astype(F32) * (2.0 * math.pi / n)
    ca, sa = jnp.cos(ang_a)[:, :, None], jnp.sin(ang_a)[:, :, None]
    cb, sb = jnp.cos(ang_b)[:, None, :], jnp.sin(ang_b)[:, None, :]
    cos = (ca * cb - sa * sb).reshape(l, l)
    sin = (sa * cb + ca * sb).reshape(l, l)
    t = jnp.arange(l, dtype=jnp.int32)[None, :]
    msin = jnp.where(k == 0, jnp.where(t % 2 == 0, 1.0, -1.0), -sin)
    fwd = jnp.concatenate([cos, msin], axis=0)
    return fwd.astype(BF16), fwd.T.astype(BF16)


def _short_conv(z, cw, cb):
    n = z.shape[0]
    row = lax.broadcasted_iota(jnp.int32, (n, 1), 0)
    prev = jnp.where(row != 0, pltpu.roll(z, 1, 0), 0.0)
    nxt = jnp.where(row != n - 1, pltpu.roll(z, n - 1, 0), 0.0)
    return prev * cw[0:1] + z * cw[1:2] + nxt * cw[2:3] + cb


def _spec_kernel(f_ref, hf_ref, hb_ref, o_ref):
    l, tn = hf_ref.shape
    row = lax.broadcasted_iota(jnp.int32, (l, 1), 0)
    hb = jnp.where(row == 0, 0.0, hb_ref[...])
    h2 = jnp.concatenate([hf_ref[...], hb], axis=1).astype(BF16)
    u = jnp.dot(f_ref[...], h2, preferred_element_type=F32)
    row2 = lax.broadcasted_iota(jnp.int32, (2 * l, 1), 0)
    sign = jnp.where(row2 > l, -1.0, 1.0)
    scale = jnp.where((row2 == 0) | (row2 == l), 1.0 / (2 * l), 2.0 / (2 * l))
    o_ref[0] = (u[:, :tn] + sign * u[:, tn:]) * scale


def _spectrum(fwd, filt, d, tn=256):
    l = filt.shape[0]
    n2 = fwd.shape[0]
    orders = filt.shape[1] // (2 * d)
    per = d // tn
    return pl.pallas_call(
        _spec_kernel,
        out_shape=jax.ShapeDtypeStruct((orders, n2, d), F32),
        grid=(orders, per),
        in_specs=[pl.BlockSpec((n2, l), lambda o, j: (0, 0), pipeline_mode=pl.Buffered(1)),
                  pl.BlockSpec((l, tn), lambda o, j: (0, o * per + j)),
                  pl.BlockSpec((l, tn), lambda o, j: (0, (orders + o) * per + j))],
        out_specs=pl.BlockSpec((1, n2, tn), lambda o, j: (o, 0, j)),
        compiler_params=_cparams(("parallel", "parallel"), n2 * l * 2 + 2 * (2 * l * tn * 4 + n2 * tn * 4)
                                 + 3 * n2 * tn * 4),
        name="filter_spectrum",
    )(fwd, filt, filt)


def _conv_fwd_kernel(f_ref, y_ref, k_ref, cw_ref, cb_ref, p_ref, *, short_conv, n_split):
    l = y_ref.shape[1]
    y = y_ref[0]
    if short_conv:
        y = _short_conv(y, cw_ref[...], cb_ref[...])
    yb = y.astype(BF16)
    rows = l // n_split
    for s in range(n_split):
        lo, hi = s * rows, (s + 1) * rows
        ure = jnp.dot(f_ref[lo:hi, :], yb, preferred_element_type=F32)
        uim = jnp.dot(f_ref[l + lo:l + hi, :], yb, preferred_element_type=F32)
        kre = k_ref[0, lo:hi, :]
        kim = k_ref[0, l + lo:l + hi, :]
        pre = ure * kre - uim * kim
        pim = ure * kim + uim * kre
        if s == 0:
            first = lax.broadcasted_iota(jnp.int32, (rows, 1), 0) == 0
            pre = jnp.where(first, ure * kre, pre)
            pim = jnp.where(first, uim * kim, pim)
        p_ref[0, lo:hi, :] = pre.astype(p_ref.dtype)
        p_ref[0, l + lo:l + hi, :] = pim.astype(p_ref.dtype)


def _conv_fwd(fwd, y, y_col0, kspec, order, conv_w, conv_b, short_conv, d, tn=256):
    b, l, _ = y.shape
    n2 = fwd.shape[0]
    off = y_col0 // tn
    est = n2 * l * 2 + 2 * (l * tn * 4 + n2 * tn * 4 + n2 * tn * 2) + 6 * l * tn * 4
    return pl.pallas_call(
        functools.partial(_conv_fwd_kernel, short_conv=short_conv, n_split=2),
        out_shape=jax.ShapeDtypeStruct((b, n2, d), BF16),
        grid=(d // tn, b),
        in_specs=[pl.BlockSpec((n2, l), lambda j, bi: (0, 0), pipeline_mode=pl.Buffered(1)),
                  pl.BlockSpec((1, l, tn), lambda j, bi: (bi, 0, j + off)),
                  pl.BlockSpec((1, n2, tn), lambda j, bi: (order, 0, j)),
                  pl.BlockSpec((3, tn), lambda j, bi: (0, j + off)),
                  pl.BlockSpec((1, tn), lambda j, bi: (0, j + off))],
        out_specs=pl.BlockSpec((1, n2, tn), lambda j, bi: (bi, 0, j)),
        compiler_params=_cparams(("parallel", "parallel"), est),
        name="hyena_conv_fwd",
    )(fwd, y, kspec, conv_w, conv_b)


def _conv_inv_kernel(ft_ref, p_ref, yp_ref, gt_ref, cwy_ref, cby_ref, cwg_ref, cbg_ref, sk_ref, o_ref,
                     *, short_conv_prev):
    conv = jnp.dot(ft_ref[...], p_ref[0], preferred_element_type=F32)
    yp = yp_ref[0]
    if short_conv_prev:
        yp = _short_conv(yp, cwy_ref[...], cby_ref[...])
    gate = _short_conv(gt_ref[0], cwg_ref[...], cbg_ref[...])
    o_ref[0] = (gate * (conv + sk_ref[...] * yp)).astype(o_ref.dtype)


def _conv_inv(finv, p, yprev, yprev_col0, short_conv_prev, z, gate_col0, conv_w, conv_b, skip, out_dtype, tn=256):
    b, n2, d = p.shape
    l = n2 // 2
    offy = yprev_col0 // tn
    offg = gate_col0 // tn
    est = l * n2 * 2 + 2 * (n2 * tn * 2 + 3 * l * tn * 4) + 6 * l * tn * 4
    return pl.pallas_call(
        functools.partial(_conv_inv_kernel, short_conv_prev=short_conv_prev),
        out_shape=jax.ShapeDtypeStruct((b, l, d), out_dtype),
        grid=(d // tn, b),
        in_specs=[pl.BlockSpec((l, n2), lambda j, bi: (0, 0), pipeline_mode=pl.Buffered(1)),
                  pl.BlockSpec((1, n2, tn), lambda j, bi: (bi, 0, j)),
                  pl.BlockSpec((1, l, tn), lambda j, bi: (bi, 0, j + offy)),
                  pl.BlockSpec((1, l, tn), lambda j, bi: (bi, 0, j + offg)),
                  pl.BlockSpec((3, tn), lambda j, bi: (0, j + offy)),
                  pl.BlockSpec((1, tn), lambda j, bi: (0, j + offy)),
                  pl.BlockSpec((3, tn), lambda j, bi: (0, j + offg)),
                  pl.BlockSpec((1, tn), lambda j, bi: (0, j + offg)),
                  pl.BlockSpec((1, tn), lambda j, bi: (0, j))],
        out_specs=pl.BlockSpec((1, l, tn), lambda j, bi: (bi, 0, j)),
        compiler_params=_cparams(("parallel", "parallel"), est),
        name="hyena_conv_inv",
    )(finv, p, yprev, z, conv_w, conv_b, conv_w, conv_b, skip)


def _hyena_mixer(h2, b, l, in_w, in_b, conv_w, conv_b, f_w1, f_w23, f_w4, f_b, f_freq, skip, out_w, out_b):
    d = h2.shape[1]
    z = _matmul(h2, in_w.astype(BF16), bias=in_b, name="hyena_in").reshape(b, l, 3 * d)
    filt = _hyena_filters(l, d, f_w1, f_w23, f_w4, f_b, f_freq)
    fwd, finv = _dft_matrices(l)
    kspec = _spectrum(fwd, filt, d)
    skip = skip.astype(F32)
    p0 = _conv_fwd(fwd, z, 0, kspec, 0, conv_w, conv_b.reshape(1, -1), True, d)
    y1 = _conv_inv(finv, p0, z, 0, True, z, d, conv_w, conv_b.reshape(1, -1), skip[0:1], F32)
    p1 = _conv_fwd(fwd, y1, 0, kspec, 1, conv_w, conv_b.reshape(1, -1), False, d)
    y2 = _conv_inv(finv, p1, y1, 0, False, z, 2 * d, conv_w, conv_b.reshape(1, -1), skip[1:2], BF16)
    return y2.reshape(b * l, d), out_w.astype(BF16), out_b


def _lora_in(w):
    pad = lambda m: jnp.pad(m, ((0, 0), (0, LANES - m.shape[1])))
    return jnp.concatenate([pad(w[0]), pad(w[1])], axis=1)


def _lora_out(w):
    return jnp.pad(w, ((0, 0), (0, LANES - w.shape[1]), (0, 0)))


def _rwkv_mixer(x, ctx, mod3, layer, g0, mu, w_r, w_k, w_v, w_o, dec_w0, dec_w1, dec_w2, a0, a1, a2,
                g1, g2, k_k, k_a, r_k, lnx_w, lnx_b):
    b, l, d = x.shape
    n_ctx = ctx.shape[1]
    s = n_ctx + l
    bf = lambda w: w.astype(BF16)
    assert dec_w1.shape[2] <= LANES and a1.shape[2] <= LANES
    xr, xk, xv, lw1, a1o, g1o = _pre_rwkv(x, ctx, mod3, layer, g0, mu, bf(_lora_in(dec_w1)), bf(_lora_in(a1)),
                                          bf(g1))
    r = _matmul(xr.reshape(b * s, d), bf(w_r), out_dtype=BF16, name="rwkv_r").reshape(b, s, d)
    k = _matmul(xk.reshape(b * s, d), bf(w_k), out_dtype=BF16, name="rwkv_k").reshape(b, s, d)
    v = _matmul(xv.reshape(b * s, d), bf(w_v), out_dtype=BF16, name="rwkv_v").reshape(b, s, d)
    zz = _wkv(r, k, v, lw1, a1o, g1o, bf(_lora_out(dec_w2)), bf(_lora_out(a2)), bf(g2), dec_w0, a0, k_k, k_a, r_k,
              lnx_w, lnx_b, n_ctx)
    return zz.reshape(b * l, d), bf(w_o), jnp.zeros((d,), F32)


def kernel(x, c, ctx, c_ctx, ada_w, ada_b, norm_g, mlp_up, mlp_down, rw_mu, rw_w_r, rw_w_k, rw_w_v, rw_w_o, rw_dec_w0, rw_dec_w1, rw_dec_w2, rw_a0, rw_a1, rw_a2, rw_g1, rw_g2, rw_k_k, rw_k_a, rw_r_k, rw_lnx_w, rw_lnx_b, hy_in_w, hy_in_b, hy_conv_w, hy_conv_b, hy_f_w1, hy_f_w23, hy_f_w4, hy_f_b, hy_f_freq, hy_skip, hy_out_w, hy_out_b):
    b, l, d = x.shape
    depth = ada_w.shape[0]
    assert b < MOD_ROWS
    c_rows = jnp.zeros((MOD_ROWS, d), F32).at[:b].set(c).at[b].set(c_ctx)
    mod3 = _ada_mod(c_rows, ada_w, ada_b)
    x2 = x.reshape(b * l, d)
    xc = ctx
    h_next = None
    mlp_up_b = mlp_up.astype(BF16)
    mlp_down_b = mlp_down.astype(BF16)
    for i in range(depth):
        kind, j = i % 2, i // 2
        ctx_live = any(q % 2 == 0 for q in range(i + 1, depth))
        assert not ctx_live, "context-stream update is not implemented for this depth"
        if kind == 0:
            mixed = _rwkv_mixer(x2.reshape(b, l, d), xc, mod3, i, norm_g[i, 0], rw_mu[j], rw_w_r[j], rw_w_k[j],
                                rw_w_v[j], rw_w_o[j], rw_dec_w0[j], rw_dec_w1[j], rw_dec_w2[j], rw_a0[j],
                                rw_a1[j], rw_a2[j], rw_g1[j], rw_g2[j], rw_k_k[j], rw_k_a[j], rw_r_k[j],
                                rw_lnx_w[j], rw_lnx_b[j])
        else:
            h = h_next if h_next is not None else _pre_norm(x2, mod3, i, norm_g[i, 0], l)
            mixed = _hyena_mixer(h, b, l, hy_in_w[j], hy_in_b[j], hy_conv_w[j], hy_conv_b[j], hy_f_w1[j],
                                 hy_f_w23[j], hy_f_w4[j], hy_f_b[j], hy_f_freq[j], hy_skip[j], hy_out_w[j],
                                 hy_out_b[j])
        x2, h2 = _proj_post(*mixed, x2, mod3, i, norm_g[i, 1], norm_g[i, 2], l,
                            name="rwkv_out_post" if kind == 0 else "hyena_out_post")
        next_is_hyena = i + 1 < depth and (i + 1) % 2 == 1
        x2, h_next = _mlp(h2, mlp_up_b, mlp_down_b, x2, mod3, i, norm_g[i, 3], l,
                          next_g0=norm_g[i + 1, 0] if next_is_hyena else None)
    return x2.reshape(b, l, d)
```
